```python
import jax, jax.numpy as jnp
from jax import lax
import numpy as np

D_MODEL = 1024
BATCH = 2
SEQ = 8192
DEPTH = 1

HEAD_DIM = 64
N_Q_HEADS = D_MODEL // HEAD_DIM
N_KV_HEADS = N_Q_HEADS // 4
Q_PER_KV = N_Q_HEADS // N_KV_HEADS
ATTN_WIDTH = N_Q_HEADS * HEAD_DIM
KV_WIDTH = N_KV_HEADS * HEAD_DIM
WINDOW = 128
ATTN_BLOCK = 128
ROPE_THETA = 10000.0

CONV_WIDTH = D_MODEL
CONV_KERNEL = 3

N_BRANCH = 2
IN_WIDTH = ATTN_WIDTH + 2 * KV_WIDTH + 3 * CONV_WIDTH + N_BRANCH * D_MODEL

N_EXPERTS = 32
TOP_K = 4
D_FF = D_MODEL
SWIGLU_LIMIT = 7.0
SWIGLU_ALPHA = 1.702
MOE_BLOCK = 128

RMS_EPS = 1e-5

kernel_name = "hybrid_swa_shortconv_moe_block"


def rms_norm(x, g):
    xf = x.astype(jnp.float32)
    y = xf * lax.rsqrt(jnp.mean(xf * xf, axis=-1, keepdims=True) + RMS_EPS)
    return (y * g.astype(jnp.float32)).astype(x.dtype)


def split_points():
    widths = [ATTN_WIDTH, KV_WIDTH, KV_WIDTH, CONV_WIDTH, CONV_WIDTH, CONV_WIDTH, D_MODEL]
    pts, acc = [], 0
    for w in widths:
        acc += w
        pts.append(acc)
    return pts


def apply_rope(t, positions):
    half = HEAD_DIM // 2
    inv_freq = ROPE_THETA ** (-jnp.arange(half, dtype=jnp.float32) / half)
    ang = positions.astype(jnp.float32)[:, None] * inv_freq[None, :]
    cos = jnp.cos(ang)[None, :, None, :]
    sin = jnp.sin(ang)[None, :, None, :]
    t1 = t[..., :half].astype(jnp.float32)
    t2 = t[..., half:].astype(jnp.float32)
    out = jnp.concatenate([t1 * cos - t2 * sin, t2 * cos + t1 * sin], axis=-1)
    return out.astype(t.dtype)


def sliding_window_attention(q, k, v, sinks):
    B, S = q.shape[0], q.shape[1]
    nb = S // ATTN_BLOCK
    qb = q.reshape(B, nb, ATTN_BLOCK, N_KV_HEADS, Q_PER_KV, HEAD_DIM)

    def band(t):
        tp = jnp.pad(t, ((0, 0), (ATTN_BLOCK, 0), (0, 0), (0, 0)))
        tp = tp.reshape(B, nb + 1, ATTN_BLOCK, N_KV_HEADS, HEAD_DIM)
        return jnp.concatenate([tp[:, :-1], tp[:, 1:]], axis=2)

    kb, vb = band(k), band(v)
    scores = jnp.einsum('bnqkgd,bnskd->bnkgqs', qb, kb,
                        preferred_element_type=jnp.float32) * (HEAD_DIM ** -0.5)
    qi = jnp.arange(ATTN_BLOCK)[:, None]
    sj = jnp.arange(2 * ATTN_BLOCK)[None, :]
    dist = qi + ATTN_BLOCK - sj
    key_pos = jnp.arange(nb)[:, None, None] * ATTN_BLOCK - ATTN_BLOCK + sj[None]
    mask = (dist >= 0) & (dist < WINDOW) & (key_pos >= 0)
    scores = jnp.where(mask[None, :, None, None], scores, -jnp.inf)
    sink = sinks.astype(jnp.float32).reshape(N_KV_HEADS, Q_PER_KV)[None, None, :, :, None]
    m = jnp.maximum(scores.max(axis=-1), sink)
    p = jnp.exp(scores - m[..., None])
    denom = p.sum(axis=-1) + jnp.exp(sink - m)
    probs = (p / denom[..., None]).astype(v.dtype)
    out = jnp.einsum('bnkgqs,bnskd->bnqkgd', probs, vb)
    return out.reshape(B, S, ATTN_WIDTH)


def causal_short_conv(u, w):
    S = u.shape[1]
    up = jnp.pad(u, ((0, 0), (CONV_KERNEL - 1, 0), (0, 0)))
    y = w[0] * up[:, 0:S]
    for j in range(1, CONV_KERNEL):
        y = y + w[j] * up[:, j:j + S]
    return y


def moe_ffn(h, w_router, b_router, w_gate, b_gate, w_up, b_up, w_down, b_down):
    B, S, D = h.shape
    T = B * S
    xt = h.reshape(T, D)
    logits = (xt @ w_router).astype(jnp.float32) + b_router.astype(jnp.float32)
    top_logit, top_idx = lax.top_k(logits, TOP_K)
    top_w = jax.nn.softmax(top_logit, axis=-1).astype(h.dtype)
    A = T * TOP_K
    flat_e = top_idx.reshape(A)
    flat_tok = jnp.repeat(jnp.arange(T, dtype=jnp.int32), TOP_K)
    flat_w = top_w.reshape(A)
    order = jnp.argsort(flat_e)
    se, stok, sw = flat_e[order], flat_tok[order], flat_w[order]
    counts = jnp.bincount(flat_e, length=N_EXPERTS)
    padded = (counts + MOE_BLOCK - 1) // MOE_BLOCK * MOE_BLOCK
    start = jnp.cumsum(counts) - counts
    pad_end = jnp.cumsum(padded)
    pad_start = pad_end - padded
    dest = pad_start[se] + jnp.arange(A) - start[se]
    n_blocks = (A + MOE_BLOCK - 1) // MOE_BLOCK + N_EXPERTS
    P = n_blocks * MOE_BLOCK
    buf_tok = jnp.full((P,), T, jnp.int32).at[dest].set(stok)
    buf_w = jnp.zeros((P,), h.dtype).at[dest].set(sw)
    blk_e = jnp.minimum(
        jnp.searchsorted(pad_end, jnp.arange(n_blocks) * MOE_BLOCK, side='right'),
        N_EXPERTS - 1)
    x_pad = jnp.concatenate([xt, jnp.zeros((1, D), xt.dtype)], axis=0)
    xb = x_pad[buf_tok].reshape(n_blocks, MOE_BLOCK, D)

    def expert_block(args):
        xblk, e = args
        g = xblk @ w_gate[e] + b_gate[e]
        u = xblk @ w_up[e] + b_up[e]
        g = jnp.minimum(g, SWIGLU_LIMIT)
        u = jnp.clip(u, -SWIGLU_LIMIT, SWIGLU_LIMIT)
        a = g * jax.nn.sigmoid(SWIGLU_ALPHA * g) * (u + 1.0)
        return a @ w_down[e] + b_down[e]

    yb = lax.map(expert_block, (xb, blk_e))
    y = jnp.zeros((T + 1, D), h.dtype).at[buf_tok].add(yb.reshape(P, D) * buf_w[:, None])
    return y[:T].reshape(B, S, D)


def setup_inputs(seed: int = 0) -> dict:
    key = jax.random.key(seed)
    ks = jax.random.split(key, 20)
    f32 = jnp.float32
    nrm = lambda k, shape, s: jax.random.normal(k, shape, f32) * s
    L = DEPTH
    return {
        "x": nrm(ks[0], (BATCH, SEQ, D_MODEL), 1.0),
        "g_mix": 1.0 + nrm(ks[1], (L, D_MODEL), 0.02),
        "w_in": nrm(ks[2], (L, D_MODEL, IN_WIDTH), D_MODEL ** -0.5),
        "b_in": nrm(ks[3], (L, IN_WIDTH), 0.02),
        "sinks": nrm(ks[4], (L, N_Q_HEADS), 0.5),
        "w_conv": nrm(ks[5], (L, CONV_KERNEL, CONV_WIDTH), CONV_KERNEL ** -0.5),
        "w_attn_o": nrm(ks[6], (L, ATTN_WIDTH, D_MODEL), ATTN_WIDTH ** -0.5),
        "w_conv_o": nrm(ks[7], (L, CONV_WIDTH, D_MODEL), CONV_WIDTH ** -0.5),
        "w_out": nrm(ks[8], (L, D_MODEL, D_MODEL), D_MODEL ** -0.5),
        "g_ffn": 1.0 + nrm(ks[9], (L, D_MODEL), 0.02),
        "w_router": nrm(ks[10], (L, D_MODEL, N_EXPERTS), D_MODEL ** -0.5),
        "b_router": nrm(ks[11], (L, N_EXPERTS), 0.01),
        "w_gate": nrm(ks[12], (L, N_EXPERTS, D_MODEL, D_FF), D_MODEL ** -0.5),
        "b_gate": nrm(ks[13], (L, N_EXPERTS, D_FF), 0.02),
        "w_up": nrm(ks[14], (L, N_EXPERTS, D_MODEL, D_FF), D_MODEL ** -0.5),
        "b_up": nrm(ks[15], (L, N_EXPERTS, D_FF), 0.02),
        "w_down": nrm(ks[16], (L, N_EXPERTS, D_FF, D_MODEL), D_FF ** -0.5),
        "b_down": nrm(ks[17], (L, N_EXPERTS, D_MODEL), 0.02),
        "g_final": 1.0 + nrm(ks[18], (D_MODEL,), 0.02),
    }


def reference(x, g_mix, w_in, b_in, sinks, w_conv, w_attn_o, w_conv_o, w_out, g_ffn,
              w_router, b_router, w_gate, b_gate, w_up, b_up, w_down, b_down, g_final):
    B, S = x.shape[0], x.shape[1]
    positions = jnp.arange(S, dtype=jnp.int32)
    pts = split_points()
    for layer in range(DEPTH):
        h = rms_norm(x, g_mix[layer])
        proj = h @ w_in[layer] + b_in[layer]
        q, k, v, cb, cc, cx, ga, gc = jnp.split(proj, pts, axis=-1)
        q = apply_rope(q.reshape(B, S, N_Q_HEADS, HEAD_DIM), positions)
        k = apply_rope(k.reshape(B, S, N_KV_HEADS, HEAD_DIM), positions)
        v = v.reshape(B, S, N_KV_HEADS, HEAD_DIM)
        y_attn = sliding_window_attention(q, k, v, sinks[layer]) @ w_attn_o[layer]
        y_conv = (cb * causal_short_conv(cc * cx, w_conv[layer])) @ w_conv_o[layer]
        merged = jax.nn.sigmoid(ga) * y_attn + jax.nn.sigmoid(gc) * y_conv
        x = x + merged @ w_out[layer]
        h = rms_norm(x, g_ffn[layer])
        x = x + moe_ffn(h, w_router[layer], b_router[layer], w_gate[layer], b_gate[layer],
                        w_up[layer], b_up[layer], w_down[layer], b_down[layer])
    return rms_norm(x, g_final)
```

```python
import functools

import jax
import jax.numpy as jnp
from jax import lax
from jax.experimental import pallas as pl
from jax.experimental.pallas import tpu as pltpu

D_MODEL = 1024
HEAD_DIM = 64
N_Q_HEADS = 16
N_KV_HEADS = 4
Q_PER_KV = N_Q_HEADS // N_KV_HEADS
ATTN_WIDTH = N_Q_HEADS * HEAD_DIM
KV_WIDTH = N_KV_HEADS * HEAD_DIM
ATTN_BLOCK = 128
WINDOW = 128
ROPE_THETA = 10000.0
CONV_WIDTH = D_MODEL
CONV_KERNEL = 3
N_EXPERTS = 32
TOP_K = 4
SWIGLU_LIMIT = 7.0
SWIGLU_ALPHA = 1.702
RMS_EPS = 1e-5

OFF_Q = 0
OFF_K = OFF_Q + ATTN_WIDTH
OFF_V = OFF_K + KV_WIDTH
OFF_CB = OFF_V + KV_WIDTH
OFF_CC = OFF_CB + CONV_WIDTH
OFF_CX = OFF_CC + CONV_WIDTH
OFF_GA = OFF_CX + CONV_WIDTH
OFF_GC = OFF_GA + D_MODEL
IN_WIDTH = OFF_GC + D_MODEL

LANES = 128
SUBLANES = 8
VMEM_LIMIT_BYTES = 56 * 1024 * 1024

TM_INPROJ = 512
COL_CHUNK = 512
TM_MIX = 256
TC_RANK = 512
TM_EXPERT = 256
TM_MOVE = 256

BF16 = jnp.bfloat16
F32 = jnp.float32
NEG_BIG = -1e30


def _rms_scale(x, g):
    ms = jnp.mean(x * x, axis=-1, keepdims=True)
    return (x * lax.rsqrt(ms + RMS_EPS)) * g


SLABS = D_MODEL // LANES


def _store_token_major(ref, val):
    tm = val.shape[0]
    for s in range(SLABS):
        ref[pl.ds(s, tm, stride=SLABS), :] = val[:, s * LANES:(s + 1) * LANES]


def _load_token_major(ref, tm):
    return jnp.concatenate(
        [ref[pl.ds(s, tm, stride=SLABS), :] for s in range(SLABS)], axis=1)


def _inproj_body(x_ref, g_ref, w_ref, b_ref, cos_ref, sin_ref, wc_ref,
                 q_ref, k_ref, v_ref, z_ref, sa_ref, sc_ref, carry_ref, *, tiles_per_seq):
    tm = x_ref.shape[0]
    i = pl.program_id(0)
    h = _rms_scale(x_ref[...], g_ref[...]).astype(BF16)

    def proj(c0, width):
        return (jnp.dot(h, w_ref[:, c0:c0 + width], preferred_element_type=F32)
                + b_ref[:, c0:c0 + width])

    cos = cos_ref[...]
    sin = sin_ref[...]
    lane = lax.broadcasted_iota(jnp.int32, (tm, LANES), 1)
    first_half = (lane & (HEAD_DIM // 2)) == 0

    def rope(t):
        partner = jnp.where(first_half,
                            pltpu.roll(t, LANES - HEAD_DIM // 2, 1),
                            pltpu.roll(t, HEAD_DIM // 2, 1))
        return t * cos + partner * sin

    for c in range(0, ATTN_WIDTH, COL_CHUNK):
        acc = proj(OFF_Q + c, COL_CHUNK)
        for j in range(0, COL_CHUNK, LANES):
            q_ref[:, c + j:c + j + LANES] = (
                rope(acc[:, j:j + LANES]) * (HEAD_DIM ** -0.5)).astype(BF16)

    acc = proj(OFF_K, 2 * KV_WIDTH)
    for j in range(0, KV_WIDTH, LANES):
        k_ref[:, j:j + LANES] = rope(acc[:, j:j + LANES]).astype(BF16)
    v_ref[...] = acc[:, KV_WIDTH:].astype(BF16)

    seq_start = (i % tiles_per_seq) == 0
    head = 2 * SUBLANES
    row = lax.broadcasted_iota(jnp.int32, (head, COL_CHUNK), 0)
    for c in range(0, CONV_WIDTH, COL_CHUNK):
        u = proj(OFF_CC + c, COL_CHUNK) * proj(OFF_CX + c, COL_CHUNK)
        cb = proj(OFF_CB + c, COL_CHUNK)
        w0 = wc_ref[0:1, c:c + COL_CHUNK]
        w1 = wc_ref[1:2, c:c + COL_CHUNK]
        w2 = wc_ref[2:3, c:c + COL_CHUNK]
        y = w0 * pltpu.roll(u, 2, 0) + w1 * pltpu.roll(u, 1, 0) + w2 * u
        z_ref[:, c:c + COL_CHUNK] = (cb * y).astype(BF16)
        prev = jnp.where(seq_start, 0.0, carry_ref[:, c:c + COL_CHUNK])
        pad = jnp.zeros((SUBLANES, COL_CHUNK), F32)
        uh = u[0:head]
        u1 = jnp.where(row < 1, jnp.concatenate([pltpu.roll(prev, 1, 0), pad], 0),
                       pltpu.roll(uh, 1, 0))
        u2 = jnp.where(row < 2, jnp.concatenate([pltpu.roll(prev, 2, 0), pad], 0),
                       pltpu.roll(uh, 2, 0))
        yh = w0 * u2 + w1 * u1 + w2 * uh
        z_ref[0:head, c:c + COL_CHUNK] = (cb[0:head] * yh).astype(BF16)
        carry_ref[:, c:c + COL_CHUNK] = u[tm - SUBLANES:tm]

    for c in range(0, D_MODEL, COL_CHUNK):
        sa_ref[:, c:c + COL_CHUNK] = jax.nn.sigmoid(proj(OFF_GA + c, COL_CHUNK)).astype(BF16)
        sc_ref[:, c:c + COL_CHUNK] = jax.nn.sigmoid(proj(OFF_GC + c, COL_CHUNK)).astype(BF16)


def _inproj(x2, g_mix, w_in_bf, b_in, cos_t, sin_t, w_conv, seq_len):
    T = x2.shape[0]
    tm = TM_INPROJ
    tiles_per_seq = seq_len // tm
    const = lambda i: (0, 0)
    row_blk = lambda i: (i, 0)
    pos_blk = lambda i: (i % tiles_per_seq, 0)
    return pl.pallas_call(
        functools.partial(_inproj_body, tiles_per_seq=tiles_per_seq),
        grid=(T // tm,),
        in_specs=[
            pl.BlockSpec((tm, D_MODEL), row_blk),
            pl.BlockSpec((1, D_MODEL), const),
            pl.BlockSpec((D_MODEL, IN_WIDTH), const, pipeline_mode=pl.Buffered(1)),
            pl.BlockSpec((1, IN_WIDTH), const),
            pl.BlockSpec((tm, LANES), pos_blk),
            pl.BlockSpec((tm, LANES), pos_blk),
            pl.BlockSpec((CONV_KERNEL, CONV_WIDTH), const),
        ],
        out_specs=[
            pl.BlockSpec((tm, ATTN_WIDTH), row_blk),
            pl.BlockSpec((tm, KV_WIDTH), row_blk),
            pl.BlockSpec((tm, KV_WIDTH), row_blk),
            pl.BlockSpec((tm, CONV_WIDTH), row_blk),
            pl.BlockSpec((tm, D_MODEL), row_blk),
            pl.BlockSpec((tm, D_MODEL), row_blk),
        ],
        out_shape=[
            jax.ShapeDtypeStruct((T, ATTN_WIDTH), BF16),
            jax.ShapeDtypeStruct((T, KV_WIDTH), BF16),
            jax.ShapeDtypeStruct((T, KV_WIDTH), BF16),
            jax.ShapeDtypeStruct((T, CONV_WIDTH), BF16),
            jax.ShapeDtypeStruct((T, D_MODEL), BF16),
            jax.ShapeDtypeStruct((T, D_MODEL), BF16),
        ],
        scratch_shapes=[pltpu.VMEM((SUBLANES, CONV_WIDTH), F32)],
        compiler_params=pltpu.CompilerParams(
            dimension_semantics=("arbitrary",), vmem_limit_bytes=VMEM_LIMIT_BYTES),
        name="inproj",
    )(x2, g_mix, w_in_bf, b_in, cos_t, sin_t, w_conv)


def _attn_body(sinks_ref, q_ref, kp_ref, kc_ref, vp_ref, vc_ref, o_ref):
    n = pl.program_id(1)
    blk = ATTN_BLOCK
    rows = Q_PER_KV * blk
    qi = lax.broadcasted_iota(jnp.int32, (rows, 2 * blk), 0) % blk
    sj = lax.broadcasted_iota(jnp.int32, (rows, 2 * blk), 1)
    dist = qi + blk - sj
    mask = (dist >= 0) & (dist < WINDOW) & ((sj >= blk) | (n > 0))
    k_all = jnp.concatenate([kp_ref[...], kc_ref[...]], axis=0)
    v_all = jnp.concatenate([vp_ref[...], vc_ref[...]], axis=0)
    for kh in range(N_KV_HEADS):
        k_h = k_all[:, kh * HEAD_DIM:(kh + 1) * HEAD_DIM]
        v_h = v_all[:, kh * HEAD_DIM:(kh + 1) * HEAD_DIM]
        heads = [kh * Q_PER_KV + g for g in range(Q_PER_KV)]
        q_g = jnp.concatenate(
            [q_ref[:, hq * HEAD_DIM:(hq + 1) * HEAD_DIM] for hq in heads], axis=0)
        sink = jnp.concatenate(
            [jnp.full((blk, 1), sinks_ref[hq], F32) for hq in heads], axis=0)
        s = lax.dot_general(q_g, k_h, (((1,), (1,)), ((), ())), preferred_element_type=F32)
        s = jnp.where(mask, s, NEG_BIG)
        m = jnp.maximum(jnp.max(s, axis=-1, keepdims=True), sink)
        p = jnp.exp(s - m)
        denom = jnp.sum(p, axis=-1, keepdims=True) + jnp.exp(sink - m)
        o = jnp.dot(p.astype(BF16), v_h, preferred_element_type=F32) / denom
        for g, hq in enumerate(heads):
            o_ref[:, hq * HEAD_DIM:(hq + 1) * HEAD_DIM] = o[g * blk:(g + 1) * blk].astype(BF16)


def _attention(q, k, v, sinks, batch, seq_len):
    T = q.shape[0]
    nb = seq_len // ATTN_BLOCK
    cur = lambda b, n: (b * nb + n, 0)
    prev = lambda b, n: (b * nb + jnp.maximum(n - 1, 0), 0)
    return pl.pallas_call(
        _attn_body,
        grid=(batch, nb),
        in_specs=[
            pl.BlockSpec(memory_space=pltpu.SMEM),
            pl.BlockSpec((ATTN_BLOCK, ATTN_WIDTH), cur),
            pl.BlockSpec((ATTN_BLOCK, KV_WIDTH), prev),
            pl.BlockSpec((ATTN_BLOCK, KV_WIDTH), cur),
            pl.BlockSpec((ATTN_BLOCK, KV_WIDTH), prev),
            pl.BlockSpec((ATTN_BLOCK, KV_WIDTH), cur),
        ],
        out_specs=pl.BlockSpec((ATTN_BLOCK, ATTN_WIDTH), cur),
        out_shape=jax.ShapeDtypeStruct((T, ATTN_WIDTH), BF16),
        compiler_params=pltpu.CompilerParams(
            dimension_semantics=("arbitrary", "arbitrary"), vmem_limit_bytes=VMEM_LIMIT_BYTES),
        name="attn",
    )(sinks, q, k, k, v, v)


def _mixout_body(x_ref, a_ref, z_ref, sa_ref, sc_ref, wa_ref, wc_ref, wo_ref, g_ref,
                 wr_ref, br_ref, x1_ref, h_ref, idx_ref, wgt_ref):
    tm = x_ref.shape[0]
    y_attn = jnp.dot(a_ref[...], wa_ref[...], preferred_element_type=F32)
    y_conv = jnp.dot(z_ref[...], wc_ref[...], preferred_element_type=F32)
    merged = sa_ref[...].astype(F32) * y_attn + sc_ref[...].astype(F32) * y_conv
    x1 = x_ref[...] + jnp.dot(merged.astype(BF16), wo_ref[...], preferred_element_type=F32)
    x1_ref[...] = x1
    h = _rms_scale(x1, g_ref[...])
    _store_token_major(h_ref, h)
    logits = lax.dot_general(wr_ref[...], h, (((1,), (1,)), ((), ())),
                             precision=lax.Precision.HIGHEST,
                             preferred_element_type=F32) + br_ref[...]
    e_iota = lax.broadcasted_iota(jnp.int32, (N_EXPERTS, tm), 0)
    vals, idxs = [], []
    for _ in range(TOP_K):
        m = jnp.max(logits, axis=0, keepdims=True)
        idx = jnp.min(jnp.where(logits == m, e_iota, N_EXPERTS), axis=0, keepdims=True)
        vals.append(m)
        idxs.append(idx)
        logits = jnp.where(e_iota == idx, -jnp.inf, logits)
    ex = [jnp.exp(v - vals[0]) for v in vals]
    tot = ex[0] + ex[1] + ex[2] + ex[3]
    idx_ref[...] = jnp.concatenate(idxs, axis=0)
    wgt_ref[...] = jnp.concatenate([e / tot for e in ex], axis=0)


def _mixout(x2, attn, z, sa, sc, wa_bf, wc_bf, wo_bf, g_ffn, wr_t, b_router):
    T = x2.shape[0]
    tm = TM_MIX
    const = lambda i: (0, 0)
    row_blk = lambda i: (i, 0)
    col_blk = lambda i: (0, i)
    act = pl.BlockSpec((tm, D_MODEL), row_blk)
    wsq = pl.BlockSpec((D_MODEL, D_MODEL), const)
    return pl.pallas_call(
        _mixout_body,
        grid=(T // tm,),
        in_specs=[act, act, act, act, act, wsq, wsq, wsq,
                  pl.BlockSpec((1, D_MODEL), const),
                  pl.BlockSpec((N_EXPERTS, D_MODEL), const),
                  pl.BlockSpec((N_EXPERTS, 1), const)],
        out_specs=[act,
                   pl.BlockSpec((tm * SLABS, LANES), row_blk),
                   pl.BlockSpec((TOP_K, tm), col_blk),
                   pl.BlockSpec((TOP_K, tm), col_blk)],
        out_shape=[jax.ShapeDtypeStruct((T, D_MODEL), F32),
                   jax.ShapeDtypeStruct((T * SLABS, LANES), F32),
                   jax.ShapeDtypeStruct((TOP_K, T), jnp.int32),
                   jax.ShapeDtypeStruct((TOP_K, T), F32)],
        compiler_params=pltpu.CompilerParams(
            dimension_semantics=("arbitrary",), vmem_limit_bytes=VMEM_LIMIT_BYTES),
        name="mixout",
    )(x2, attn, z, sa, sc, wa_bf, wc_bf, wo_bf, g_ffn, wr_t, b_router)


def _rank_body(idx_ref, rank_ref, cnt_ref, carry_ref):
    tc = idx_ref.shape[1]
    i = pl.program_id(0)

    @pl.when(i == 0)
    def _():
        carry_ref[...] = jnp.zeros_like(carry_ref)

    idx = idx_ref[...]
    e_iota = lax.broadcasted_iota(jnp.int32, (N_EXPERTS, tc), 0)
    sel = [e_iota == idx[k:k + 1, :] for k in range(TOP_K)]
    member = (sel[0] | sel[1] | sel[2] | sel[3])
    onehot = jnp.where(member, 1.0, 0.0).astype(BF16)
    r = lax.broadcasted_iota(jnp.int32, (tc, tc), 0)
    c = lax.broadcasted_iota(jnp.int32, (tc, tc), 1)
    before = jnp.where(r < c, 1.0, 0.0).astype(BF16)
    carry = carry_ref[:, 0:1]
    prefix = jnp.dot(onehot, before, preferred_element_type=F32) + carry
    ranks = [jnp.sum(jnp.where(sel[k], prefix, 0.0), axis=0, keepdims=True)
             for k in range(TOP_K)]
    rank_ref[...] = jnp.concatenate(ranks, axis=0).astype(jnp.int32)
    total = carry + jnp.sum(onehot.astype(F32), axis=1, keepdims=True)
    carry_ref[...] = jnp.broadcast_to(total, carry_ref.shape)
    cnt_ref[...] = jnp.broadcast_to(total, cnt_ref.shape).astype(jnp.int32)


def _rank(idx_t):
    T = idx_t.shape[1]
    tc = TC_RANK
    return pl.pallas_call(
        _rank_body,
        grid=(T // tc,),
        in_specs=[pl.BlockSpec((TOP_K, tc), lambda i: (0, i))],
        out_specs=[pl.BlockSpec((TOP_K, tc), lambda i: (0, i)),
                   pl.BlockSpec((N_EXPERTS, LANES), lambda i: (0, 0))],
        out_shape=[jax.ShapeDtypeStruct((TOP_K, T), jnp.int32),
                   jax.ShapeDtypeStruct((N_EXPERTS, LANES), jnp.int32)],
        scratch_shapes=[pltpu.VMEM((N_EXPERTS, LANES), F32)],
        compiler_params=pltpu.CompilerParams(dimension_semantics=("arbitrary",)),
        name="rank",
    )(idx_t)


def _dispatch_body(dest_ref, fill_ref, h_ref, xs_ref, zero_ref, sem, *, n_tok):
    tm = h_ref.shape[0]
    i = pl.program_id(0)

    @pl.when(i == 0)
    def _():
        zero_ref[...] = jnp.zeros_like(zero_ref)
        fills = [pltpu.make_async_copy(
            zero_ref, xs_ref.at[pl.ds(fill_ref[e], TM_EXPERT)], sem) for e in range(N_EXPERTS)]
        for f in fills:
            f.start()
        for f in fills:
            f.wait()

    def row_copy(j, k):
        d = dest_ref[k * n_tok + i * tm + j]
        return pltpu.make_async_copy(h_ref.at[j], xs_ref.at[d], sem)

    def issue(j, carry):
        for k in range(TOP_K):
            row_copy(j, k).start()
        return carry

    lax.fori_loop(0, tm, issue, 0, unroll=8)

    def drain(j, carry):
        for k in range(TOP_K):
            row_copy(j, k).wait()
        return carry

    lax.fori_loop(0, tm, drain, 0, unroll=8)


def _dispatch(dest_flat, fill_start, h3, n_rows):
    T = h3.shape[0]
    tm = TM_MOVE
    return pl.pallas_call(
        functools.partial(_dispatch_body, n_tok=T),
        grid_spec=pltpu.PrefetchScalarGridSpec(
            num_scalar_prefetch=2,
            grid=(T // tm,),
            in_specs=[pl.BlockSpec((tm, SLABS, LANES), lambda i, d, f: (i, 0, 0))],
            out_specs=pl.BlockSpec(memory_space=pl.ANY),
            scratch_shapes=[pltpu.VMEM((TM_EXPERT, SLABS, LANES), F32),
                            pltpu.SemaphoreType.DMA(())],
        ),
        out_shape=jax.ShapeDtypeStruct((n_rows + TM_EXPERT, SLABS, LANES), F32),
        compiler_params=pltpu.CompilerParams(dimension_semantics=("arbitrary",)),
        name="dispatch",
    )(dest_flat, fill_start, h3)


def _expert_body(blk_e_ref, n_used_ref, xs_ref, wg_ref, bg_ref, wu_ref, bu_ref, wd_ref, bd_ref,
                 y_ref, wg_bf, wu_bf, wd_bf):
    i = pl.program_id(0)
    e = blk_e_ref[i]
    e_prev = blk_e_ref[jnp.maximum(i - 1, 0)]

    @pl.when(i < n_used_ref[0])
    def _():
        @pl.when((i == 0) | (e != e_prev))
        def _():
            wg_bf[...] = wg_ref[0].astype(BF16)
            wu_bf[...] = wu_ref[0].astype(BF16)
            wd_bf[...] = wd_ref[0].astype(BF16)

        x = _load_token_major(xs_ref, TM_EXPERT).astype(BF16)
        g = jnp.dot(x, wg_bf[...], preferred_element_type=F32) + bg_ref[0]
        u = jnp.dot(x, wu_bf[...], preferred_element_type=F32) + bu_ref[0]
        g = jnp.minimum(g, SWIGLU_LIMIT)
        u = jnp.clip(u, -SWIGLU_LIMIT, SWIGLU_LIMIT)
        a = g * jax.nn.sigmoid(SWIGLU_ALPHA * g) * (u + 1.0)
        y = jnp.dot(a.astype(BF16), wd_bf[...], preferred_element_type=F32) + bd_ref[0]
        _store_token_major(y_ref, y)


def _experts(blk_e, n_used, xs, w_gate, b_gate, w_up, b_up, w_down, b_down, n_tiles):
    tm = TM_EXPERT
    d_ff = w_gate.shape[2]
    tile = lambda i, be, nu: (jnp.minimum(i, nu[0] - 1), 0)
    wsel = lambda i, be, nu: (be[i], 0, 0)
    return pl.pallas_call(
        _expert_body,
        grid_spec=pltpu.PrefetchScalarGridSpec(
            num_scalar_prefetch=2,
            grid=(n_tiles,),
            in_specs=[
                pl.BlockSpec((tm * SLABS, LANES), tile),
                pl.BlockSpec((1, D_MODEL, d_ff), wsel),
                pl.BlockSpec((1, 1, d_ff), wsel),
                pl.BlockSpec((1, D_MODEL, d_ff), wsel),
                pl.BlockSpec((1, 1, d_ff), wsel),
                pl.BlockSpec((1, d_ff, D_MODEL), wsel),
                pl.BlockSpec((1, 1, D_MODEL), wsel),
            ],
            out_specs=pl.BlockSpec((tm * SLABS, LANES), tile),
            scratch_shapes=[pltpu.VMEM((D_MODEL, d_ff), BF16),
                            pltpu.VMEM((D_MODEL, d_ff), BF16),
                            pltpu.VMEM((d_ff, D_MODEL), BF16)],
        ),
        out_shape=jax.ShapeDtypeStruct((n_tiles * tm * SLABS, LANES), F32),
        compiler_params=pltpu.CompilerParams(
            dimension_semantics=("arbitrary",), vmem_limit_bytes=VMEM_LIMIT_BYTES),
        name="experts",
    )(blk_e, n_used, xs, w_gate, b_gate, w_up, b_up, w_down, b_down)


def _combine_body(dest_ref, y_ref, x1_ref, wgt_ref, g_ref, o_ref, buf_ref, sem, *, n_tok):
    tm = x1_ref.shape[0]
    i = pl.program_id(0)

    def row_copy(j, k):
        d = dest_ref[k * n_tok + i * tm + j]
        slab = pl.ds(pl.multiple_of(j * SLABS, SLABS), SLABS)
        return pltpu.make_async_copy(y_ref.at[d], buf_ref.at[k, slab], sem)

    def issue(j, carry):
        for k in range(TOP_K):
            row_copy(j, k).start()
        return carry

    lax.fori_loop(0, tm, issue, 0, unroll=8)

    def drain(j, carry):
        for k in range(TOP_K):
            row_copy(j, k).wait()
        return carry

    lax.fori_loop(0, tm, drain, 0, unroll=8)

    wgt = wgt_ref[...]
    acc = x1_ref[...]
    for k in range(TOP_K):
        rows = _load_token_major(buf_ref.at[k], tm)
        acc = acc + wgt[:, k:k + 1] * rows
    o_ref[...] = _rms_scale(acc, g_ref[...])


def _combine(dest_flat, yb, x1, wgt_rows, g_final):
    T = x1.shape[0]
    tm = TM_MOVE
    return pl.pallas_call(
        functools.partial(_combine_body, n_tok=T),
        grid_spec=pltpu.PrefetchScalarGridSpec(
            num_scalar_prefetch=1,
            grid=(T // tm,),
            in_specs=[
                pl.BlockSpec(memory_space=pl.ANY),
                pl.BlockSpec((tm, D_MODEL), lambda i, d: (i, 0)),
                pl.BlockSpec((tm, TOP_K), lambda i, d: (i, 0)),
                pl.BlockSpec((1, D_MODEL), lambda i, d: (0, 0)),
            ],
            out_specs=pl.BlockSpec((tm, D_MODEL), lambda i, d: (i, 0)),
            scratch_shapes=[pltpu.VMEM((TOP_K, tm * SLABS, LANES), F32),
                            pltpu.SemaphoreType.DMA(())],
        ),
        out_shape=jax.ShapeDtypeStruct((T, D_MODEL), F32),
        compiler_params=pltpu.CompilerParams(dimension_semantics=("arbitrary",)),
        name="combine",
    )(dest_flat, yb, x1, wgt_rows, g_final)


def _rope_tables(seq_len):
    half = HEAD_DIM // 2
    inv_freq = ROPE_THETA ** (-jnp.arange(half, dtype=F32) / half)
    ang = jnp.arange(seq_len, dtype=jnp.int32).astype(F32)[:, None] * inv_freq[None, :]
    cos = jnp.cos(ang)
    sin = jnp.sin(ang)
    reps = LANES // HEAD_DIM
    cos_t = jnp.tile(jnp.concatenate([cos, cos], axis=-1), (1, reps))
    sin_t = jnp.tile(jnp.concatenate([-sin, sin], axis=-1), (1, reps))
    return cos_t, sin_t


def _layer(x2, batch, seq_len, g_mix, w_in, b_in, sinks, w_conv, w_attn_o, w_conv_o, w_out,
           g_ffn, w_router, b_router, w_gate, b_gate, w_up, b_up, w_down, b_down, g_out):
    T = x2.shape[0]
    cos_t, sin_t = _rope_tables(seq_len)
    q, k, v, z, sa, sc = _inproj(x2, g_mix[None, :], w_in.astype(BF16), b_in[None, :],
                                 cos_t, sin_t, w_conv, seq_len)
    attn = _attention(q, k, v, sinks, batch, seq_len)
    x1, h, idx_t, wgt_t = _mixout(
        x2, attn, z, sa, sc, w_attn_o.astype(BF16), w_conv_o.astype(BF16), w_out.astype(BF16),
        g_ffn[None, :], w_router.T, b_router[:, None])

    rank_t, cnt = _rank(idx_t)
    counts = cnt[:, 0]
    padded = (counts + TM_EXPERT - 1) // TM_EXPERT * TM_EXPERT
    pad_end = jnp.cumsum(padded)
    pad_start = pad_end - padded
    n_tiles = (T * TOP_K) // TM_EXPERT + N_EXPERTS
    tile_row = jnp.arange(n_tiles, dtype=jnp.int32) * TM_EXPERT
    blk_e = jnp.minimum(jnp.sum(pad_end[None, :] <= tile_row[:, None], axis=1),
                        N_EXPERTS - 1).astype(jnp.int32)
    n_used = (pad_end[-1:] // TM_EXPERT).astype(jnp.int32)
    e_ids = jnp.arange(N_EXPERTS, dtype=jnp.int32)
    dest = rank_t + jnp.sum(
        jnp.where(idx_t[:, :, None] == e_ids[None, None, :], pad_start[None, None, :], 0), axis=-1)
    dest_flat = dest.reshape(-1).astype(jnp.int32)
    fill_start = (pad_start + counts).astype(jnp.int32)

    xs3 = _dispatch(dest_flat, fill_start, h.reshape(T, SLABS, LANES), n_tiles * TM_EXPERT)
    yb = _experts(blk_e, n_used, xs3.reshape(-1, LANES), w_gate, b_gate[:, None, :],
                  w_up, b_up[:, None, :], w_down, b_down[:, None, :], n_tiles)
    return _combine(dest_flat, yb.reshape(-1, SLABS, LANES), x1, wgt_t.T, g_out[None, :])


def kernel(x, g_mix, w_in, b_in, sinks, w_conv, w_attn_o, w_conv_o, w_out, g_ffn, w_router,
           b_router, w_gate, b_gate, w_up, b_up, w_down, b_down, g_final):
    batch, seq_len, d = x.shape
    depth = g_mix.shape[0]
    assert depth == 1, "the final norm is fused into the single layer's combine step"
    x2 = x.reshape(batch * seq_len, d)
    out = _layer(x2, batch, seq_len, g_mix[0], w_in[0], b_in[0], sinks[0], w_conv[0],
                 w_attn_o[0], w_conv_o[0], w_out[0], g_ffn[0], w_router[0], b_router[0],
                 w_gate[0], b_gate[0], w_up[0], b_up[0], w_down[0], b_down[0], g_final)
    return out.reshape(batch, seq_len, d)
```

```python
import functools

import jax
import jax.numpy as jnp
from jax import lax
from jax.experimental import pallas as pl
from jax.experimental.pallas import tpu as pltpu

D_MODEL = 1024
HEAD_DIM = 64
N_Q_HEADS = 16
N_KV_HEADS = 4
Q_PER_KV = N_Q_HEADS // N_KV_HEADS
ATTN_WIDTH = N_Q_HEADS * HEAD_DIM
KV_WIDTH = N_KV_HEADS * HEAD_DIM
ATTN_BLOCK = 128
WINDOW = 128
ROPE_THETA = 10000.0
CONV_WIDTH = D_MODEL
CONV_KERNEL = 3
N_EXPERTS = 32
TOP_K = 4
SWIGLU_LIMIT = 7.0
SWIGLU_ALPHA = 1.702
RMS_EPS = 1e-5

OFF_Q = 0
OFF_K = OFF_Q + ATTN_WIDTH
OFF_V = OFF_K + KV_WIDTH
OFF_CB = OFF_V + KV_WIDTH
OFF_CC = OFF_CB + CONV_WIDTH
OFF_CX = OFF_CC + CONV_WIDTH
OFF_GA = OFF_CX + CONV_WIDTH
OFF_GC = OFF_GA + D_MODEL
IN_WIDTH = OFF_GC + D_MODEL

LANES = 128
SUBLANES = 8
VMEM_LIMIT_BYTES = 56 * 1024 * 1024

TM_INPROJ = 512
COL_CHUNK = 512
TM_MIX = 512
TC_RANK = 512
TM_EXPERT = 256
TM_MOVE = 256

BF16 = jnp.bfloat16
F32 = jnp.float32
NEG_BIG = -1e30


def _rms_scale(x, g):
    ms = jnp.mean(x * x, axis=-1, keepdims=True)
    return (x * lax.rsqrt(ms + RMS_EPS)) * g


SLABS = D_MODEL // LANES


def _store_token_major(ref, val):
    tm = val.shape[0]
    for s in range(SLABS):
        ref[pl.ds(s, tm, stride=SLABS), :] = val[:, s * LANES:(s + 1) * LANES]


def _load_token_major(ref, tm):
    return jnp.concatenate(
        [ref[pl.ds(s, tm, stride=SLABS), :] for s in range(SLABS)], axis=1)


def _inproj_body(x_ref, g_ref, w_ref, b_ref, cos_ref, sin_ref, wc_ref,
                 q_ref, k_ref, v_ref, z_ref, sa_ref, sc_ref, carry_ref, *, tiles_per_seq):
    tm = x_ref.shape[0]
    i = pl.program_id(0)
    h = _rms_scale(x_ref[...], g_ref[...]).astype(BF16)

    def proj(c0, width):
        return (jnp.dot(h, w_ref[:, c0:c0 + width], preferred_element_type=F32)
                + b_ref[:, c0:c0 + width])

    cos = cos_ref[...]
    sin = sin_ref[...]
    lane = lax.broadcasted_iota(jnp.int32, (tm, LANES), 1)
    first_half = (lane & (HEAD_DIM // 2)) == 0

    def rope(t):
        partner = jnp.where(first_half,
                            pltpu.roll(t, LANES - HEAD_DIM // 2, 1),
                            pltpu.roll(t, HEAD_DIM // 2, 1))
        return t * cos + partner * sin

    for c in range(0, ATTN_WIDTH, COL_CHUNK):
        acc = proj(OFF_Q + c, COL_CHUNK)
        for j in range(0, COL_CHUNK, LANES):
            q_ref[:, c + j:c + j + LANES] = (
                rope(acc[:, j:j + LANES]) * (HEAD_DIM ** -0.5)).astype(BF16)

    acc = proj(OFF_K, 2 * KV_WIDTH)
    for j in range(0, KV_WIDTH, LANES):
        k_ref[:, j:j + LANES] = rope(acc[:, j:j + LANES]).astype(BF16)
    v_ref[...] = acc[:, KV_WIDTH:].astype(BF16)

    seq_start = (i % tiles_per_seq) == 0
    head = 2 * SUBLANES
    row = lax.broadcasted_iota(jnp.int32, (head, COL_CHUNK), 0)
    for c in range(0, CONV_WIDTH, COL_CHUNK):
        u = proj(OFF_CC + c, COL_CHUNK) * proj(OFF_CX + c, COL_CHUNK)
        cb = proj(OFF_CB + c, COL_CHUNK)
        w0 = wc_ref[0:1, c:c + COL_CHUNK]
        w1 = wc_ref[1:2, c:c + COL_CHUNK]
        w2 = wc_ref[2:3, c:c + COL_CHUNK]
        y = w0 * pltpu.roll(u, 2, 0) + w1 * pltpu.roll(u, 1, 0) + w2 * u
        z_ref[:, c:c + COL_CHUNK] = (cb * y).astype(BF16)
        prev = jnp.where(seq_start, 0.0, carry_ref[:, c:c + COL_CHUNK])
        pad = jnp.zeros((SUBLANES, COL_CHUNK), F32)
        uh = u[0:head]
        u1 = jnp.where(row < 1, jnp.concatenate([pltpu.roll(prev, 1, 0), pad], 0),
                       pltpu.roll(uh, 1, 0))
        u2 = jnp.where(row < 2, jnp.concatenate([pltpu.roll(prev, 2, 0), pad], 0),
                       pltpu.roll(uh, 2, 0))
        yh = w0 * u2 + w1 * u1 + w2 * uh
        z_ref[0:head, c:c + COL_CHUNK] = (cb[0:head] * yh).astype(BF16)
        carry_ref[:, c:c + COL_CHUNK] = u[tm - SUBLANES:tm]

    for c in range(0, D_MODEL, COL_CHUNK):
        sa_ref[:, c:c + COL_CHUNK] = jax.nn.sigmoid(proj(OFF_GA + c, COL_CHUNK)).astype(BF16)
        sc_ref[:, c:c + COL_CHUNK] = jax.nn.sigmoid(proj(OFF_GC + c, COL_CHUNK)).astype(BF16)


def _inproj(x2, g_mix, w_in_bf, b_in, cos_t, sin_t, w_conv, seq_len):
    T = x2.shape[0]
    tm = TM_INPROJ
    tiles_per_seq = seq_len // tm
    const = lambda i: (0, 0)
    row_blk = lambda i: (i, 0)
    pos_blk = lambda i: (i % tiles_per_seq, 0)
    return pl.pallas_call(
        functools.partial(_inproj_body, tiles_per_seq=tiles_per_seq),
        grid=(T // tm,),
        in_specs=[
            pl.BlockSpec((tm, D_MODEL), row_blk),
            pl.BlockSpec((1, D_MODEL), const),
            pl.BlockSpec((D_MODEL, IN_WIDTH), const, pipeline_mode=pl.Buffered(1)),
            pl.BlockSpec((1, IN_WIDTH), const),
            pl.BlockSpec((tm, LANES), pos_blk),
            pl.BlockSpec((tm, LANES), pos_blk),
            pl.BlockSpec((CONV_KERNEL, CONV_WIDTH), const),
        ],
        out_specs=[
            pl.BlockSpec((tm, ATTN_WIDTH), row_blk),
            pl.BlockSpec((tm, KV_WIDTH), row_blk),
            pl.BlockSpec((tm, KV_WIDTH), row_blk),
            pl.BlockSpec((tm, CONV_WIDTH), row_blk),
            pl.BlockSpec((tm, D_MODEL), row_blk),
            pl.BlockSpec((tm, D_MODEL), row_blk),
        ],
        out_shape=[
            jax.ShapeDtypeStruct((T, ATTN_WIDTH), BF16),
            jax.ShapeDtypeStruct((T, KV_WIDTH), BF16),
            jax.ShapeDtypeStruct((T, KV_WIDTH), BF16),
            jax.ShapeDtypeStruct((T, CONV_WIDTH), BF16),
            jax.ShapeDtypeStruct((T, D_MODEL), BF16),
            jax.ShapeDtypeStruct((T, D_MODEL), BF16),
        ],
        scratch_shapes=[pltpu.VMEM((SUBLANES, CONV_WIDTH), F32)],
        compiler_params=pltpu.CompilerParams(
            dimension_semantics=("arbitrary",), vmem_limit_bytes=VMEM_LIMIT_BYTES),
        name="inproj",
    )(x2, g_mix, w_in_bf, b_in, cos_t, sin_t, w_conv)


def _attn_body(sinks_ref, q_ref, kp_ref, kc_ref, vp_ref, vc_ref, o_ref):
    n = pl.program_id(1)
    blk = ATTN_BLOCK
    cols = Q_PER_KV * blk
    sj = lax.broadcasted_iota(jnp.int32, (2 * blk, cols), 0)
    qi = lax.broadcasted_iota(jnp.int32, (2 * blk, cols), 1) % blk
    dist = qi + blk - sj
    mask = (dist >= 0) & (dist < WINDOW) & ((sj >= blk) | (n > 0))
    k_all = jnp.concatenate([kp_ref[...], kc_ref[...]], axis=0)
    v_all = jnp.concatenate([vp_ref[...], vc_ref[...]], axis=0)
    v_t = v_all.astype(F32).T.astype(BF16)
    for kh in range(N_KV_HEADS):
        k_h = k_all[:, kh * HEAD_DIM:(kh + 1) * HEAD_DIM]
        vt_h = v_t[kh * HEAD_DIM:(kh + 1) * HEAD_DIM, :]
        heads = [kh * Q_PER_KV + g for g in range(Q_PER_KV)]
        q_g = jnp.concatenate(
            [q_ref[:, hq * HEAD_DIM:(hq + 1) * HEAD_DIM] for hq in heads], axis=0)
        sink = jnp.concatenate(
            [jnp.full((1, blk), sinks_ref[hq], F32) for hq in heads], axis=1)
        s = lax.dot_general(k_h, q_g, (((1,), (1,)), ((), ())), preferred_element_type=F32)
        s = jnp.where(mask, s, NEG_BIG)
        m = jnp.maximum(jnp.max(s, axis=0, keepdims=True), sink)
        p = jnp.exp(s - m)
        denom = jnp.sum(p, axis=0, keepdims=True) + jnp.exp(sink - m)
        o_t = jnp.dot(vt_h, p.astype(BF16), preferred_element_type=F32) / denom
        for g in range(0, Q_PER_KV, 2):
            pair = jnp.concatenate(
                [o_t[:, g * blk:(g + 1) * blk], o_t[:, (g + 1) * blk:(g + 2) * blk]], axis=0)
            c0 = heads[g] * HEAD_DIM
            o_ref[:, c0:c0 + 2 * HEAD_DIM] = pair.T.astype(BF16)


def _attention(q, k, v, sinks, batch, seq_len):
    T = q.shape[0]
    nb = seq_len // ATTN_BLOCK
    cur = lambda b, n: (b * nb + n, 0)
    prev = lambda b, n: (b * nb + jnp.maximum(n - 1, 0), 0)
    return pl.pallas_call(
        _attn_body,
        grid=(batch, nb),
        in_specs=[
            pl.BlockSpec(memory_space=pltpu.SMEM),
            pl.BlockSpec((ATTN_BLOCK, ATTN_WIDTH), cur),
            pl.BlockSpec((ATTN_BLOCK, KV_WIDTH), prev),
            pl.BlockSpec((ATTN_BLOCK, KV_WIDTH), cur),
            pl.BlockSpec((ATTN_BLOCK, KV_WIDTH), prev),
            pl.BlockSpec((ATTN_BLOCK, KV_WIDTH), cur),
        ],
        out_specs=pl.BlockSpec((ATTN_BLOCK, ATTN_WIDTH), cur),
        out_shape=jax.ShapeDtypeStruct((T, ATTN_WIDTH), BF16),
        compiler_params=pltpu.CompilerParams(
            dimension_semantics=("arbitrary", "arbitrary"), vmem_limit_bytes=VMEM_LIMIT_BYTES),
        name="attn",
    )(sinks, q, k, k, v, v)


def _mixout_body(x_ref, a_ref, z_ref, sa_ref, sc_ref, wa_ref, wc_ref, wo_ref, g_ref,
                 wr_ref, br_ref, x1_ref, h_ref, idx_ref, wgt_ref):
    tm = x_ref.shape[0]
    y_attn = jnp.dot(a_ref[...], wa_ref[...], preferred_element_type=F32)
    y_conv = jnp.dot(z_ref[...], wc_ref[...], preferred_element_type=F32)
    merged = sa_ref[...].astype(F32) * y_attn + sc_ref[...].astype(F32) * y_conv
    x1 = x_ref[...] + jnp.dot(merged.astype(BF16), wo_ref[...], preferred_element_type=F32)
    x1_ref[...] = x1
    h = _rms_scale(x1, g_ref[...])
    _store_token_major(h_ref, h)
    h_hi = h.astype(BF16)
    h_lo = (h - h_hi.astype(F32)).astype(BF16)
    p_hi = jnp.dot(h_hi, wr_ref[...], preferred_element_type=F32)
    p_lo = jnp.dot(h_lo, wr_ref[...], preferred_element_type=F32)
    lg = p_hi + pltpu.roll(p_hi, LANES - N_EXPERTS, 1) + p_lo
    logits = lg.T[0:N_EXPERTS, :] + br_ref[...]
    e_iota = lax.broadcasted_iota(jnp.int32, (N_EXPERTS, tm), 0)
    vals, idxs = [], []
    for _ in range(TOP_K):
        m = jnp.max(logits, axis=0, keepdims=True)
        idx = jnp.min(jnp.where(logits == m, e_iota, N_EXPERTS), axis=0, keepdims=True)
        vals.append(m)
        idxs.append(idx)
        logits = jnp.where(e_iota == idx, -jnp.inf, logits)
    ex = [jnp.exp(v - vals[0]) for v in vals]
    tot = ex[0] + ex[1] + ex[2] + ex[3]
    idx_ref[...] = jnp.concatenate(idxs, axis=0)
    wgt_ref[...] = jnp.concatenate([e / tot for e in ex], axis=0)


def _mixout(x2, attn, z, sa, sc, wa_bf, wc_bf, wo_bf, g_ffn, wr_t, b_router):
    T = x2.shape[0]
    tm = TM_MIX
    const = lambda i: (0, 0)
    row_blk = lambda i: (i, 0)
    col_blk = lambda i: (0, i)
    act = pl.BlockSpec((tm, D_MODEL), row_blk)
    wsq = pl.BlockSpec((D_MODEL, D_MODEL), const)
    return pl.pallas_call(
        _mixout_body,
        grid=(T // tm,),
        in_specs=[act, act, act, act, act, wsq, wsq, wsq,
                  pl.BlockSpec((1, D_MODEL), const),
                  pl.BlockSpec((D_MODEL, LANES), const),
                  pl.BlockSpec((N_EXPERTS, 1), const)],
        out_specs=[act,
                   pl.BlockSpec((tm * SLABS, LANES), row_blk),
                   pl.BlockSpec((TOP_K, tm), col_blk),
                   pl.BlockSpec((TOP_K, tm), col_blk)],
        out_shape=[jax.ShapeDtypeStruct((T, D_MODEL), F32),
                   jax.ShapeDtypeStruct((T * SLABS, LANES), F32),
                   jax.ShapeDtypeStruct((TOP_K, T), jnp.int32),
                   jax.ShapeDtypeStruct((TOP_K, T), F32)],
        compiler_params=pltpu.CompilerParams(
            dimension_semantics=("arbitrary",), vmem_limit_bytes=VMEM_LIMIT_BYTES),
        name="mixout",
    )(x2, attn, z, sa, sc, wa_bf, wc_bf, wo_bf, g_ffn, wr_t, b_router)


def _rank_body(idx_ref, rank_ref, cnt_ref, carry_ref):
    tc = idx_ref.shape[1]
    i = pl.program_id(0)

    @pl.when(i == 0)
    def _():
        carry_ref[...] = jnp.zeros_like(carry_ref)

    idx = idx_ref[...]
    e_iota = lax.broadcasted_iota(jnp.int32, (N_EXPERTS, tc), 0)
    sel = [e_iota == idx[k:k + 1, :] for k in range(TOP_K)]
    member = (sel[0] | sel[1] | sel[2] | sel[3])
    onehot = jnp.where(member, 1.0, 0.0).astype(BF16)
    r = lax.broadcasted_iota(jnp.int32, (tc, tc), 0)
    c = lax.broadcasted_iota(jnp.int32, (tc, tc), 1)
    before = jnp.where(r < c, 1.0, 0.0).astype(BF16)
    carry = carry_ref[:, 0:1]
    prefix = jnp.dot(onehot, before, preferred_element_type=F32) + carry
    ranks = [jnp.sum(jnp.where(sel[k], prefix, 0.0), axis=0, keepdims=True)
             for k in range(TOP_K)]
    rank_ref[...] = jnp.concatenate(ranks, axis=0).astype(jnp.int32)
    total = carry + jnp.sum(onehot.astype(F32), axis=1, keepdims=True)
    carry_ref[...] = jnp.broadcast_to(total, carry_ref.shape)
    cnt_ref[...] = jnp.broadcast_to(total, cnt_ref.shape).astype(jnp.int32)


def _rank(idx_t):
    T = idx_t.shape[1]
    tc = TC_RANK
    return pl.pallas_call(
        _rank_body,
        grid=(T // tc,),
        in_specs=[pl.BlockSpec((TOP_K, tc), lambda i: (0, i))],
        out_specs=[pl.BlockSpec((TOP_K, tc), lambda i: (0, i)),
                   pl.BlockSpec((N_EXPERTS, LANES), lambda i: (0, 0))],
        out_shape=[jax.ShapeDtypeStruct((TOP_K, T), jnp.int32),
                   jax.ShapeDtypeStruct((N_EXPERTS, LANES), jnp.int32)],
        scratch_shapes=[pltpu.VMEM((N_EXPERTS, LANES), F32)],
        compiler_params=pltpu.CompilerParams(dimension_semantics=("arbitrary",)),
        name="rank",
    )(idx_t)


def _dispatch_body(dest_ref, fill_ref, h_ref, xs_ref, zero_ref, sem, *, n_tok):
    tm = h_ref.shape[0]
    i = pl.program_id(0)

    @pl.when(i == 0)
    def _():
        zero_ref[...] = jnp.zeros_like(zero_ref)
        fills = [pltpu.make_async_copy(
            zero_ref, xs_ref.at[pl.ds(fill_ref[e], TM_EXPERT)], sem) for e in range(N_EXPERTS)]
        for f in fills:
            f.start()
        for f in fills:
            f.wait()

    def row_copy(j, k):
        d = dest_ref[k * n_tok + i * tm + j]
        return pltpu.make_async_copy(h_ref.at[j], xs_ref.at[d], sem)

    def issue(j, carry):
        for k in range(TOP_K):
            row_copy(j, k).start()
        return carry

    lax.fori_loop(0, tm, issue, 0, unroll=8)

    def drain(j, carry):
        for k in range(TOP_K):
            row_copy(j, k).wait()
        return carry

    lax.fori_loop(0, tm, drain, 0, unroll=8)


def _dispatch(dest_flat, fill_start, h3, n_rows):
    T = h3.shape[0]
    tm = TM_MOVE
    return pl.pallas_call(
        functools.partial(_dispatch_body, n_tok=T),
        grid_spec=pltpu.PrefetchScalarGridSpec(
            num_scalar_prefetch=2,
            grid=(T // tm,),
            in_specs=[pl.BlockSpec((tm, SLABS, LANES), lambda i, d, f: (i, 0, 0))],
            out_specs=pl.BlockSpec(memory_space=pl.ANY),
            scratch_shapes=[pltpu.VMEM((TM_EXPERT, SLABS, LANES), F32),
                            pltpu.SemaphoreType.DMA(())],
        ),
        out_shape=jax.ShapeDtypeStruct((n_rows + TM_EXPERT, SLABS, LANES), F32),
        compiler_params=pltpu.CompilerParams(dimension_semantics=("arbitrary",)),
        name="dispatch",
    )(dest_flat, fill_start, h3)


def _expert_body(blk_e_ref, n_used_ref, xs_ref, wg_ref, bg_ref, wu_ref, bu_ref, wd_ref, bd_ref,
                 y_ref, wg_bf, wu_bf, wd_bf):
    i = pl.program_id(0)
    e = blk_e_ref[i]
    e_prev = blk_e_ref[jnp.maximum(i - 1, 0)]

    @pl.when(i < n_used_ref[0])
    def _():
        @pl.when((i == 0) | (e != e_prev))
        def _():
            wg_bf[...] = wg_ref[0].astype(BF16)
            wu_bf[...] = wu_ref[0].astype(BF16)
            wd_bf[...] = wd_ref[0].astype(BF16)

        x = _load_token_major(xs_ref, TM_EXPERT).astype(BF16)
        g = jnp.dot(x, wg_bf[...], preferred_element_type=F32) + bg_ref[0]
        u = jnp.dot(x, wu_bf[...], preferred_element_type=F32) + bu_ref[0]
        g = jnp.minimum(g, SWIGLU_LIMIT)
        u = jnp.clip(u, -SWIGLU_LIMIT, SWIGLU_LIMIT)
        a = g * jax.nn.sigmoid(SWIGLU_ALPHA * g) * (u + 1.0)
        y = jnp.dot(a.astype(BF16), wd_bf[...], preferred_element_type=F32) + bd_ref[0]
        _store_token_major(y_ref, y)


def _experts(blk_e, n_used, xs, w_gate, b_gate, w_up, b_up, w_down, b_down, n_tiles):
    tm = TM_EXPERT
    d_ff = w_gate.shape[2]
    tile = lambda i, be, nu: (jnp.minimum(i, nu[0] - 1), 0)
    wsel = lambda i, be, nu: (be[i], 0, 0)
    return pl.pallas_call(
        _expert_body,
        grid_spec=pltpu.PrefetchScalarGridSpec(
            num_scalar_prefetch=2,
            grid=(n_tiles,),
            in_specs=[
                pl.BlockSpec((tm * SLABS, LANES), tile),
                pl.BlockSpec((1, D_MODEL, d_ff), wsel),
                pl.BlockSpec((1, 1, d_ff), wsel),
                pl.BlockSpec((1, D_MODEL, d_ff), wsel),
                pl.BlockSpec((1, 1, d_ff), wsel),
                pl.BlockSpec((1, d_ff, D_MODEL), wsel),
                pl.BlockSpec((1, 1, D_MODEL), wsel),
            ],
            out_specs=pl.BlockSpec((tm * SLABS, LANES), tile),
            scratch_shapes=[pltpu.VMEM((D_MODEL, d_ff), BF16),
                            pltpu.VMEM((D_MODEL, d_ff), BF16),
                            pltpu.VMEM((d_ff, D_MODEL), BF16)],
        ),
        out_shape=jax.ShapeDtypeStruct((n_tiles * tm * SLABS, LANES), F32),
        compiler_params=pltpu.CompilerParams(
            dimension_semantics=("arbitrary",), vmem_limit_bytes=VMEM_LIMIT_BYTES),
        name="experts",
    )(blk_e, n_used, xs, w_gate, b_gate, w_up, b_up, w_down, b_down)


def _combine_body(dest_ref, y_ref, x1_ref, wgt_ref, g_ref, o_ref, buf_ref, sem, *, n_tok):
    tm = x1_ref.shape[0]
    i = pl.program_id(0)

    def row_copy(j, k):
        d = dest_ref[k * n_tok + i * tm + j]
        slab = pl.ds(pl.multiple_of(j * SLABS, SLABS), SLABS)
        return pltpu.make_async_copy(y_ref.at[d], buf_ref.at[k, slab], sem)

    def issue(j, carry):
        for k in range(TOP_K):
            row_copy(j, k).start()
        return carry

    lax.fori_loop(0, tm, issue, 0, unroll=8)

    def drain(j, carry):
        for k in range(TOP_K):
            row_copy(j, k).wait()
        return carry

    lax.fori_loop(0, tm, drain, 0, unroll=8)

    wgt = wgt_ref[...]
    acc = x1_ref[...]
    for k in range(TOP_K):
        rows = _load_token_major(buf_ref.at[k], tm)
        acc = acc + wgt[:, k:k + 1] * rows
    o_ref[...] = _rms_scale(acc, g_ref[...])


def _combine(dest_flat, yb, x1, wgt_rows, g_final):
    T = x1.shape[0]
    tm = TM_MOVE
    return pl.pallas_call(
        functools.partial(_combine_body, n_tok=T),
        grid_spec=pltpu.PrefetchScalarGridSpec(
            num_scalar_prefetch=1,
            grid=(T // tm,),
            in_specs=[
                pl.BlockSpec(memory_space=pl.ANY),
                pl.BlockSpec((tm, D_MODEL), lambda i, d: (i, 0)),
                pl.BlockSpec((tm, TOP_K), lambda i, d: (i, 0)),
                pl.BlockSpec((1, D_MODEL), lambda i, d: (0, 0)),
            ],
            out_specs=pl.BlockSpec((tm, D_MODEL), lambda i, d: (i, 0)),
            scratch_shapes=[pltpu.VMEM((TOP_K, tm * SLABS, LANES), F32),
                            pltpu.SemaphoreType.DMA(())],
        ),
        out_shape=jax.ShapeDtypeStruct((T, D_MODEL), F32),
        compiler_params=pltpu.CompilerParams(dimension_semantics=("arbitrary",)),
        name="combine",
    )(dest_flat, yb, x1, wgt_rows, g_final)


def _rope_tables(seq_len):
    half = HEAD_DIM // 2
    inv_freq = ROPE_THETA ** (-jnp.arange(half, dtype=F32) / half)
    ang = jnp.arange(seq_len, dtype=jnp.int32).astype(F32)[:, None] * inv_freq[None, :]
    cos = jnp.cos(ang)
    sin = jnp.sin(ang)
    reps = LANES // HEAD_DIM
    cos_t = jnp.tile(jnp.concatenate([cos, cos], axis=-1), (1, reps))
    sin_t = jnp.tile(jnp.concatenate([-sin, sin], axis=-1), (1, reps))
    return cos_t, sin_t


def _router_hi_lo(w_router):
    hi = w_router.astype(BF16)
    lo = (w_router - hi.astype(F32)).astype(BF16)
    pad = jnp.zeros((w_router.shape[0], LANES - 2 * N_EXPERTS), BF16)
    return jnp.concatenate([hi, lo, pad], axis=1)


def _layer(x2, batch, seq_len, g_mix, w_in, b_in, sinks, w_conv, w_attn_o, w_conv_o, w_out,
           g_ffn, w_router, b_router, w_gate, b_gate, w_up, b_up, w_down, b_down, g_out):
    T = x2.shape[0]
    cos_t, sin_t = _rope_tables(seq_len)
    q, k, v, z, sa, sc = _inproj(x2, g_mix[None, :], w_in.astype(BF16), b_in[None, :],
                                 cos_t, sin_t, w_conv, seq_len)
    attn = _attention(q, k, v, sinks, batch, seq_len)
    x1, h, idx_t, wgt_t = _mixout(
        x2, attn, z, sa, sc, w_attn_o.astype(BF16), w_conv_o.astype(BF16), w_out.astype(BF16),
        g_ffn[None, :], _router_hi_lo(w_router), b_router[:, None])

    rank_t, cnt = _rank(idx_t)
    counts = cnt[:, 0]
    padded = (counts + TM_EXPERT - 1) // TM_EXPERT * TM_EXPERT
    pad_end = jnp.cumsum(padded)
    pad_start = pad_end - padded
    n_tiles = (T * TOP_K) // TM_EXPERT + N_EXPERTS
    tile_row = jnp.arange(n_tiles, dtype=jnp.int32) * TM_EXPERT
    blk_e = jnp.minimum(jnp.sum(pad_end[None, :] <= tile_row[:, None], axis=1),
                        N_EXPERTS - 1).astype(jnp.int32)
    n_used = (pad_end[-1:] // TM_EXPERT).astype(jnp.int32)
    e_ids = jnp.arange(N_EXPERTS, dtype=jnp.int32)
    dest = rank_t + jnp.sum(
        jnp.where(idx_t[:, :, None] == e_ids[None, None, :], pad_start[None, None, :], 0), axis=-1)
    dest_flat = dest.reshape(-1).astype(jnp.int32)
    fill_start = (pad_start + counts).astype(jnp.int32)

    xs3 = _dispatch(dest_flat, fill_start, h.reshape(T, SLABS, LANES), n_tiles * TM_EXPERT)
    yb = _experts(blk_e, n_used, xs3.reshape(-1, LANES), w_gate, b_gate[:, None, :],
                  w_up, b_up[:, None, :], w_down, b_down[:, None, :], n_tiles)
    return _combine(dest_flat, yb.reshape(-1, SLABS, LANES), x1, wgt_t.T, g_out[None, :])


def kernel(x, g_mix, w_in, b_in, sinks, w_conv, w_attn_o, w_conv_o, w_out, g_ffn, w_router,
           b_router, w_gate, b_gate, w_up, b_up, w_down, b_down, g_final):
    batch, seq_len, d = x.shape
    depth = g_mix.shape[0]
    assert depth == 1, "the final norm is fused into the single layer's combine step"
    x2 = x.reshape(batch * seq_len, d)
    out = _layer(x2, batch, seq_len, g_mix[0], w_in[0], b_in[0], sinks[0], w_conv[0],
                 w_attn_o[0], w_conv_o[0], w_out[0], g_ffn[0], w_router[0], b_router[0],
                 w_gate[0], b_gate[0], w_up[0], b_up[0], w_down[0], b_down[0], g_final)
    return out.reshape(batch, seq_len, d)
```

```python
import functools

import jax
import jax.numpy as jnp
from jax import lax
from jax.experimental import pallas as pl
from jax.experimental.pallas import tpu as pltpu

D_MODEL = 1024
HEAD_DIM = 64
N_Q_HEADS = 16
N_KV_HEADS = 4
Q_PER_KV = N_Q_HEADS // N_KV_HEADS
ATTN_WIDTH = N_Q_HEADS * HEAD_DIM
KV_WIDTH = N_KV_HEADS * HEAD_DIM
ATTN_BLOCK = 128
WINDOW = 128
ROPE_THETA = 10000.0
CONV_WIDTH = D_MODEL
CONV_KERNEL = 3
N_EXPERTS = 32
TOP_K = 4
SWIGLU_LIMIT = 7.0
SWIGLU_ALPHA = 1.702
RMS_EPS = 1e-5

OFF_Q = 0
OFF_K = OFF_Q + ATTN_WIDTH
OFF_V = OFF_K + KV_WIDTH
OFF_CB = OFF_V + KV_WIDTH
OFF_CC = OFF_CB + CONV_WIDTH
OFF_CX = OFF_CC + CONV_WIDTH
OFF_GA = OFF_CX + CONV_WIDTH
OFF_GC = OFF_GA + D_MODEL
IN_WIDTH = OFF_GC + D_MODEL

LANES = 128
SUBLANES = 8
VMEM_LIMIT_BYTES = 56 * 1024 * 1024

TM_INPROJ = 512
COL_CHUNK = 512
TM_MIX = 512
TC_RANK = 512
TM_EXPERT = 256
TM_MOVE = 512

BF16 = jnp.bfloat16
F32 = jnp.float32
NEG_BIG = -1e30


def _rms_scale(x, g):
    ms = jnp.mean(x * x, axis=-1, keepdims=True)
    return (x * lax.rsqrt(ms + RMS_EPS)) * g


SLABS = D_MODEL // LANES


def _store_token_major(ref, val):
    tm = val.shape[0]
    for s in range(SLABS):
        ref[pl.ds(s, tm, stride=SLABS), :] = val[:, s * LANES:(s + 1) * LANES]


def _load_token_major(ref, tm):
    return jnp.concatenate(
        [ref[pl.ds(s, tm, stride=SLABS), :] for s in range(SLABS)], axis=1)


def _inproj_body(x_ref, g_ref, w_ref, b_ref, cos_ref, sin_ref, wc_ref,
                 q_ref, k_ref, v_ref, z_ref, sa_ref, sc_ref, carry_ref, *, tiles_per_seq):
    tm = x_ref.shape[0]
    i = pl.program_id(0)
    h = _rms_scale(x_ref[...], g_ref[...]).astype(BF16)

    def proj(c0, width):
        return (jnp.dot(h, w_ref[:, c0:c0 + width], preferred_element_type=F32)
                + b_ref[:, c0:c0 + width])

    cos = cos_ref[...]
    sin = sin_ref[...]
    lane = lax.broadcasted_iota(jnp.int32, (tm, LANES), 1)
    first_half = (lane & (HEAD_DIM // 2)) == 0

    def rope(t):
        partner = jnp.where(first_half,
                            pltpu.roll(t, LANES - HEAD_DIM // 2, 1),
                            pltpu.roll(t, HEAD_DIM // 2, 1))
        return t * cos + partner * sin

    for c in range(0, ATTN_WIDTH, COL_CHUNK):
        acc = proj(OFF_Q + c, COL_CHUNK)
        for j in range(0, COL_CHUNK, LANES):
            q_ref[:, c + j:c + j + LANES] = (
                rope(acc[:, j:j + LANES]) * (HEAD_DIM ** -0.5)).astype(BF16)

    acc = proj(OFF_K, 2 * KV_WIDTH)
    for j in range(0, KV_WIDTH, LANES):
        k_ref[:, j:j + LANES] = rope(acc[:, j:j + LANES]).astype(BF16)
    v_ref[...] = acc[:, KV_WIDTH:].astype(BF16)

    seq_start = (i % tiles_per_seq) == 0
    head = 2 * SUBLANES
    row = lax.broadcasted_iota(jnp.int32, (head, COL_CHUNK), 0)
    for c in range(0, CONV_WIDTH, COL_CHUNK):
        u = proj(OFF_CC + c, COL_CHUNK) * proj(OFF_CX + c, COL_CHUNK)
        cb = proj(OFF_CB + c, COL_CHUNK)
        w0 = wc_ref[0:1, c:c + COL_CHUNK]
        w1 = wc_ref[1:2, c:c + COL_CHUNK]
        w2 = wc_ref[2:3, c:c + COL_CHUNK]
        y = w0 * pltpu.roll(u, 2, 0) + w1 * pltpu.roll(u, 1, 0) + w2 * u
        z_ref[:, c:c + COL_CHUNK] = (cb * y).astype(BF16)
        prev = jnp.where(seq_start, 0.0, carry_ref[:, c:c + COL_CHUNK])
        pad = jnp.zeros((SUBLANES, COL_CHUNK), F32)
        uh = u[0:head]
        u1 = jnp.where(row < 1, jnp.concatenate([pltpu.roll(prev, 1, 0), pad], 0),
                       pltpu.roll(uh, 1, 0))
        u2 = jnp.where(row < 2, jnp.concatenate([pltpu.roll(prev, 2, 0), pad], 0),
                       pltpu.roll(uh, 2, 0))
        yh = w0 * u2 + w1 * u1 + w2 * uh
        z_ref[0:head, c:c + COL_CHUNK] = (cb[0:head] * yh).astype(BF16)
        carry_ref[:, c:c + COL_CHUNK] = u[tm - SUBLANES:tm]

    for c in range(0, D_MODEL, COL_CHUNK):
        sa_ref[:, c:c + COL_CHUNK] = jax.nn.sigmoid(proj(OFF_GA + c, COL_CHUNK)).astype(BF16)
        sc_ref[:, c:c + COL_CHUNK] = jax.nn.sigmoid(proj(OFF_GC + c, COL_CHUNK)).astype(BF16)


def _inproj(x2, g_mix, w_in_bf, b_in, cos_t, sin_t, w_conv, seq_len):
    T = x2.shape[0]
    tm = TM_INPROJ
    tiles_per_seq = seq_len // tm
    const = lambda i: (0, 0)
    row_blk = lambda i: (i, 0)
    pos_blk = lambda i: (i % tiles_per_seq, 0)
    return pl.pallas_call(
        functools.partial(_inproj_body, tiles_per_seq=tiles_per_seq),
        grid=(T // tm,),
        in_specs=[
            pl.BlockSpec((tm, D_MODEL), row_blk),
            pl.BlockSpec((1, D_MODEL), const),
            pl.BlockSpec((D_MODEL, IN_WIDTH), const, pipeline_mode=pl.Buffered(1)),
            pl.BlockSpec((1, IN_WIDTH), const),
            pl.BlockSpec((tm, LANES), pos_blk),
            pl.BlockSpec((tm, LANES), pos_blk),
            pl.BlockSpec((CONV_KERNEL, CONV_WIDTH), const),
        ],
        out_specs=[
            pl.BlockSpec((tm, ATTN_WIDTH), row_blk),
            pl.BlockSpec((tm, KV_WIDTH), row_blk),
            pl.BlockSpec((tm, KV_WIDTH), row_blk),
            pl.BlockSpec((tm, CONV_WIDTH), row_blk),
            pl.BlockSpec((tm, D_MODEL), row_blk),
            pl.BlockSpec((tm, D_MODEL), row_blk),
        ],
        out_shape=[
            jax.ShapeDtypeStruct((T, ATTN_WIDTH), BF16),
            jax.ShapeDtypeStruct((T, KV_WIDTH), BF16),
            jax.ShapeDtypeStruct((T, KV_WIDTH), BF16),
            jax.ShapeDtypeStruct((T, CONV_WIDTH), BF16),
            jax.ShapeDtypeStruct((T, D_MODEL), BF16),
            jax.ShapeDtypeStruct((T, D_MODEL), BF16),
        ],
        scratch_shapes=[pltpu.VMEM((SUBLANES, CONV_WIDTH), F32)],
        compiler_params=pltpu.CompilerParams(
            dimension_semantics=("arbitrary",), vmem_limit_bytes=VMEM_LIMIT_BYTES),
        name="inproj",
    )(x2, g_mix, w_in_bf, b_in, cos_t, sin_t, w_conv)


def _attn_body(sinks_ref, q_ref, kp_ref, kc_ref, vp_ref, vc_ref, o_ref):
    n = pl.program_id(1)
    blk = ATTN_BLOCK
    cols = Q_PER_KV * blk
    sj = lax.broadcasted_iota(jnp.int32, (2 * blk, cols), 0)
    qi = lax.broadcasted_iota(jnp.int32, (2 * blk, cols), 1) % blk
    dist = qi + blk - sj
    mask = (dist >= 0) & (dist < WINDOW) & ((sj >= blk) | (n > 0))
    k_all = jnp.concatenate([kp_ref[...], kc_ref[...]], axis=0)
    v_all = jnp.concatenate([vp_ref[...], vc_ref[...]], axis=0)
    v_t = v_all.astype(F32).T.astype(BF16)
    for kh in range(N_KV_HEADS):
        k_h = k_all[:, kh * HEAD_DIM:(kh + 1) * HEAD_DIM]
        vt_h = v_t[kh * HEAD_DIM:(kh + 1) * HEAD_DIM, :]
        heads = [kh * Q_PER_KV + g for g in range(Q_PER_KV)]
        q_g = jnp.concatenate(
            [q_ref[:, hq * HEAD_DIM:(hq + 1) * HEAD_DIM] for hq in heads], axis=0)
        sink = jnp.concatenate(
            [jnp.full((1, blk), sinks_ref[hq], F32) for hq in heads], axis=1)
        s = lax.dot_general(k_h, q_g, (((1,), (1,)), ((), ())), preferred_element_type=F32)
        s = jnp.where(mask, s, NEG_BIG)
        m = jnp.maximum(jnp.max(s, axis=0, keepdims=True), sink)
        p = jnp.exp(s - m)
        denom = jnp.sum(p, axis=0, keepdims=True) + jnp.exp(sink - m)
        o_t = jnp.dot(vt_h, p.astype(BF16), preferred_element_type=F32) / denom
        for g in range(0, Q_PER_KV, 2):
            pair = jnp.concatenate(
                [o_t[:, g * blk:(g + 1) * blk], o_t[:, (g + 1) * blk:(g + 2) * blk]], axis=0)
            c0 = heads[g] * HEAD_DIM
            o_ref[:, c0:c0 + 2 * HEAD_DIM] = pair.T.astype(BF16)


def _attention(q, k, v, sinks, batch, seq_len):
    T = q.shape[0]
    nb = seq_len // ATTN_BLOCK
    cur = lambda b, n: (b * nb + n, 0)
    prev = lambda b, n: (b * nb + jnp.maximum(n - 1, 0), 0)
    return pl.pallas_call(
        _attn_body,
        grid=(batch, nb),
        in_specs=[
            pl.BlockSpec(memory_space=pltpu.SMEM),
            pl.BlockSpec((ATTN_BLOCK, ATTN_WIDTH), cur),
            pl.BlockSpec((ATTN_BLOCK, KV_WIDTH), prev),
            pl.BlockSpec((ATTN_BLOCK, KV_WIDTH), cur),
            pl.BlockSpec((ATTN_BLOCK, KV_WIDTH), prev),
            pl.BlockSpec((ATTN_BLOCK, KV_WIDTH), cur),
        ],
        out_specs=pl.BlockSpec((ATTN_BLOCK, ATTN_WIDTH), cur),
        out_shape=jax.ShapeDtypeStruct((T, ATTN_WIDTH), BF16),
        compiler_params=pltpu.CompilerParams(
            dimension_semantics=("arbitrary", "arbitrary"), vmem_limit_bytes=VMEM_LIMIT_BYTES),
        name="attn",
    )(sinks, q, k, k, v, v)


def _mixout_body(x_ref, a_ref, z_ref, sa_ref, sc_ref, wa_ref, wc_ref, wo_ref, g_ref,
                 wr_ref, br_ref, x1_ref, h_ref, idx_ref, wgt_ref):
    tm = x_ref.shape[0]
    y_attn = jnp.dot(a_ref[...], wa_ref[...], preferred_element_type=F32)
    y_conv = jnp.dot(z_ref[...], wc_ref[...], preferred_element_type=F32)
    merged = sa_ref[...].astype(F32) * y_attn + sc_ref[...].astype(F32) * y_conv
    x1 = x_ref[...] + jnp.dot(merged.astype(BF16), wo_ref[...], preferred_element_type=F32)
    x1_ref[...] = x1
    h = _rms_scale(x1, g_ref[...])
    _store_token_major(h_ref, h)
    h_hi = h.astype(BF16)
    h_lo = (h - h_hi.astype(F32)).astype(BF16)
    p_hi = jnp.dot(h_hi, wr_ref[...], preferred_element_type=F32)
    p_lo = jnp.dot(h_lo, wr_ref[...], preferred_element_type=F32)
    lg = p_hi + pltpu.roll(p_hi, LANES - N_EXPERTS, 1) + p_lo
    logits = lg.T[0:N_EXPERTS, :] + br_ref[...]
    e_iota = lax.broadcasted_iota(jnp.int32, (N_EXPERTS, tm), 0)
    vals, idxs = [], []
    for _ in range(TOP_K):
        m = jnp.max(logits, axis=0, keepdims=True)
        idx = jnp.min(jnp.where(logits == m, e_iota, N_EXPERTS), axis=0, keepdims=True)
        vals.append(m)
        idxs.append(idx)
        logits = jnp.where(e_iota == idx, -jnp.inf, logits)
    ex = [jnp.exp(v - vals[0]) for v in vals]
    tot = ex[0] + ex[1] + ex[2] + ex[3]
    idx_ref[...] = jnp.concatenate(idxs, axis=0)
    wgt_ref[...] = jnp.concatenate([e / tot for e in ex], axis=0)


def _mixout(x2, attn, z, sa, sc, wa_bf, wc_bf, wo_bf, g_ffn, wr_t, b_router):
    T = x2.shape[0]
    tm = TM_MIX
    const = lambda i: (0, 0)
    row_blk = lambda i: (i, 0)
    col_blk = lambda i: (0, i)
    act = pl.BlockSpec((tm, D_MODEL), row_blk)
    wsq = pl.BlockSpec((D_MODEL, D_MODEL), const)
    return pl.pallas_call(
        _mixout_body,
        grid=(T // tm,),
        in_specs=[act, act, act, act, act, wsq, wsq, wsq,
                  pl.BlockSpec((1, D_MODEL), const),
                  pl.BlockSpec((D_MODEL, LANES), const),
                  pl.BlockSpec((N_EXPERTS, 1), const)],
        out_specs=[act,
                   pl.BlockSpec((tm * SLABS, LANES), row_blk),
                   pl.BlockSpec((TOP_K, tm), col_blk),
                   pl.BlockSpec((TOP_K, tm), col_blk)],
        out_shape=[jax.ShapeDtypeStruct((T, D_MODEL), F32),
                   jax.ShapeDtypeStruct((T * SLABS, LANES), F32),
                   jax.ShapeDtypeStruct((TOP_K, T), jnp.int32),
                   jax.ShapeDtypeStruct((TOP_K, T), F32)],
        compiler_params=pltpu.CompilerParams(
            dimension_semantics=("arbitrary",), vmem_limit_bytes=VMEM_LIMIT_BYTES),
        name="mixout",
    )(x2, attn, z, sa, sc, wa_bf, wc_bf, wo_bf, g_ffn, wr_t, b_router)


def _rank_body(idx_ref, rank_ref, cnt_ref, carry_ref):
    tc = idx_ref.shape[1]
    i = pl.program_id(0)

    @pl.when(i == 0)
    def _():
        carry_ref[...] = jnp.zeros_like(carry_ref)

    idx = idx_ref[...]
    e_iota = lax.broadcasted_iota(jnp.int32, (N_EXPERTS, tc), 0)
    sel = [e_iota == idx[k:k + 1, :] for k in range(TOP_K)]
    member = (sel[0] | sel[1] | sel[2] | sel[3])
    onehot = jnp.where(member, 1.0, 0.0).astype(BF16)
    r = lax.broadcasted_iota(jnp.int32, (tc, tc), 0)
    c = lax.broadcasted_iota(jnp.int32, (tc, tc), 1)
    before = jnp.where(r < c, 1.0, 0.0).astype(BF16)
    carry = carry_ref[:, 0:1]
    prefix = jnp.dot(onehot, before, preferred_element_type=F32) + carry
    ranks = [jnp.sum(jnp.where(sel[k], prefix, 0.0), axis=0, keepdims=True)
             for k in range(TOP_K)]
    rank_ref[...] = jnp.concatenate(ranks, axis=0).astype(jnp.int32)
    total = carry + jnp.sum(onehot.astype(F32), axis=1, keepdims=True)
    carry_ref[...] = jnp.broadcast_to(total, carry_ref.shape)
    cnt_ref[...] = jnp.broadcast_to(total, cnt_ref.shape).astype(jnp.int32)


def _rank(idx_t):
    T = idx_t.shape[1]
    tc = TC_RANK
    return pl.pallas_call(
        _rank_body,
        grid=(T // tc,),
        in_specs=[pl.BlockSpec((TOP_K, tc), lambda i: (0, i))],
        out_specs=[pl.BlockSpec((TOP_K, tc), lambda i: (0, i)),
                   pl.BlockSpec((N_EXPERTS, LANES), lambda i: (0, 0))],
        out_shape=[jax.ShapeDtypeStruct((TOP_K, T), jnp.int32),
                   jax.ShapeDtypeStruct((N_EXPERTS, LANES), jnp.int32)],
        scratch_shapes=[pltpu.VMEM((N_EXPERTS, LANES), F32)],
        compiler_params=pltpu.CompilerParams(dimension_semantics=("arbitrary",)),
        name="rank",
    )(idx_t)


def _invperm_body(dest_ref, fill_ref, end_ref, src_ref, *, n_assign, n_slots):
    def mark_empty(p, carry):
        src_ref[p] = n_assign + (p & (2 * TM_EXPERT - 1))
        return carry

    for e in range(N_EXPERTS):
        lax.fori_loop(fill_ref[e], end_ref[e], mark_empty, 0)
    lax.fori_loop(end_ref[N_EXPERTS - 1], n_slots, mark_empty, 0)

    def place(a, carry):
        src_ref[dest_ref[a]] = a
        return carry

    lax.fori_loop(0, n_assign, place, 0, unroll=16)


def _invperm(dest_flat, fill_start, pad_end, n_slots):
    return pl.pallas_call(
        functools.partial(_invperm_body, n_assign=dest_flat.shape[0], n_slots=n_slots),
        grid_spec=pltpu.PrefetchScalarGridSpec(
            num_scalar_prefetch=3,
            grid=(1,),
            in_specs=[],
            out_specs=pl.BlockSpec(memory_space=pltpu.SMEM),
        ),
        out_shape=jax.ShapeDtypeStruct((n_slots,), jnp.int32),
        compiler_params=pltpu.CompilerParams(dimension_semantics=("arbitrary",)),
        name="invperm",
    )(dest_flat, fill_start, pad_end)


def _expert_body(blk_e_ref, n_used_ref, src_ref, h_ref, wg_ref, bg_ref, wu_ref, bu_ref, wd_ref,
                 bd_ref, y_ref, xbuf, ybuf, wg_bf, wu_bf, wd_bf, gsem, ssem, *, n_tok):
    tm = TM_EXPERT
    i = pl.program_id(0)
    n_used = n_used_ref[0]
    slot = i % 2
    e = blk_e_ref[i]
    e_prev = blk_e_ref[jnp.maximum(i - 1, 0)]

    def gather_copy(tile, buf, r):
        a = src_ref[tile * tm + r]
        return pltpu.make_async_copy(h_ref.at[a & (n_tok - 1)],
                                     xbuf.at[buf, pl.ds(r * SLABS, SLABS)], gsem.at[buf])

    def scatter_copy(tile, buf, r):
        a = src_ref[tile * tm + r]
        return pltpu.make_async_copy(ybuf.at[buf, pl.ds(r * SLABS, SLABS)], y_ref.at[a],
                                     ssem.at[buf])

    @pl.when(i == 0)
    def _():
        for r in range(tm):
            gather_copy(0, 0, r).start()

    @pl.when(i < n_used)
    def _():
        @pl.when(i >= 2)
        def _():
            for r in range(tm):
                scatter_copy(i - 2, slot, r).wait()

        @pl.when((i == 0) | (e != e_prev))
        def _():
            wg_bf[...] = wg_ref[0].astype(BF16)
            wu_bf[...] = wu_ref[0].astype(BF16)
            wd_bf[...] = wd_ref[0].astype(BF16)

        nxt = jnp.minimum(i + 1, n_used - 1)
        for r in range(tm):
            gather_copy(i, slot, r).wait()
        for r in range(tm):
            gather_copy(nxt, 1 - slot, r).start()
        x = _load_token_major(xbuf.at[slot], tm).astype(BF16)
        g = jnp.dot(x, wg_bf[...], preferred_element_type=F32) + bg_ref[0]
        u = jnp.dot(x, wu_bf[...], preferred_element_type=F32) + bu_ref[0]
        g = jnp.minimum(g, SWIGLU_LIMIT)
        u = jnp.clip(u, -SWIGLU_LIMIT, SWIGLU_LIMIT)
        a = g * jax.nn.sigmoid(SWIGLU_ALPHA * g) * (u + 1.0)
        y = jnp.dot(a.astype(BF16), wd_bf[...], preferred_element_type=F32) + bd_ref[0]
        _store_token_major(ybuf.at[slot], y)
        for r in range(tm):
            scatter_copy(i, slot, r).start()

        @pl.when(i == n_used - 1)
        def _():
            for r in range(tm):
                gather_copy(nxt, 1 - slot, r).wait()
            for r in range(tm):
                scatter_copy(i - 1, 1 - slot, r).wait()
            for r in range(tm):
                scatter_copy(i, slot, r).wait()


def _experts(blk_e, n_used, src, h_slabs, w_gate, b_gate, w_up, b_up, w_down, b_down, n_tiles):
    tm = TM_EXPERT
    d_ff = w_gate.shape[2]
    n_tok = h_slabs.shape[0]
    assert n_tok & (n_tok - 1) == 0, "assignment ids are split with a power-of-two mask"
    assert n_tiles >= 2
    wsel = lambda i, be, nu, sr: (be[i], 0, 0)
    return pl.pallas_call(
        functools.partial(_expert_body, n_tok=n_tok),
        grid_spec=pltpu.PrefetchScalarGridSpec(
            num_scalar_prefetch=3,
            grid=(n_tiles,),
            in_specs=[
                pl.BlockSpec(memory_space=pl.ANY),
                pl.BlockSpec((1, D_MODEL, d_ff), wsel),
                pl.BlockSpec((1, 1, d_ff), wsel),
                pl.BlockSpec((1, D_MODEL, d_ff), wsel),
                pl.BlockSpec((1, 1, d_ff), wsel),
                pl.BlockSpec((1, d_ff, D_MODEL), wsel),
                pl.BlockSpec((1, 1, D_MODEL), wsel),
            ],
            out_specs=pl.BlockSpec(memory_space=pl.ANY),
            scratch_shapes=[pltpu.VMEM((2, tm * SLABS, LANES), F32),
                            pltpu.VMEM((2, tm * SLABS, LANES), F32),
                            pltpu.VMEM((D_MODEL, d_ff), BF16),
                            pltpu.VMEM((D_MODEL, d_ff), BF16),
                            pltpu.VMEM((d_ff, D_MODEL), BF16),
                            pltpu.SemaphoreType.DMA((2,)),
                            pltpu.SemaphoreType.DMA((2,))],
        ),
        out_shape=jax.ShapeDtypeStruct((TOP_K * n_tok + 2 * tm, SLABS, LANES), F32),
        compiler_params=pltpu.CompilerParams(
            dimension_semantics=("arbitrary",), vmem_limit_bytes=VMEM_LIMIT_BYTES),
        name="experts",
    )(blk_e, n_used, src, h_slabs, w_gate, b_gate, w_up, b_up, w_down, b_down)


def _combine_body(y0_ref, y1_ref, y2_ref, y3_ref, x1_ref, wgt_ref, g_ref, o_ref):
    tm = x1_ref.shape[0]
    wgt = wgt_ref[...]
    acc = x1_ref[...]
    for k, yk_ref in enumerate((y0_ref, y1_ref, y2_ref, y3_ref)):
        acc = acc + wgt[:, k:k + 1] * _load_token_major(yk_ref, tm)
    o_ref[...] = _rms_scale(acc, g_ref[...])


def _combine(y_slabs, x1, wgt_rows, g_final):
    T = x1.shape[0]
    tm = TM_MOVE
    steps = T // tm
    row_blk = lambda i: (i, 0)
    y_specs = [pl.BlockSpec((tm * SLABS, LANES), functools.partial(lambda i, k: (k * steps + i, 0), k=k))
               for k in range(TOP_K)]
    return pl.pallas_call(
        _combine_body,
        grid=(steps,),
        in_specs=y_specs + [
            pl.BlockSpec((tm, D_MODEL), row_blk),
            pl.BlockSpec((tm, TOP_K), row_blk),
            pl.BlockSpec((1, D_MODEL), lambda i: (0, 0)),
        ],
        out_specs=pl.BlockSpec((tm, D_MODEL), row_blk),
        out_shape=jax.ShapeDtypeStruct((T, D_MODEL), F32),
        compiler_params=pltpu.CompilerParams(
            dimension_semantics=("arbitrary",), vmem_limit_bytes=VMEM_LIMIT_BYTES),
        name="combine",
    )(y_slabs, y_slabs, y_slabs, y_slabs, x1, wgt_rows, g_final)


def _rope_tables(seq_len):
    half = HEAD_DIM // 2
    inv_freq = ROPE_THETA ** (-jnp.arange(half, dtype=F32) / half)
    ang = jnp.arange(seq_len, dtype=jnp.int32).astype(F32)[:, None] * inv_freq[None, :]
    cos = jnp.cos(ang)
    sin = jnp.sin(ang)
    reps = LANES // HEAD_DIM
    cos_t = jnp.tile(jnp.concatenate([cos, cos], axis=-1), (1, reps))
    sin_t = jnp.tile(jnp.concatenate([-sin, sin], axis=-1), (1, reps))
    return cos_t, sin_t


def _router_hi_lo(w_router):
    hi = w_router.astype(BF16)
    lo = (w_router - hi.astype(F32)).astype(BF16)
    pad = jnp.zeros((w_router.shape[0], LANES - 2 * N_EXPERTS), BF16)
    return jnp.concatenate([hi, lo, pad], axis=1)


def _layer(x2, batch, seq_len, g_mix, w_in, b_in, sinks, w_conv, w_attn_o, w_conv_o, w_out,
           g_ffn, w_router, b_router, w_gate, b_gate, w_up, b_up, w_down, b_down, g_out):
    T = x2.shape[0]
    cos_t, sin_t = _rope_tables(seq_len)
    q, k, v, z, sa, sc = _inproj(x2, g_mix[None, :], w_in.astype(BF16), b_in[None, :],
                                 cos_t, sin_t, w_conv, seq_len)
    attn = _attention(q, k, v, sinks, batch, seq_len)
    x1, h, idx_t, wgt_t = _mixout(
        x2, attn, z, sa, sc, w_attn_o.astype(BF16), w_conv_o.astype(BF16), w_out.astype(BF16),
        g_ffn[None, :], _router_hi_lo(w_router), b_router[:, None])

    rank_t, cnt = _rank(idx_t)
    counts = cnt[:, 0]
    padded = (counts + TM_EXPERT - 1) // TM_EXPERT * TM_EXPERT
    pad_end = jnp.cumsum(padded)
    pad_start = pad_end - padded
    n_tiles = (T * TOP_K) // TM_EXPERT + N_EXPERTS
    tile_row = jnp.arange(n_tiles, dtype=jnp.int32) * TM_EXPERT
    blk_e = jnp.minimum(jnp.sum(pad_end[None, :] <= tile_row[:, None], axis=1),
                        N_EXPERTS - 1).astype(jnp.int32)
    n_used = (pad_end[-1:] // TM_EXPERT).astype(jnp.int32)
    e_ids = jnp.arange(N_EXPERTS, dtype=jnp.int32)
    dest = rank_t + jnp.sum(
        jnp.where(idx_t[:, :, None] == e_ids[None, None, :], pad_start[None, None, :], 0), axis=-1)
    dest_flat = dest.reshape(-1).astype(jnp.int32)
    fill_start = (pad_start + counts).astype(jnp.int32)

    src = _invperm(dest_flat, fill_start, pad_end.astype(jnp.int32), n_tiles * TM_EXPERT)
    y_slabs = _experts(blk_e, n_used, src, h.reshape(T, SLABS, LANES), w_gate, b_gate[:, None, :],
                       w_up, b_up[:, None, :], w_down, b_down[:, None, :], n_tiles)
    return _combine(y_slabs.reshape(-1, LANES), x1, wgt_t.T, g_out[None, :])


def kernel(x, g_mix, w_in, b_in, sinks, w_conv, w_attn_o, w_conv_o, w_out, g_ffn, w_router,
           b_router, w_gate, b_gate, w_up, b_up, w_down, b_down, g_final):
    batch, seq_len, d = x.shape
    depth = g_mix.shape[0]
    assert depth == 1, "the final norm is fused into the single layer's combine step"
    x2 = x.reshape(batch * seq_len, d)
    out = _layer(x2, batch, seq_len, g_mix[0], w_in[0], b_in[0], sinks[0], w_conv[0],
                 w_attn_o[0], w_conv_o[0], w_out[0], g_ffn[0], w_router[0], b_router[0],
                 w_gate[0], b_gate[0], w_up[0], b_up[0], w_down[0], b_down[0], g_final)
    return out.reshape(batch, seq_len, d)
```

```python
import functools

import jax
import jax.numpy as jnp
from jax import lax
from jax.experimental import pallas as pl
from jax.experimental.pallas import tpu as pltpu

D_MODEL = 1024
HEAD_DIM = 64
N_Q_HEADS = 16
N_KV_HEADS = 4
Q_PER_KV = N_Q_HEADS // N_KV_HEADS
ATTN_WIDTH = N_Q_HEADS * HEAD_DIM
KV_WIDTH = N_KV_HEADS * HEAD_DIM
ATTN_BLOCK = 128
WINDOW = 128
ROPE_THETA = 10000.0
CONV_WIDTH = D_MODEL
CONV_KERNEL = 3
N_EXPERTS = 32
TOP_K = 4
SWIGLU_LIMIT = 7.0
SWIGLU_ALPHA = 1.702
RMS_EPS = 1e-5

OFF_Q = 0
OFF_K = OFF_Q + ATTN_WIDTH
OFF_V = OFF_K + KV_WIDTH
OFF_CB = OFF_V + KV_WIDTH
OFF_CC = OFF_CB + CONV_WIDTH
OFF_CX = OFF_CC + CONV_WIDTH
OFF_GA = OFF_CX + CONV_WIDTH
OFF_GC = OFF_GA + D_MODEL
IN_WIDTH = OFF_GC + D_MODEL

LANES = 128
SUBLANES = 8
VMEM_LIMIT_BYTES = 56 * 1024 * 1024
EXPERT_VMEM_LIMIT_BYTES = 60 * 1024 * 1024

TM_INPROJ = 512
COL_CHUNK = 512
TM_MIX = 512
TC_RANK = 512
TM_EXPERT = 256
TM_MOVE = 512

BF16 = jnp.bfloat16
F32 = jnp.float32
NEG_BIG = -1e30


def _rms_scale(x, g):
    ms = jnp.mean(x * x, axis=-1, keepdims=True)
    return (x * lax.rsqrt(ms + RMS_EPS)) * g


SLABS = D_MODEL // LANES


def _store_token_major(ref, val):
    tm = val.shape[0]
    for s in range(SLABS):
        ref[pl.ds(s, tm, stride=SLABS), :] = val[:, s * LANES:(s + 1) * LANES]


def _load_token_major(ref, tm):
    return jnp.concatenate(
        [ref[pl.ds(s, tm, stride=SLABS), :] for s in range(SLABS)], axis=1)


def _inproj_body(x_ref, g_ref, w_ref, b_ref, cos_ref, sin_ref, wc_ref,
                 q_ref, k_ref, v_ref, z_ref, sa_ref, sc_ref, carry_ref, *, tiles_per_seq):
    tm = x_ref.shape[0]
    i = pl.program_id(0)
    h = _rms_scale(x_ref[...], g_ref[...]).astype(BF16)

    def proj(c0, width):
        return (jnp.dot(h, w_ref[:, c0:c0 + width], preferred_element_type=F32)
                + b_ref[:, c0:c0 + width])

    cos = cos_ref[...]
    sin = sin_ref[...]
    lane = lax.broadcasted_iota(jnp.int32, (tm, LANES), 1)
    first_half = (lane & (HEAD_DIM // 2)) == 0

    def rope(t):
        partner = jnp.where(first_half,
                            pltpu.roll(t, LANES - HEAD_DIM // 2, 1),
                            pltpu.roll(t, HEAD_DIM // 2, 1))
        return t * cos + partner * sin

    for c in range(0, ATTN_WIDTH, COL_CHUNK):
        acc = proj(OFF_Q + c, COL_CHUNK)
        for j in range(0, COL_CHUNK, LANES):
            q_ref[:, c + j:c + j + LANES] = (
                rope(acc[:, j:j + LANES]) * (HEAD_DIM ** -0.5)).astype(BF16)

    acc = proj(OFF_K, 2 * KV_WIDTH)
    for j in range(0, KV_WIDTH, LANES):
        k_ref[:, j:j + LANES] = rope(acc[:, j:j + LANES]).astype(BF16)
    v_ref[...] = acc[:, KV_WIDTH:].astype(BF16)

    seq_start = (i % tiles_per_seq) == 0
    head = 2 * SUBLANES
    row = lax.broadcasted_iota(jnp.int32, (head, COL_CHUNK), 0)
    for c in range(0, CONV_WIDTH, COL_CHUNK):
        u = proj(OFF_CC + c, COL_CHUNK) * proj(OFF_CX + c, COL_CHUNK)
        cb = proj(OFF_CB + c, COL_CHUNK)
        w0 = wc_ref[0:1, c:c + COL_CHUNK]
        w1 = wc_ref[1:2, c:c + COL_CHUNK]
        w2 = wc_ref[2:3, c:c + COL_CHUNK]
        y = w0 * pltpu.roll(u, 2, 0) + w1 * pltpu.roll(u, 1, 0) + w2 * u
        z_ref[:, c:c + COL_CHUNK] = (cb * y).astype(BF16)
        prev = jnp.where(seq_start, 0.0, carry_ref[:, c:c + COL_CHUNK])
        pad = jnp.zeros((SUBLANES, COL_CHUNK), F32)
        uh = u[0:head]
        u1 = jnp.where(row < 1, jnp.concatenate([pltpu.roll(prev, 1, 0), pad], 0),
                       pltpu.roll(uh, 1, 0))
        u2 = jnp.where(row < 2, jnp.concatenate([pltpu.roll(prev, 2, 0), pad], 0),
                       pltpu.roll(uh, 2, 0))
        yh = w0 * u2 + w1 * u1 + w2 * uh
        z_ref[0:head, c:c + COL_CHUNK] = (cb[0:head] * yh).astype(BF16)
        carry_ref[:, c:c + COL_CHUNK] = u[tm - SUBLANES:tm]

    for c in range(0, D_MODEL, COL_CHUNK):
        sa_ref[:, c:c + COL_CHUNK] = jax.nn.sigmoid(proj(OFF_GA + c, COL_CHUNK)).astype(BF16)
        sc_ref[:, c:c + COL_CHUNK] = jax.nn.sigmoid(proj(OFF_GC + c, COL_CHUNK)).astype(BF16)


def _inproj(x2, g_mix, w_in_bf, b_in, cos_t, sin_t, w_conv, seq_len):
    T = x2.shape[0]
    tm = TM_INPROJ
    tiles_per_seq = seq_len // tm
    const = lambda i: (0, 0)
    row_blk = lambda i: (i, 0)
    pos_blk = lambda i: (i % tiles_per_seq, 0)
    return pl.pallas_call(
        functools.partial(_inproj_body, tiles_per_seq=tiles_per_seq),
        grid=(T // tm,),
        in_specs=[
            pl.BlockSpec((tm, D_MODEL), row_blk),
            pl.BlockSpec((1, D_MODEL), const),
            pl.BlockSpec((D_MODEL, IN_WIDTH), const, pipeline_mode=pl.Buffered(1)),
            pl.BlockSpec((1, IN_WIDTH), const),
            pl.BlockSpec((tm, LANES), pos_blk),
            pl.BlockSpec((tm, LANES), pos_blk),
            pl.BlockSpec((CONV_KERNEL, CONV_WIDTH), const),
        ],
        out_specs=[
            pl.BlockSpec((tm, ATTN_WIDTH), row_blk),
            pl.BlockSpec((tm, KV_WIDTH), row_blk),
            pl.BlockSpec((tm, KV_WIDTH), row_blk),
            pl.BlockSpec((tm, CONV_WIDTH), row_blk),
            pl.BlockSpec((tm, D_MODEL), row_blk),
            pl.BlockSpec((tm, D_MODEL), row_blk),
        ],
        out_shape=[
            jax.ShapeDtypeStruct((T, ATTN_WIDTH), BF16),
            jax.ShapeDtypeStruct((T, KV_WIDTH), BF16),
            jax.ShapeDtypeStruct((T, KV_WIDTH), BF16),
            jax.ShapeDtypeStruct((T, CONV_WIDTH), BF16),
            jax.ShapeDtypeStruct((T, D_MODEL), BF16),
            jax.ShapeDtypeStruct((T, D_MODEL), BF16),
        ],
        scratch_shapes=[pltpu.VMEM((SUBLANES, CONV_WIDTH), F32)],
        compiler_params=pltpu.CompilerParams(
            dimension_semantics=("arbitrary",), vmem_limit_bytes=VMEM_LIMIT_BYTES),
        name="inproj",
    )(x2, g_mix, w_in_bf, b_in, cos_t, sin_t, w_conv)


def _attn_body(sinks_ref, q_ref, kp_ref, kc_ref, vp_ref, vc_ref, o_ref):
    n = pl.program_id(1)
    blk = ATTN_BLOCK
    cols = Q_PER_KV * blk
    sj = lax.broadcasted_iota(jnp.int32, (2 * blk, cols), 0)
    qi = lax.broadcasted_iota(jnp.int32, (2 * blk, cols), 1) % blk
    dist = qi + blk - sj
    mask = (dist >= 0) & (dist < WINDOW) & ((sj >= blk) | (n > 0))
    k_all = jnp.concatenate([kp_ref[...], kc_ref[...]], axis=0)
    v_all = jnp.concatenate([vp_ref[...], vc_ref[...]], axis=0)
    v_t = v_all.astype(F32).T.astype(BF16)
    for kh in range(N_KV_HEADS):
        k_h = k_all[:, kh * HEAD_DIM:(kh + 1) * HEAD_DIM]
        vt_h = v_t[kh * HEAD_DIM:(kh + 1) * HEAD_DIM, :]
        heads = [kh * Q_PER_KV + g for g in range(Q_PER_KV)]
        q_g = jnp.concatenate(
            [q_ref[:, hq * HEAD_DIM:(hq + 1) * HEAD_DIM] for hq in heads], axis=0)
        sink = jnp.concatenate(
            [jnp.full((1, blk), sinks_ref[hq], F32) for hq in heads], axis=1)
        s = lax.dot_general(k_h, q_g, (((1,), (1,)), ((), ())), preferred_element_type=F32)
        s = jnp.where(mask, s, NEG_BIG)
        m = jnp.maximum(jnp.max(s, axis=0, keepdims=True), sink)
        p = jnp.exp(s - m)
        denom = jnp.sum(p, axis=0, keepdims=True) + jnp.exp(sink - m)
        o_t = jnp.dot(vt_h, p.astype(BF16), preferred_element_type=F32) / denom
        for g in range(0, Q_PER_KV, 2):
            pair = jnp.concatenate(
                [o_t[:, g * blk:(g + 1) * blk], o_t[:, (g + 1) * blk:(g + 2) * blk]], axis=0)
            c0 = heads[g] * HEAD_DIM
            o_ref[:, c0:c0 + 2 * HEAD_DIM] = pair.T.astype(BF16)


def _attention(q, k, v, sinks, batch, seq_len):
    T = q.shape[0]
    nb = seq_len // ATTN_BLOCK
    cur = lambda b, n: (b * nb + n, 0)
    prev = lambda b, n: (b * nb + jnp.maximum(n - 1, 0), 0)
    return pl.pallas_call(
        _attn_body,
        grid=(batch, nb),
        in_specs=[
            pl.BlockSpec(memory_space=pltpu.SMEM),
            pl.BlockSpec((ATTN_BLOCK, ATTN_WIDTH), cur),
            pl.BlockSpec((ATTN_BLOCK, KV_WIDTH), prev),
            pl.BlockSpec((ATTN_BLOCK, KV_WIDTH), cur),
            pl.BlockSpec((ATTN_BLOCK, KV_WIDTH), prev),
            pl.BlockSpec((ATTN_BLOCK, KV_WIDTH), cur),
        ],
        out_specs=pl.BlockSpec((ATTN_BLOCK, ATTN_WIDTH), cur),
        out_shape=jax.ShapeDtypeStruct((T, ATTN_WIDTH), BF16),
        compiler_params=pltpu.CompilerParams(
            dimension_semantics=("arbitrary", "arbitrary"), vmem_limit_bytes=VMEM_LIMIT_BYTES),
        name="attn",
    )(sinks, q, k, k, v, v)


def _mixout_body(x_ref, a_ref, z_ref, sa_ref, sc_ref, wa_ref, wc_ref, wo_ref, g_ref,
                 wr_ref, br_ref, x1_ref, hp_ref, idx_ref, wgt_ref):
    tm = x_ref.shape[0]
    y_attn = jnp.dot(a_ref[...], wa_ref[...], preferred_element_type=F32)
    y_conv = jnp.dot(z_ref[...], wc_ref[...], preferred_element_type=F32)
    merged = sa_ref[...].astype(F32) * y_attn + sc_ref[...].astype(F32) * y_conv
    x1 = x_ref[...] + jnp.dot(merged.astype(BF16), wo_ref[...], preferred_element_type=F32)
    x1_ref[...] = x1
    h = _rms_scale(x1, g_ref[...])
    h_bf = h.astype(BF16)
    half = D_MODEL // 2
    hp_ref[:, 0, :] = h_bf[:, :half]
    hp_ref[:, 1, :] = h_bf[:, half:]
    h_hi = h.astype(BF16)
    h_lo = (h - h_hi.astype(F32)).astype(BF16)
    p_hi = jnp.dot(h_hi, wr_ref[...], preferred_element_type=F32)
    p_lo = jnp.dot(h_lo, wr_ref[...], preferred_element_type=F32)
    lg = p_hi + pltpu.roll(p_hi, LANES - N_EXPERTS, 1) + p_lo
    logits = lg.T[0:N_EXPERTS, :] + br_ref[...]
    e_iota = lax.broadcasted_iota(jnp.int32, (N_EXPERTS, tm), 0)
    vals, idxs = [], []
    for _ in range(TOP_K):
        m = jnp.max(logits, axis=0, keepdims=True)
        idx = jnp.min(jnp.where(logits == m, e_iota, N_EXPERTS), axis=0, keepdims=True)
        vals.append(m)
        idxs.append(idx)
        logits = jnp.where(e_iota == idx, -jnp.inf, logits)
    ex = [jnp.exp(v - vals[0]) for v in vals]
    tot = ex[0] + ex[1] + ex[2] + ex[3]
    idx_ref[...] = jnp.concatenate(idxs, axis=0)
    wgt_ref[...] = jnp.concatenate([e / tot for e in ex], axis=0)


def _mixout(x2, attn, z, sa, sc, wa_bf, wc_bf, wo_bf, g_ffn, wr_t, b_router):
    T = x2.shape[0]
    tm = TM_MIX
    const = lambda i: (0, 0)
    row_blk = lambda i: (i, 0)
    col_blk = lambda i: (0, i)
    act = pl.BlockSpec((tm, D_MODEL), row_blk)
    wsq = pl.BlockSpec((D_MODEL, D_MODEL), const)
    return pl.pallas_call(
        _mixout_body,
        grid=(T // tm,),
        in_specs=[act, act, act, act, act, wsq, wsq, wsq,
                  pl.BlockSpec((1, D_MODEL), const),
                  pl.BlockSpec((D_MODEL, LANES), const),
                  pl.BlockSpec((N_EXPERTS, 1), const)],
        out_specs=[act,
                   pl.BlockSpec((tm, 2, D_MODEL // 2), lambda i: (i, 0, 0)),
                   pl.BlockSpec((TOP_K, tm), col_blk),
                   pl.BlockSpec((TOP_K, tm), col_blk)],
        out_shape=[jax.ShapeDtypeStruct((T, D_MODEL), F32),
                   jax.ShapeDtypeStruct((T, 2, D_MODEL // 2), BF16),
                   jax.ShapeDtypeStruct((TOP_K, T), jnp.int32),
                   jax.ShapeDtypeStruct((TOP_K, T), F32)],
        compiler_params=pltpu.CompilerParams(
            dimension_semantics=("arbitrary",), vmem_limit_bytes=VMEM_LIMIT_BYTES),
        name="mixout",
    )(x2, attn, z, sa, sc, wa_bf, wc_bf, wo_bf, g_ffn, wr_t, b_router)


def _rank_body(idx_ref, rank_ref, cnt_ref, carry_ref):
    tc = idx_ref.shape[1]
    i = pl.program_id(0)

    @pl.when(i == 0)
    def _():
        carry_ref[...] = jnp.zeros_like(carry_ref)

    idx = idx_ref[...]
    e_iota = lax.broadcasted_iota(jnp.int32, (N_EXPERTS, tc), 0)
    sel = [e_iota == idx[k:k + 1, :] for k in range(TOP_K)]
    member = (sel[0] | sel[1] | sel[2] | sel[3])
    onehot = jnp.where(member, 1.0, 0.0).astype(BF16)
    r = lax.broadcasted_iota(jnp.int32, (tc, tc), 0)
    c = lax.broadcasted_iota(jnp.int32, (tc, tc), 1)
    before = jnp.where(r < c, 1.0, 0.0).astype(BF16)
    carry = carry_ref[:, 0:1]
    prefix = jnp.dot(onehot, before, preferred_element_type=F32) + carry
    ranks = [jnp.sum(jnp.where(sel[k], prefix, 0.0), axis=0, keepdims=True)
             for k in range(TOP_K)]
    rank_ref[...] = jnp.concatenate(ranks, axis=0).astype(jnp.int32)
    total = carry + jnp.sum(onehot.astype(F32), axis=1, keepdims=True)
    carry_ref[...] = jnp.broadcast_to(total, carry_ref.shape)
    cnt_ref[...] = jnp.broadcast_to(total, cnt_ref.shape).astype(jnp.int32)


def _rank(idx_t):
    T = idx_t.shape[1]
    tc = TC_RANK
    return pl.pallas_call(
        _rank_body,
        grid=(T // tc,),
        in_specs=[pl.BlockSpec((TOP_K, tc), lambda i: (0, i))],
        out_specs=[pl.BlockSpec((TOP_K, tc), lambda i: (0, i)),
                   pl.BlockSpec((N_EXPERTS, LANES), lambda i: (0, 0))],
        out_shape=[jax.ShapeDtypeStruct((TOP_K, T), jnp.int32),
                   jax.ShapeDtypeStruct((N_EXPERTS, LANES), jnp.int32)],
        scratch_shapes=[pltpu.VMEM((N_EXPERTS, LANES), F32)],
        compiler_params=pltpu.CompilerParams(dimension_semantics=("arbitrary",)),
        name="rank",
    )(idx_t)


def _invperm_body(dest_ref, fill_ref, end_ref, src_ref, *, n_assign, n_slots):
    def mark_tile(start):
        def mark(q, carry):
            p = start + q
            src_ref[p] = n_assign + (p & (2 * TM_EXPERT - 1))
            return carry
        lax.fori_loop(0, TM_EXPERT, mark, 0, unroll=16)

    def mark_expert_tail(e, carry):
        mark_tile(fill_ref[e])
        return carry

    lax.fori_loop(0, N_EXPERTS, mark_expert_tail, 0)

    def mark_unused(t, carry):
        mark_tile(t * TM_EXPERT)
        return carry

    lax.fori_loop(end_ref[N_EXPERTS - 1] // TM_EXPERT, n_slots // TM_EXPERT + 1, mark_unused, 0)

    def place(a, carry):
        src_ref[dest_ref[a]] = a
        return carry

    lax.fori_loop(0, n_assign, place, 0, unroll=16)


def _invperm(dest_flat, fill_start, pad_end, n_slots):
    return pl.pallas_call(
        functools.partial(_invperm_body, n_assign=dest_flat.shape[0], n_slots=n_slots),
        grid_spec=pltpu.PrefetchScalarGridSpec(
            num_scalar_prefetch=3,
            grid=(1,),
            in_specs=[],
            out_specs=pl.BlockSpec(memory_space=pltpu.SMEM),
        ),
        out_shape=jax.ShapeDtypeStruct((n_slots + TM_EXPERT,), jnp.int32),
        compiler_params=pltpu.CompilerParams(dimension_semantics=("arbitrary",)),
        name="invperm",
    )(dest_flat, fill_start, pad_end)


def _expert_body(blk_e_ref, nxt_e_ref, n_used_ref, src_ref, hp_ref, wg_hbm, wu_hbm, wd_hbm,
                 bg_ref, bu_ref, bd_ref, y_ref, stage, xbuf, ybuf, wst, w_bf, wsem, ssem, *,
                 n_tok):
    tm = TM_EXPERT
    i = pl.program_id(0)
    n_used = n_used_ref[0]
    slot = i % 2
    e = blk_e_ref[i]
    e_prev = blk_e_ref[jnp.maximum(i - 1, 0)]
    half = D_MODEL // 2

    def weight_copies(ex):
        return [pltpu.make_async_copy(w.at[ex], wst.at[m], wsem.at[m])
                for m, w in enumerate((wg_hbm, wu_hbm, wd_hbm))]

    def rows_of(buf):
        return pl.ds(pl.multiple_of(buf * tm, tm), tm)

    def gather_rows(tile, buf):
        for r in range(tm):
            a = src_ref[tile * tm + r]
            stage[r] = hp_ref[a & (n_tok - 1)]
        xbuf[rows_of(buf), :] = jnp.concatenate([stage[:, 0, :], stage[:, 1, :]], axis=1)

    def scatter_copy(tile, buf, r, to_spare):
        a = src_ref[tile * tm + r]
        if to_spare is not None:
            a = jnp.where(to_spare, TOP_K * n_tok + buf * tm + r, a)
        return pltpu.make_async_copy(ybuf.at[buf, pl.ds(r * SLABS, SLABS)], y_ref.at[a],
                                     ssem.at[buf])

    @pl.when(i == 0)
    def _():
        for c in weight_copies(e):
            c.start()
        gather_rows(0, 0)
        ybuf[...] = jnp.zeros_like(ybuf)

    @pl.when(i < n_used)
    def _():
        @pl.when(i >= 1)
        def _():
            for r in range(tm):
                scatter_copy(jnp.maximum(i - 2, 0), slot, r, i == 1).wait()

        @pl.when((i == 0) | (e != e_prev))
        def _():
            for c in weight_copies(e):
                c.wait()
            rows = D_MODEL // SUBLANES
            for m in range(3):
                def cast_rows(j, carry, m=m):
                    sl = pl.ds(pl.multiple_of(j * rows, rows), rows)
                    w_bf[m, sl, :] = wst[m, sl, :].astype(BF16)
                    return carry
                lax.fori_loop(0, SUBLANES, cast_rows, 0)
            nxt_e = nxt_e_ref[i]

            @pl.when(nxt_e >= 0)
            def _():
                for c in weight_copies(nxt_e):
                    c.start()

        x = xbuf[rows_of(slot), :]
        gather_rows(jnp.minimum(i + 1, n_used - 1), 1 - slot)
        for r in range(tm):
            scatter_copy(jnp.maximum(i - 1, 0), 1 - slot, r, i == 0).start()
        g = jnp.dot(x, w_bf[0], preferred_element_type=F32) + bg_ref[pl.ds(e, 1), :]
        u = jnp.dot(x, w_bf[1], preferred_element_type=F32) + bu_ref[pl.ds(e, 1), :]
        g = jnp.minimum(g, SWIGLU_LIMIT)
        u = jnp.clip(u, -SWIGLU_LIMIT, SWIGLU_LIMIT)
        a = g * jax.nn.sigmoid(SWIGLU_ALPHA * g) * (u + 1.0)
        y = jnp.dot(a.astype(BF16), w_bf[2], preferred_element_type=F32) + bd_ref[pl.ds(e, 1), :]
        _store_token_major(ybuf.at[slot], y)

        @pl.when(i == n_used - 1)
        def _():
            for r in range(tm):
                scatter_copy(i, slot, r, None).start()
            for r in range(tm):
                scatter_copy(jnp.maximum(i - 1, 0), 1 - slot, r, i == 0).wait()
            for r in range(tm):
                scatter_copy(i, slot, r, None).wait()


def _experts(blk_e, nxt_e, n_used, src, hp, w_gate, b_gate, w_up, b_up, w_down, b_down, n_tiles):
    tm = TM_EXPERT
    d_ff = w_gate.shape[2]
    assert d_ff == D_MODEL
    n_tok = hp.shape[0]
    assert n_tok & (n_tok - 1) == 0, "assignment ids are split with a power-of-two mask"
    assert n_tiles >= 2
    const2 = lambda i, *_: (0, 0)
    bias = pl.BlockSpec((N_EXPERTS, D_MODEL), const2)
    hbm = pl.BlockSpec(memory_space=pl.ANY)
    return pl.pallas_call(
        functools.partial(_expert_body, n_tok=n_tok),
        grid_spec=pltpu.PrefetchScalarGridSpec(
            num_scalar_prefetch=4,
            grid=(n_tiles,),
            in_specs=[
                pl.BlockSpec((n_tok, 2, D_MODEL // 2), lambda i, *_: (0, 0, 0),
                             pipeline_mode=pl.Buffered(1)),
                hbm, hbm, hbm, bias, bias, bias,
            ],
            out_specs=pl.BlockSpec(memory_space=pl.ANY),
            scratch_shapes=[pltpu.VMEM((tm, 2, D_MODEL // 2), BF16),
                            pltpu.VMEM((2 * tm, D_MODEL), BF16),
                            pltpu.VMEM((2, tm * SLABS, LANES), F32),
                            pltpu.VMEM((3, D_MODEL, D_MODEL), F32),
                            pltpu.VMEM((3, D_MODEL, D_MODEL), BF16),
                            pltpu.SemaphoreType.DMA((3,)),
                            pltpu.SemaphoreType.DMA((2,))],
        ),
        out_shape=jax.ShapeDtypeStruct((TOP_K * n_tok + 2 * tm, SLABS, LANES), F32),
        compiler_params=pltpu.CompilerParams(
            dimension_semantics=("arbitrary",), vmem_limit_bytes=EXPERT_VMEM_LIMIT_BYTES),
        name="experts",
    )(blk_e, nxt_e, n_used, src, hp, w_gate, w_up, w_down, b_gate, b_up, b_down)


def _combine_body(y0_ref, y1_ref, y2_ref, y3_ref, x1_ref, wgt_ref, g_ref, o_ref):
    tm = x1_ref.shape[0]
    wgt = wgt_ref[...]
    acc = x1_ref[...]
    for k, yk_ref in enumerate((y0_ref, y1_ref, y2_ref, y3_ref)):
        acc = acc + wgt[:, k:k + 1] * _load_token_major(yk_ref, tm)
    o_ref[...] = _rms_scale(acc, g_ref[...])


def _combine(y_slabs, x1, wgt_rows, g_final):
    T = x1.shape[0]
    tm = TM_MOVE
    steps = T // tm
    row_blk = lambda i: (i, 0)
    y_specs = [pl.BlockSpec((tm * SLABS, LANES), functools.partial(lambda i, k: (k * steps + i, 0), k=k))
               for k in range(TOP_K)]
    return pl.pallas_call(
        _combine_body,
        grid=(steps,),
        in_specs=y_specs + [
            pl.BlockSpec((tm, D_MODEL), row_blk),
            pl.BlockSpec((tm, TOP_K), row_blk),
            pl.BlockSpec((1, D_MODEL), lambda i: (0, 0)),
        ],
        out_specs=pl.BlockSpec((tm, D_MODEL), row_blk),
        out_shape=jax.ShapeDtypeStruct((T, D_MODEL), F32),
        compiler_params=pltpu.CompilerParams(
            dimension_semantics=("arbitrary",), vmem_limit_bytes=VMEM_LIMIT_BYTES),
        name="combine",
    )(y_slabs, y_slabs, y_slabs, y_slabs, x1, wgt_rows, g_final)


def _rope_tables(seq_len):
    half = HEAD_DIM // 2
    inv_freq = ROPE_THETA ** (-jnp.arange(half, dtype=F32) / half)
    ang = jnp.arange(seq_len, dtype=jnp.int32).astype(F32)[:, None] * inv_freq[None, :]
    cos = jnp.cos(ang)
    sin = jnp.sin(ang)
    reps = LANES // HEAD_DIM
    cos_t = jnp.tile(jnp.concatenate([cos, cos], axis=-1), (1, reps))
    sin_t = jnp.tile(jnp.concatenate([-sin, sin], axis=-1), (1, reps))
    return cos_t, sin_t


def _router_hi_lo(w_router):
    hi = w_router.astype(BF16)
    lo = (w_router - hi.astype(F32)).astype(BF16)
    pad = jnp.zeros((w_router.shape[0], LANES - 2 * N_EXPERTS), BF16)
    return jnp.concatenate([hi, lo, pad], axis=1)


def _layer(x2, batch, seq_len, g_mix, w_in, b_in, sinks, w_conv, w_attn_o, w_conv_o, w_out,
           g_ffn, w_router, b_router, w_gate, b_gate, w_up, b_up, w_down, b_down, g_out):
    T = x2.shape[0]
    cos_t, sin_t = _rope_tables(seq_len)
    q, k, v, z, sa, sc = _inproj(x2, g_mix[None, :], w_in.astype(BF16), b_in[None, :],
                                 cos_t, sin_t, w_conv, seq_len)
    attn = _attention(q, k, v, sinks, batch, seq_len)
    x1, hp, idx_t, wgt_t = _mixout(
        x2, attn, z, sa, sc, w_attn_o.astype(BF16), w_conv_o.astype(BF16), w_out.astype(BF16),
        g_ffn[None, :], _router_hi_lo(w_router), b_router[:, None])

    rank_t, cnt = _rank(idx_t)
    counts = cnt[:, 0]
    padded = (counts + TM_EXPERT - 1) // TM_EXPERT * TM_EXPERT
    pad_end = jnp.cumsum(padded)
    pad_start = pad_end - padded
    n_tiles = (T * TOP_K) // TM_EXPERT + N_EXPERTS
    tile_row = jnp.arange(n_tiles, dtype=jnp.int32) * TM_EXPERT
    blk_e = jnp.minimum(jnp.sum(pad_end[None, :] <= tile_row[:, None], axis=1),
                        N_EXPERTS - 1).astype(jnp.int32)
    n_used = (pad_end[-1:] // TM_EXPERT).astype(jnp.int32)
    e_ids = jnp.arange(N_EXPERTS, dtype=jnp.int32)
    dest = rank_t + jnp.sum(
        jnp.where(idx_t[:, :, None] == e_ids[None, None, :], pad_start[None, None, :], 0), axis=-1)
    dest_flat = dest.reshape(-1).astype(jnp.int32)
    fill_start = (pad_start + counts).astype(jnp.int32)

    next_tile = pad_end[blk_e] // TM_EXPERT
    nxt_e = jnp.where(next_tile < n_used[0],
                      blk_e[jnp.minimum(next_tile, n_tiles - 1)], -1).astype(jnp.int32)

    src = _invperm(dest_flat, fill_start, pad_end.astype(jnp.int32), n_tiles * TM_EXPERT)
    y_slabs = _experts(blk_e, nxt_e, n_used, src, hp, w_gate, b_gate, w_up, b_up,
                       w_down, b_down, n_tiles)
    return _combine(y_slabs.reshape(-1, LANES), x1, wgt_t.T, g_out[None, :])


def kernel(x, g_mix, w_in, b_in, sinks, w_conv, w_attn_o, w_conv_o, w_out, g_ffn, w_router,
           b_router, w_gate, b_gate, w_up, b_up, w_down, b_down, g_final):
    batch, seq_len, d = x.shape
    depth = g_mix.shape[0]
    assert depth == 1, "the final norm is fused into the single layer's combine step"
    x2 = x.reshape(batch * seq_len, d)
    out = _layer(x2, batch, seq_len, g_mix[0], w_in[0], b_in[0], sinks[0], w_conv[0],
                 w_attn_o[0], w_conv_o[0], w_out[0], g_ffn[0], w_router[0], b_router[0],
                 w_gate[0], b_gate[0], w_up[0], b_up[0], w_down[0], b_down[0], g_final)
    return out.reshape(batch, seq_len, d)
```

```python
import functools

import jax
import jax.numpy as jnp
from jax import lax
from jax.experimental import pallas as pl
from jax.experimental.pallas import tpu as pltpu

D_MODEL = 1024
HEAD_DIM = 64
N_Q_HEADS = 16
N_KV_HEADS = 4
Q_PER_KV = N_Q_HEADS // N_KV_HEADS
ATTN_WIDTH = N_Q_HEADS * HEAD_DIM
KV_WIDTH = N_KV_HEADS * HEAD_DIM
ATTN_BLOCK = 128
WINDOW = 128
ROPE_THETA = 10000.0
CONV_WIDTH = D_MODEL
CONV_KERNEL = 3
N_EXPERTS = 32
TOP_K = 4
SWIGLU_LIMIT = 7.0
SWIGLU_ALPHA = 1.702
RMS_EPS = 1e-5

OFF_Q = 0
OFF_K = OFF_Q + ATTN_WIDTH
OFF_V = OFF_K + KV_WIDTH
OFF_CB = OFF_V + KV_WIDTH
OFF_CC = OFF_CB + CONV_WIDTH
OFF_CX = OFF_CC + CONV_WIDTH
OFF_GA = OFF_CX + CONV_WIDTH
OFF_GC = OFF_GA + D_MODEL
IN_WIDTH = OFF_GC + D_MODEL

LANES = 128
SUBLANES = 8
VMEM_LIMIT_BYTES = 56 * 1024 * 1024
EXPERT_VMEM_LIMIT_BYTES = 60 * 1024 * 1024

TM_INPROJ = 512
COL_CHUNK = 512
Q_BLOCKS_PER_STEP = 2
TM_MIX = 512
TC_RANK = 512
TM_EXPERT = 256
TM_MOVE = 512
MOVE_GROUP = 32

BF16 = jnp.bfloat16
F32 = jnp.float32
NEG_BIG = -1e30


def _rms_scale(x, g):
    ms = jnp.mean(x * x, axis=-1, keepdims=True)
    return (x * lax.rsqrt(ms + RMS_EPS)) * g


SLABS = D_MODEL // LANES


def _store_token_major(ref, val):
    tm = val.shape[0]
    for s in range(SLABS):
        ref[pl.ds(s, tm, stride=SLABS), :] = val[:, s * LANES:(s + 1) * LANES]


def _load_token_major(ref, tm):
    return jnp.concatenate(
        [ref[pl.ds(s, tm, stride=SLABS), :] for s in range(SLABS)], axis=1)


def _inproj_body(x_ref, g_ref, w_ref, b_ref, cos_ref, sin_ref, wc_ref,
                 q_ref, k_ref, v_ref, z_ref, sa_ref, sc_ref, carry_ref, *, tiles_per_seq):
    tm = x_ref.shape[0]
    i = pl.program_id(0)
    h = _rms_scale(x_ref[...], g_ref[...]).astype(BF16)

    def proj(c0, width):
        return (jnp.dot(h, w_ref[:, c0:c0 + width], preferred_element_type=F32)
                + b_ref[:, c0:c0 + width])

    cos = cos_ref[...]
    sin = sin_ref[...]
    lane = lax.broadcasted_iota(jnp.int32, (tm, LANES), 1)
    first_half = (lane & (HEAD_DIM // 2)) == 0

    def rope(t):
        partner = jnp.where(first_half,
                            pltpu.roll(t, LANES - HEAD_DIM // 2, 1),
                            pltpu.roll(t, HEAD_DIM // 2, 1))
        return t * cos + partner * sin

    for c in range(0, ATTN_WIDTH, COL_CHUNK):
        acc = proj(OFF_Q + c, COL_CHUNK)
        for j in range(0, COL_CHUNK, LANES):
            q_ref[:, c + j:c + j + LANES] = (
                rope(acc[:, j:j + LANES]) * (HEAD_DIM ** -0.5)).astype(BF16)

    acc = proj(OFF_K, 2 * KV_WIDTH)
    for j in range(0, KV_WIDTH, LANES):
        k_ref[:, j:j + LANES] = rope(acc[:, j:j + LANES]).astype(BF16)
    v_ref[...] = acc[:, KV_WIDTH:].astype(BF16)

    seq_start = (i % tiles_per_seq) == 0
    head = 2 * SUBLANES
    row = lax.broadcasted_iota(jnp.int32, (head, COL_CHUNK), 0)
    for c in range(0, CONV_WIDTH, COL_CHUNK):
        u = proj(OFF_CC + c, COL_CHUNK) * proj(OFF_CX + c, COL_CHUNK)
        cb = proj(OFF_CB + c, COL_CHUNK)
        w0 = wc_ref[0:1, c:c + COL_CHUNK]
        w1 = wc_ref[1:2, c:c + COL_CHUNK]
        w2 = wc_ref[2:3, c:c + COL_CHUNK]
        y = w0 * pltpu.roll(u, 2, 0) + w1 * pltpu.roll(u, 1, 0) + w2 * u
        z_ref[:, c:c + COL_CHUNK] = (cb * y).astype(BF16)
        prev = jnp.where(seq_start, 0.0, carry_ref[:, c:c + COL_CHUNK])
        pad = jnp.zeros((SUBLANES, COL_CHUNK), F32)
        uh = u[0:head]
        u1 = jnp.where(row < 1, jnp.concatenate([pltpu.roll(prev, 1, 0), pad], 0),
                       pltpu.roll(uh, 1, 0))
        u2 = jnp.where(row < 2, jnp.concatenate([pltpu.roll(prev, 2, 0), pad], 0),
                       pltpu.roll(uh, 2, 0))
        yh = w0 * u2 + w1 * u1 + w2 * uh
        z_ref[0:head, c:c + COL_CHUNK] = (cb[0:head] * yh).astype(BF16)
        carry_ref[:, c:c + COL_CHUNK] = u[tm - SUBLANES:tm]

    for c in range(0, D_MODEL, COL_CHUNK):
        sa_ref[:, c:c + COL_CHUNK] = jax.nn.sigmoid(proj(OFF_GA + c, COL_CHUNK)).astype(BF16)
        sc_ref[:, c:c + COL_CHUNK] = jax.nn.sigmoid(proj(OFF_GC + c, COL_CHUNK)).astype(BF16)


def _inproj(x2, g_mix, w_in_bf, b_in, cos_t, sin_t, w_conv, seq_len):
    T = x2.shape[0]
    tm = TM_INPROJ
    tiles_per_seq = seq_len // tm
    const = lambda i: (0, 0)
    row_blk = lambda i: (i, 0)
    pos_blk = lambda i: (i % tiles_per_seq, 0)
    return pl.pallas_call(
        functools.partial(_inproj_body, tiles_per_seq=tiles_per_seq),
        grid=(T // tm,),
        in_specs=[
            pl.BlockSpec((tm, D_MODEL), row_blk),
            pl.BlockSpec((1, D_MODEL), const),
            pl.BlockSpec((D_MODEL, IN_WIDTH), const, pipeline_mode=pl.Buffered(1)),
            pl.BlockSpec((1, IN_WIDTH), const),
            pl.BlockSpec((tm, LANES), pos_blk),
            pl.BlockSpec((tm, LANES), pos_blk),
            pl.BlockSpec((CONV_KERNEL, CONV_WIDTH), const),
        ],
        out_specs=[
            pl.BlockSpec((tm, ATTN_WIDTH), row_blk),
            pl.BlockSpec((tm, KV_WIDTH), row_blk),
            pl.BlockSpec((tm, KV_WIDTH), row_blk),
            pl.BlockSpec((tm, CONV_WIDTH), row_blk),
            pl.BlockSpec((tm, D_MODEL), row_blk),
            pl.BlockSpec((tm, D_MODEL), row_blk),
        ],
        out_shape=[
            jax.ShapeDtypeStruct((T, ATTN_WIDTH), BF16),
            jax.ShapeDtypeStruct((T, KV_WIDTH), BF16),
            jax.ShapeDtypeStruct((T, KV_WIDTH), BF16),
            jax.ShapeDtypeStruct((T, CONV_WIDTH), BF16),
            jax.ShapeDtypeStruct((T, D_MODEL), BF16),
            jax.ShapeDtypeStruct((T, D_MODEL), BF16),
        ],
        scratch_shapes=[pltpu.VMEM((SUBLANES, CONV_WIDTH), F32)],
        compiler_params=pltpu.CompilerParams(
            dimension_semantics=("arbitrary",), vmem_limit_bytes=VMEM_LIMIT_BYTES),
        name="inproj",
    )(x2, g_mix, w_in_bf, b_in, cos_t, sin_t, w_conv)


def _attn_body(sinks_ref, q_ref, kp_ref, kc_ref, vp_ref, vc_ref, o_ref):
    n = pl.program_id(1)
    blk = ATTN_BLOCK
    cols = Q_PER_KV * blk
    j = lax.broadcasted_iota(jnp.int32, (blk, cols), 0)
    qi = lax.broadcasted_iota(jnp.int32, (blk, cols), 1) % blk
    tri = j > qi
    no_prev = tri & (n == 0)
    k3 = jnp.concatenate([kp_ref[...], kc_ref[...]], axis=0)
    v3 = jnp.concatenate([vp_ref[...], vc_ref[...]], axis=0)
    v3_t = v3.astype(F32).T.astype(BF16)
    for qb in range(Q_BLOCKS_PER_STEP):
        k_all = k3[qb * blk:(qb + 2) * blk]
        v_t = v3_t[:, qb * blk:(qb + 2) * blk]
        for kh in range(N_KV_HEADS):
            k_h = k_all[:, kh * HEAD_DIM:(kh + 1) * HEAD_DIM]
            vt_h = v_t[kh * HEAD_DIM:(kh + 1) * HEAD_DIM, :]
            heads = [kh * Q_PER_KV + g for g in range(Q_PER_KV)]
            q_g = jnp.concatenate(
                [q_ref[qb * blk:(qb + 1) * blk, hq * HEAD_DIM:(hq + 1) * HEAD_DIM]
                 for hq in heads], axis=0)
            sink = jnp.concatenate(
                [jnp.full((1, blk), sinks_ref[hq], F32) for hq in heads], axis=1)
            s = lax.dot_general(k_h, q_g, (((1,), (1,)), ((), ())), preferred_element_type=F32)
            fold = jnp.where(tri, s[:blk], s[blk:])
            if qb == 0:
                fold = jnp.where(no_prev, NEG_BIG, fold)
            m = jnp.maximum(jnp.max(fold, axis=0, keepdims=True), sink)
            p = jnp.exp(fold - m)
            denom = jnp.sum(p, axis=0, keepdims=True) + jnp.exp(sink - m)
            p2 = jnp.concatenate([jnp.where(tri, p, 0.0), jnp.where(tri, 0.0, p)], axis=0)
            o_t = jnp.dot(vt_h, p2.astype(BF16), preferred_element_type=F32) / denom
            for g in range(0, Q_PER_KV, 2):
                pair = jnp.concatenate(
                    [o_t[:, g * blk:(g + 1) * blk], o_t[:, (g + 1) * blk:(g + 2) * blk]], axis=0)
                c0 = heads[g] * HEAD_DIM
                o_ref[qb * blk:(qb + 1) * blk, c0:c0 + 2 * HEAD_DIM] = pair.T.astype(BF16)


def _attention(q, k, v, sinks, batch, seq_len):
    T = q.shape[0]
    assert WINDOW == ATTN_BLOCK, "the folded score tile needs window == block"
    rows = Q_BLOCKS_PER_STEP * ATTN_BLOCK
    steps = seq_len // rows
    cur = lambda b, n: (b * steps + n, 0)
    prev = lambda b, n: ((b * steps + n) * Q_BLOCKS_PER_STEP - jnp.minimum(n, 1), 0)
    return pl.pallas_call(
        _attn_body,
        grid=(batch, steps),
        in_specs=[
            pl.BlockSpec(memory_space=pltpu.SMEM),
            pl.BlockSpec((rows, ATTN_WIDTH), cur),
            pl.BlockSpec((ATTN_BLOCK, KV_WIDTH), prev),
            pl.BlockSpec((rows, KV_WIDTH), cur),
            pl.BlockSpec((ATTN_BLOCK, KV_WIDTH), prev),
            pl.BlockSpec((rows, KV_WIDTH), cur),
        ],
        out_specs=pl.BlockSpec((rows, ATTN_WIDTH), cur),
        out_shape=jax.ShapeDtypeStruct((T, ATTN_WIDTH), BF16),
        compiler_params=pltpu.CompilerParams(
            dimension_semantics=("arbitrary", "arbitrary"), vmem_limit_bytes=VMEM_LIMIT_BYTES),
        name="attn",
    )(sinks, q, k, k, v, v)


def _mixout_body(x_ref, a_ref, z_ref, sa_ref, sc_ref, wa_ref, wc_ref, wo_ref, g_ref,
                 wr_ref, br_ref, x1_ref, hp_ref, idx_ref, wgt_ref, slab_ref):
    tm = x_ref.shape[0]
    y_attn = jnp.dot(a_ref[...], wa_ref[...], preferred_element_type=F32)
    y_conv = jnp.dot(z_ref[...], wc_ref[...], preferred_element_type=F32)
    merged = sa_ref[...].astype(F32) * y_attn + sc_ref[...].astype(F32) * y_conv
    x1 = x_ref[...] + jnp.dot(merged.astype(BF16), wo_ref[...], preferred_element_type=F32)
    x1_ref[...] = x1
    h = _rms_scale(x1, g_ref[...])
    _store_token_major(slab_ref, h)
    hp_ref[...] = slab_ref[...].astype(BF16)
    h_hi = h.astype(BF16)
    h_lo = (h - h_hi.astype(F32)).astype(BF16)
    p_hi = jnp.dot(h_hi, wr_ref[...], preferred_element_type=F32)
    p_lo = jnp.dot(h_lo, wr_ref[...], preferred_element_type=F32)
    lg = p_hi + pltpu.roll(p_hi, LANES - N_EXPERTS, 1) + p_lo
    logits = lg.T[0:N_EXPERTS, :] + br_ref[...]
    e_iota = lax.broadcasted_iota(jnp.int32, (N_EXPERTS, tm), 0)
    vals, idxs = [], []
    for _ in range(TOP_K):
        m = jnp.max(logits, axis=0, keepdims=True)
        idx = jnp.min(jnp.where(logits == m, e_iota, N_EXPERTS), axis=0, keepdims=True)
        vals.append(m)
        idxs.append(idx)
        logits = jnp.where(e_iota == idx, -jnp.inf, logits)
    ex = [jnp.exp(v - vals[0]) for v in vals]
    tot = ex[0] + ex[1] + ex[2] + ex[3]
    idx_ref[...] = jnp.concatenate(idxs, axis=0)
    wgt_ref[...] = jnp.concatenate([e / tot for e in ex], axis=0)


def _mixout(x2, attn, z, sa, sc, wa_bf, wc_bf, wo_bf, g_ffn, wr_t, b_router):
    T = x2.shape[0]
    tm = TM_MIX
    const = lambda i: (0, 0)
    row_blk = lambda i: (i, 0)
    col_blk = lambda i: (0, i)
    act = pl.BlockSpec((tm, D_MODEL), row_blk)
    wsq = pl.BlockSpec((D_MODEL, D_MODEL), const)
    return pl.pallas_call(
        _mixout_body,
        grid=(T // tm,),
        in_specs=[act, act, act, act, act, wsq, wsq, wsq,
                  pl.BlockSpec((1, D_MODEL), const),
                  pl.BlockSpec((D_MODEL, LANES), const),
                  pl.BlockSpec((N_EXPERTS, 1), const)],
        out_specs=[act,
                   pl.BlockSpec((tm * SLABS, LANES), row_blk),
                   pl.BlockSpec((TOP_K, tm), col_blk),
                   pl.BlockSpec((TOP_K, tm), col_blk)],
        out_shape=[jax.ShapeDtypeStruct((T, D_MODEL), F32),
                   jax.ShapeDtypeStruct((T * SLABS, LANES), BF16),
                   jax.ShapeDtypeStruct((TOP_K, T), jnp.int32),
                   jax.ShapeDtypeStruct((TOP_K, T), F32)],
        scratch_shapes=[pltpu.VMEM((tm * SLABS, LANES), F32)],
        compiler_params=pltpu.CompilerParams(
            dimension_semantics=("arbitrary",), vmem_limit_bytes=VMEM_LIMIT_BYTES),
        name="mixout",
    )(x2, attn, z, sa, sc, wa_bf, wc_bf, wo_bf, g_ffn, wr_t, b_router)


def _rank_body(idx_ref, rank_ref, cnt_ref, carry_ref):
    tc = idx_ref.shape[1]
    i = pl.program_id(0)

    @pl.when(i == 0)
    def _():
        carry_ref[...] = jnp.zeros_like(carry_ref)

    idx = idx_ref[...]
    e_iota = lax.broadcasted_iota(jnp.int32, (N_EXPERTS, tc), 0)
    sel = [e_iota == idx[k:k + 1, :] for k in range(TOP_K)]
    member = (sel[0] | sel[1] | sel[2] | sel[3])
    onehot = jnp.where(member, 1.0, 0.0).astype(BF16)
    r = lax.broadcasted_iota(jnp.int32, (tc, tc), 0)
    c = lax.broadcasted_iota(jnp.int32, (tc, tc), 1)
    before = jnp.where(r < c, 1.0, 0.0).astype(BF16)
    carry = carry_ref[:, 0:1]
    prefix = jnp.dot(onehot, before, preferred_element_type=F32) + carry
    ranks = [jnp.sum(jnp.where(sel[k], prefix, 0.0), axis=0, keepdims=True)
             for k in range(TOP_K)]
    rank_ref[...] = jnp.concatenate(ranks, axis=0).astype(jnp.int32)
    total = carry + jnp.sum(onehot.astype(F32), axis=1, keepdims=True)
    carry_ref[...] = jnp.broadcast_to(total, carry_ref.shape)
    cnt_ref[...] = jnp.broadcast_to(total, cnt_ref.shape).astype(jnp.int32)


def _rank(idx_t):
    T = idx_t.shape[1]
    tc = TC_RANK
    return pl.pallas_call(
        _rank_body,
        grid=(T // tc,),
        in_specs=[pl.BlockSpec((TOP_K, tc), lambda i: (0, i))],
        out_specs=[pl.BlockSpec((TOP_K, tc), lambda i: (0, i)),
                   pl.BlockSpec((N_EXPERTS, LANES), lambda i: (0, 0))],
        out_shape=[jax.ShapeDtypeStruct((TOP_K, T), jnp.int32),
                   jax.ShapeDtypeStruct((N_EXPERTS, LANES), jnp.int32)],
        scratch_shapes=[pltpu.VMEM((N_EXPERTS, LANES), F32)],
        compiler_params=pltpu.CompilerParams(dimension_semantics=("arbitrary",)),
        name="rank",
    )(idx_t)


def _invperm_body(dest_ref, fill_ref, end_ref, src_ref, *, n_assign, n_slots):
    def mark_tile(start):
        def mark(q, carry):
            p = start + q
            src_ref[p] = n_assign + (p & (2 * TM_EXPERT - 1))
            return carry
        lax.fori_loop(0, TM_EXPERT, mark, 0, unroll=16)

    def mark_expert_tail(e, carry):
        mark_tile(fill_ref[e])
        return carry

    lax.fori_loop(0, N_EXPERTS, mark_expert_tail, 0)

    def mark_unused(t, carry):
        mark_tile(t * TM_EXPERT)
        return carry

    lax.fori_loop(end_ref[N_EXPERTS - 1] // TM_EXPERT, n_slots // TM_EXPERT + 1, mark_unused, 0)

    def place(a, carry):
        src_ref[dest_ref[a]] = a
        return carry

    lax.fori_loop(0, n_assign, place, 0, unroll=16)


def _invperm(dest_flat, fill_start, pad_end, n_slots):
    return pl.pallas_call(
        functools.partial(_invperm_body, n_assign=dest_flat.shape[0], n_slots=n_slots),
        grid_spec=pltpu.PrefetchScalarGridSpec(
            num_scalar_prefetch=3,
            grid=(1,),
            in_specs=[],
            out_specs=pl.BlockSpec(memory_space=pltpu.SMEM),
        ),
        out_shape=jax.ShapeDtypeStruct((n_slots + TM_EXPERT,), jnp.int32),
        compiler_params=pltpu.CompilerParams(dimension_semantics=("arbitrary",)),
        name="invperm",
    )(dest_flat, fill_start, pad_end)


def _expert_body(blk_e_ref, nxt_e_ref, n_used_ref, src_ref, hp_ref, wg_hbm, wu_hbm, wd_hbm,
                 bg_ref, bu_ref, bd_ref, y_ref, stage_bf, stage, xbuf, ybuf, wst, w_bf, wsem, ssem, *,
                 n_tok):
    tm = TM_EXPERT
    i = pl.program_id(0)
    n_used = n_used_ref[0]
    slot = i % 2
    e = blk_e_ref[i]
    e_prev = blk_e_ref[jnp.maximum(i - 1, 0)]

    def weight_copies(ex):
        return [pltpu.make_async_copy(w.at[ex], wst.at[m], wsem.at[m])
                for m, w in enumerate((wg_hbm, wu_hbm, wd_hbm))]

    def rows_of(buf):
        return pl.ds(pl.multiple_of(buf * tm, tm), tm)

    def gather_rows(tile, rows):
        for r in rows:
            a = src_ref[tile * tm + r]
            stage_bf[pl.ds(r * SLABS, SLABS), :] = hp_ref[a & (n_tok - 1)]

    def finish_gather(buf):
        stage[...] = stage_bf[...].astype(F32)
        xbuf[rows_of(buf), :] = _load_token_major(stage, tm).astype(BF16)

    def scatter_copy(tile, buf, r, to_spare):
        a = src_ref[tile * tm + r]
        if to_spare is not None:
            a = jnp.where(to_spare, TOP_K * n_tok + buf * tm + r, a)
        return pltpu.make_async_copy(ybuf.at[buf, pl.ds(r * SLABS, SLABS)], y_ref.at[a],
                                     ssem.at[buf])

    @pl.when(i == 0)
    def _():
        for c in weight_copies(e):
            c.start()
        gather_rows(0, range(tm))
        finish_gather(0)
        ybuf[...] = jnp.zeros_like(ybuf)

    @pl.when(i < n_used)
    def _():
        @pl.when(i >= 1)
        def _():
            for r in range(tm):
                scatter_copy(jnp.maximum(i - 2, 0), slot, r, i == 1).wait()

        @pl.when((i == 0) | (e != e_prev))
        def _():
            for c in weight_copies(e):
                c.wait()
            rows = D_MODEL // SUBLANES
            for m in range(3):
                def cast_rows(j, carry, m=m):
                    sl = pl.ds(pl.multiple_of(j * rows, rows), rows)
                    w_bf[m, sl, :] = wst[m, sl, :].astype(BF16)
                    return carry
                lax.fori_loop(0, SUBLANES, cast_rows, 0)
            nxt_e = nxt_e_ref[i]

            @pl.when(nxt_e >= 0)
            def _():
                for c in weight_copies(nxt_e):
                    c.start()

        x = xbuf[rows_of(slot), :]
        nxt = jnp.minimum(i + 1, n_used - 1)

        def move_rows(groups):
            for grp in groups:
                rows = range(grp * MOVE_GROUP, (grp + 1) * MOVE_GROUP)
                for r in rows:
                    scatter_copy(jnp.maximum(i - 1, 0), 1 - slot, r, i == 0).start()
                gather_rows(nxt, rows)

        n_groups = tm // MOVE_GROUP
        move_rows(range(0, 3 * n_groups // 8))
        g = jnp.dot(x, w_bf[0], preferred_element_type=F32) + bg_ref[pl.ds(e, 1), :]
        move_rows(range(3 * n_groups // 8, 6 * n_groups // 8))
        u = jnp.dot(x, w_bf[1], preferred_element_type=F32) + bu_ref[pl.ds(e, 1), :]
        move_rows(range(6 * n_groups // 8, n_groups))
        finish_gather(1 - slot)
        g = jnp.minimum(g, SWIGLU_LIMIT)
        u = jnp.clip(u, -SWIGLU_LIMIT, SWIGLU_LIMIT)
        a = g * jax.nn.sigmoid(SWIGLU_ALPHA * g) * (u + 1.0)
        y = jnp.dot(a.astype(BF16), w_bf[2], preferred_element_type=F32) + bd_ref[pl.ds(e, 1), :]
        _store_token_major(ybuf.at[slot], y)

        @pl.when(i == n_used - 1)
        def _():
            for r in range(tm):
                scatter_copy(i, slot, r, None).start()
            for r in range(tm):
                scatter_copy(jnp.maximum(i - 1, 0), 1 - slot, r, i == 0).wait()
            for r in range(tm):
                scatter_copy(i, slot, r, None).wait()


def _experts(blk_e, nxt_e, n_used, src, hp, w_gate, b_gate, w_up, b_up, w_down, b_down, n_tiles):
    tm = TM_EXPERT
    d_ff = w_gate.shape[2]
    assert d_ff == D_MODEL
    n_tok = hp.shape[0]
    assert n_tok & (n_tok - 1) == 0, "assignment ids are split with a power-of-two mask"
    assert n_tiles >= 2
    const2 = lambda i, *_: (0, 0)
    bias = pl.BlockSpec((N_EXPERTS, D_MODEL), const2)
    hbm = pl.BlockSpec(memory_space=pl.ANY)
    return pl.pallas_call(
        functools.partial(_expert_body, n_tok=n_tok),
        grid_spec=pltpu.PrefetchScalarGridSpec(
            num_scalar_prefetch=4,
            grid=(n_tiles,),
            in_specs=[
                pl.BlockSpec((n_tok, SLABS, LANES), lambda i, *_: (0, 0, 0),
                             pipeline_mode=pl.Buffered(1)),
                hbm, hbm, hbm, bias, bias, bias,
            ],
            out_specs=pl.BlockSpec(memory_space=pl.ANY),
            scratch_shapes=[pltpu.VMEM((tm * SLABS, LANES), BF16),
                            pltpu.VMEM((tm * SLABS, LANES), F32),
                            pltpu.VMEM((2 * tm, D_MODEL), BF16),
                            pltpu.VMEM((2, tm * SLABS, LANES), F32),
                            pltpu.VMEM((3, D_MODEL, D_MODEL), F32),
                            pltpu.VMEM((3, D_MODEL, D_MODEL), BF16),
                            pltpu.SemaphoreType.DMA((3,)),
                            pltpu.SemaphoreType.DMA((2,))],
        ),
        out_shape=jax.ShapeDtypeStruct((TOP_K * n_tok + 2 * tm, SLABS, LANES), F32),
        compiler_params=pltpu.CompilerParams(
            dimension_semantics=("arbitrary",), vmem_limit_bytes=EXPERT_VMEM_LIMIT_BYTES),
        name="experts",
    )(blk_e, nxt_e, n_used, src, hp, w_gate, w_up, w_down, b_gate, b_up, b_down)


def _combine_body(y0_ref, y1_ref, y2_ref, y3_ref, x1_ref, wgt_ref, g_ref, o_ref):
    tm = x1_ref.shape[0]
    wgt = wgt_ref[...]
    acc = x1_ref[...]
    for k, yk_ref in enumerate((y0_ref, y1_ref, y2_ref, y3_ref)):
        acc = acc + wgt[:, k:k + 1] * _load_token_major(yk_ref, tm)
    o_ref[...] = _rms_scale(acc, g_ref[...])


def _combine(y_slabs, x1, wgt_rows, g_final):
    T = x1.shape[0]
    tm = TM_MOVE
    steps = T // tm
    row_blk = lambda i: (i, 0)
    y_specs = [pl.BlockSpec((tm * SLABS, LANES), functools.partial(lambda i, k: (k * steps + i, 0), k=k))
               for k in range(TOP_K)]
    return pl.pallas_call(
        _combine_body,
        grid=(steps,),
        in_specs=y_specs + [
            pl.BlockSpec((tm, D_MODEL), row_blk),
            pl.BlockSpec((tm, TOP_K), row_blk),
            pl.BlockSpec((1, D_MODEL), lambda i: (0, 0)),
        ],
        out_specs=pl.BlockSpec((tm, D_MODEL), row_blk),
        out_shape=jax.ShapeDtypeStruct((T, D_MODEL), F32),
        compiler_params=pltpu.CompilerParams(
            dimension_semantics=("arbitrary",), vmem_limit_bytes=VMEM_LIMIT_BYTES),
        name="combine",
    )(y_slabs, y_slabs, y_slabs, y_slabs, x1, wgt_rows, g_final)


def _rope_tables(seq_len):
    half = HEAD_DIM // 2
    inv_freq = ROPE_THETA ** (-jnp.arange(half, dtype=F32) / half)
    ang = jnp.arange(seq_len, dtype=jnp.int32).astype(F32)[:, None] * inv_freq[None, :]
    cos = jnp.cos(ang)
    sin = jnp.sin(ang)
    reps = LANES // HEAD_DIM
    cos_t = jnp.tile(jnp.concatenate([cos, cos], axis=-1), (1, reps))
    sin_t = jnp.tile(jnp.concatenate([-sin, sin], axis=-1), (1, reps))
    return cos_t, sin_t


def _router_hi_lo(w_router):
    hi = w_router.astype(BF16)
    lo = (w_router - hi.astype(F32)).astype(BF16)
    pad = jnp.zeros((w_router.shape[0], LANES - 2 * N_EXPERTS), BF16)
    return jnp.concatenate([hi, lo, pad], axis=1)


def _layer(x2, batch, seq_len, g_mix, w_in, b_in, sinks, w_conv, w_attn_o, w_conv_o, w_out,
           g_ffn, w_router, b_router, w_gate, b_gate, w_up, b_up, w_down, b_down, g_out):
    T = x2.shape[0]
    cos_t, sin_t = _rope_tables(seq_len)
    q, k, v, z, sa, sc = _inproj(x2, g_mix[None, :], w_in.astype(BF16), b_in[None, :],
                                 cos_t, sin_t, w_conv, seq_len)
    attn = _attention(q, k, v, sinks, batch, seq_len)
    x1, hp, idx_t, wgt_t = _mixout(
        x2, attn, z, sa, sc, w_attn_o.astype(BF16), w_conv_o.astype(BF16), w_out.astype(BF16),
        g_ffn[None, :], _router_hi_lo(w_router), b_router[:, None])

    rank_t, cnt = _rank(idx_t)
    counts = cnt[:, 0]
    padded = (counts + TM_EXPERT - 1) // TM_EXPERT * TM_EXPERT
    pad_end = jnp.cumsum(padded)
    pad_start = pad_end - padded
    n_tiles = (T * TOP_K) // TM_EXPERT + N_EXPERTS
    tile_row = jnp.arange(n_tiles, dtype=jnp.int32) * TM_EXPERT
    blk_e = jnp.minimum(jnp.sum(pad_end[None, :] <= tile_row[:, None], axis=1),
                        N_EXPERTS - 1).astype(jnp.int32)
    n_used = (pad_end[-1:] // TM_EXPERT).astype(jnp.int32)
    e_ids = jnp.arange(N_EXPERTS, dtype=jnp.int32)
    dest = rank_t + jnp.sum(
        jnp.where(idx_t[:, :, None] == e_ids[None, None, :], pad_start[None, None, :], 0), axis=-1)
    dest_flat = dest.reshape(-1).astype(jnp.int32)
    fill_start = (pad_start + counts).astype(jnp.int32)

    next_tile = pad_end[blk_e] // TM_EXPERT
    nxt_e = jnp.where(next_tile < n_used[0],
                      blk_e[jnp.minimum(next_tile, n_tiles - 1)], -1).astype(jnp.int32)

    src = _invperm(dest_flat, fill_start, pad_end.astype(jnp.int32), n_tiles * TM_EXPERT)
    y_slabs = _experts(blk_e, nxt_e, n_used, src, hp.reshape(T, SLABS, LANES),
                       w_gate, b_gate, w_up, b_up,
                       w_down, b_down, n_tiles)
    return _combine(y_slabs.reshape(-1, LANES), x1, wgt_t.T, g_out[None, :])


def kernel(x, g_mix, w_in, b_in, sinks, w_conv, w_attn_o, w_conv_o, w_out, g_ffn, w_router,
           b_router, w_gate, b_gate, w_up, b_up, w_down, b_down, g_final):
    batch, seq_len, d = x.shape
    depth = g_mix.shape[0]
    assert depth == 1, "the final norm is fused into the single layer's combine step"
    x2 = x.reshape(batch * seq_len, d)
    out = _layer(x2, batch, seq_len, g_mix[0], w_in[0], b_in[0], sinks[0], w_conv[0],
                 w_attn_o[0], w_conv_o[0], w_out[0], g_ffn[0], w_router[0], b_router[0],
                 w_gate[0], b_gate[0], w_up[0], b_up[0], w_down[0], b_down[0], g_final)
    return out.reshape(batch, seq_len, d)
```

```python
import functools

import jax
import jax.numpy as jnp
from jax import lax
from jax.experimental import pallas as pl
from jax.experimental.pallas import tpu as pltpu

D_MODEL = 1024
HEAD_DIM = 64
N_Q_HEADS = 16
N_KV_HEADS = 4
Q_PER_KV = N_Q_HEADS // N_KV_HEADS
ATTN_WIDTH = N_Q_HEADS * HEAD_DIM
KV_WIDTH = N_KV_HEADS * HEAD_DIM
ATTN_BLOCK = 128
WINDOW = 128
ROPE_THETA = 10000.0
CONV_WIDTH = D_MODEL
CONV_KERNEL = 3
N_EXPERTS = 32
TOP_K = 4
SWIGLU_LIMIT = 7.0
SWIGLU_ALPHA = 1.702
RMS_EPS = 1e-5

OFF_Q = 0
OFF_K = OFF_Q + ATTN_WIDTH
OFF_V = OFF_K + KV_WIDTH
OFF_CB = OFF_V + KV_WIDTH
OFF_CC = OFF_CB + CONV_WIDTH
OFF_CX = OFF_CC + CONV_WIDTH
OFF_GA = OFF_CX + CONV_WIDTH
OFF_GC = OFF_GA + D_MODEL
IN_WIDTH = OFF_GC + D_MODEL

LANES = 128
SUBLANES = 8
VMEM_LIMIT_BYTES = 56 * 1024 * 1024
EXPERT_VMEM_LIMIT_BYTES = 60 * 1024 * 1024

TM_INPROJ = 512
COL_CHUNK = 512
Q_BLOCKS_PER_STEP = 8
TM_MIX = 512
TC_RANK = 512
TM_EXPERT = 256
TM_MOVE = 512
MOVE_GROUP = 32

BF16 = jnp.bfloat16
F32 = jnp.float32
NEG_BIG = -1e30


def _rms_scale(x, g):
    ms = jnp.mean(x * x, axis=-1, keepdims=True)
    return (x * lax.rsqrt(ms + RMS_EPS)) * g


SLABS = D_MODEL // LANES


def _store_token_major(ref, val):
    tm = val.shape[0]
    for s in range(SLABS):
        ref[pl.ds(s, tm, stride=SLABS), :] = val[:, s * LANES:(s + 1) * LANES]


def _load_token_major(ref, tm):
    return jnp.concatenate(
        [ref[pl.ds(s, tm, stride=SLABS), :] for s in range(SLABS)], axis=1)


def _inproj_body(x_ref, g_ref, w_ref, b_ref, cos_ref, sin_ref, wc_ref,
                 q_ref, k_ref, v_ref, z_ref, sa_ref, sc_ref, carry_ref, *, tiles_per_seq):
    tm = x_ref.shape[0]
    i = pl.program_id(0)
    h = _rms_scale(x_ref[...], g_ref[...]).astype(BF16)

    def proj(c0, width):
        return (jnp.dot(h, w_ref[:, c0:c0 + width], preferred_element_type=F32)
                + b_ref[:, c0:c0 + width])

    cos = cos_ref[...]
    sin = sin_ref[...]
    lane = lax.broadcasted_iota(jnp.int32, (tm, LANES), 1)
    first_half = (lane & (HEAD_DIM // 2)) == 0

    def rope(t):
        partner = jnp.where(first_half,
                            pltpu.roll(t, LANES - HEAD_DIM // 2, 1),
                            pltpu.roll(t, HEAD_DIM // 2, 1))
        return t * cos + partner * sin

    for c in range(0, ATTN_WIDTH, COL_CHUNK):
        acc = proj(OFF_Q + c, COL_CHUNK)
        for j in range(0, COL_CHUNK, LANES):
            q_ref[:, c + j:c + j + LANES] = (
                rope(acc[:, j:j + LANES]) * (HEAD_DIM ** -0.5)).astype(BF16)

    acc = proj(OFF_K, 2 * KV_WIDTH)
    for j in range(0, KV_WIDTH, LANES):
        k_ref[:, j:j + LANES] = rope(acc[:, j:j + LANES]).astype(BF16)
    v_ref[...] = acc[:, KV_WIDTH:].astype(BF16)

    seq_start = (i % tiles_per_seq) == 0
    head = 2 * SUBLANES
    row = lax.broadcasted_iota(jnp.int32, (head, COL_CHUNK), 0)
    for c in range(0, CONV_WIDTH, COL_CHUNK):
        u = proj(OFF_CC + c, COL_CHUNK) * proj(OFF_CX + c, COL_CHUNK)
        cb = proj(OFF_CB + c, COL_CHUNK)
        w0 = wc_ref[0:1, c:c + COL_CHUNK]
        w1 = wc_ref[1:2, c:c + COL_CHUNK]
        w2 = wc_ref[2:3, c:c + COL_CHUNK]
        y = w0 * pltpu.roll(u, 2, 0) + w1 * pltpu.roll(u, 1, 0) + w2 * u
        z_ref[:, c:c + COL_CHUNK] = (cb * y).astype(BF16)
        prev = jnp.where(seq_start, 0.0, carry_ref[:, c:c + COL_CHUNK])
        pad = jnp.zeros((SUBLANES, COL_CHUNK), F32)
        uh = u[0:head]
        u1 = jnp.where(row < 1, jnp.concatenate([pltpu.roll(prev, 1, 0), pad], 0),
                       pltpu.roll(uh, 1, 0))
        u2 = jnp.where(row < 2, jnp.concatenate([pltpu.roll(prev, 2, 0), pad], 0),
                       pltpu.roll(uh, 2, 0))
        yh = w0 * u2 + w1 * u1 + w2 * uh
        z_ref[0:head, c:c + COL_CHUNK] = (cb[0:head] * yh).astype(BF16)
        carry_ref[:, c:c + COL_CHUNK] = u[tm - SUBLANES:tm]

    for c in range(0, D_MODEL, COL_CHUNK):
        sa_ref[:, c:c + COL_CHUNK] = jax.nn.sigmoid(proj(OFF_GA + c, COL_CHUNK)).astype(BF16)
        sc_ref[:, c:c + COL_CHUNK] = jax.nn.sigmoid(proj(OFF_GC + c, COL_CHUNK)).astype(BF16)


def _inproj(x2, g_mix, w_in_bf, b_in, cos_t, sin_t, w_conv, seq_len):
    T = x2.shape[0]
    tm = TM_INPROJ
    tiles_per_seq = seq_len // tm
    const = lambda i: (0, 0)
    row_blk = lambda i: (i, 0)
    pos_blk = lambda i: (i % tiles_per_seq, 0)
    return pl.pallas_call(
        functools.partial(_inproj_body, tiles_per_seq=tiles_per_seq),
        grid=(T // tm,),
        in_specs=[
            pl.BlockSpec((tm, D_MODEL), row_blk),
            pl.BlockSpec((1, D_MODEL), const),
            pl.BlockSpec((D_MODEL, IN_WIDTH), const, pipeline_mode=pl.Buffered(1)),
            pl.BlockSpec((1, IN_WIDTH), const),
            pl.BlockSpec((tm, LANES), pos_blk),
            pl.BlockSpec((tm, LANES), pos_blk),
            pl.BlockSpec((CONV_KERNEL, CONV_WIDTH), const),
        ],
        out_specs=[
            pl.BlockSpec((tm, ATTN_WIDTH), row_blk),
            pl.BlockSpec((tm, KV_WIDTH), row_blk),
            pl.BlockSpec((tm, KV_WIDTH), row_blk),
            pl.BlockSpec((tm, CONV_WIDTH), row_blk),
            pl.BlockSpec((tm, D_MODEL), row_blk),
            pl.BlockSpec((tm, D_MODEL), row_blk),
        ],
        out_shape=[
            jax.ShapeDtypeStruct((T, ATTN_WIDTH), BF16),
            jax.ShapeDtypeStruct((T, KV_WIDTH), BF16),
            jax.ShapeDtypeStruct((T, KV_WIDTH), BF16),
            jax.ShapeDtypeStruct((T, CONV_WIDTH), BF16),
            jax.ShapeDtypeStruct((T, D_MODEL), BF16),
            jax.ShapeDtypeStruct((T, D_MODEL), BF16),
        ],
        scratch_shapes=[pltpu.VMEM((SUBLANES, CONV_WIDTH), F32)],
        compiler_params=pltpu.CompilerParams(
            dimension_semantics=("arbitrary",), vmem_limit_bytes=VMEM_LIMIT_BYTES),
        name="inproj",
    )(x2, g_mix, w_in_bf, b_in, cos_t, sin_t, w_conv)


def _attn_body(sinks_ref, q_ref, kp_ref, kc_ref, vp_ref, vc_ref, o_ref):
    n = pl.program_id(1)
    blk = ATTN_BLOCK
    cols = Q_PER_KV * blk
    j = lax.broadcasted_iota(jnp.int32, (blk, cols), 0)
    qi = lax.broadcasted_iota(jnp.int32, (blk, cols), 1) % blk
    tri = j > qi
    no_prev = tri & (n == 0)
    k3 = jnp.concatenate([kp_ref[...], kc_ref[...]], axis=0)
    v3 = jnp.concatenate([vp_ref[...], vc_ref[...]], axis=0)
    v3_t = v3.astype(F32).T.astype(BF16)
    for qb in range(Q_BLOCKS_PER_STEP):
        k_all = k3[qb * blk:(qb + 2) * blk]
        v_t = v3_t[:, qb * blk:(qb + 2) * blk]
        for kh in range(N_KV_HEADS):
            k_h = k_all[:, kh * HEAD_DIM:(kh + 1) * HEAD_DIM]
            vt_h = v_t[kh * HEAD_DIM:(kh + 1) * HEAD_DIM, :]
            heads = [kh * Q_PER_KV + g for g in range(Q_PER_KV)]
            q_g = jnp.concatenate(
                [q_ref[qb * blk:(qb + 1) * blk, hq * HEAD_DIM:(hq + 1) * HEAD_DIM]
                 for hq in heads], axis=0)
            sink = jnp.concatenate(
                [jnp.full((1, blk), sinks_ref[hq], F32) for hq in heads], axis=1)
            s = lax.dot_general(k_h, q_g, (((1,), (1,)), ((), ())), preferred_element_type=F32)
            fold = jnp.where(tri, s[:blk], s[blk:])
            if qb == 0:
                fold = jnp.where(no_prev, NEG_BIG, fold)
            m = jnp.maximum(jnp.max(fold, axis=0, keepdims=True), sink)
            p = jnp.exp(fold - m)
            denom = jnp.sum(p, axis=0, keepdims=True) + jnp.exp(sink - m)
            p2 = jnp.concatenate([jnp.where(tri, p, 0.0), jnp.where(tri, 0.0, p)], axis=0)
            o_t = jnp.dot(vt_h, p2.astype(BF16), preferred_element_type=F32) / denom
            for g in range(0, Q_PER_KV, 2):
                pair = jnp.concatenate(
                    [o_t[:, g * blk:(g + 1) * blk], o_t[:, (g + 1) * blk:(g + 2) * blk]], axis=0)
                c0 = heads[g] * HEAD_DIM
                o_ref[qb * blk:(qb + 1) * blk, c0:c0 + 2 * HEAD_DIM] = pair.T.astype(BF16)


def _attention(q, k, v, sinks, batch, seq_len):
    T = q.shape[0]
    assert WINDOW == ATTN_BLOCK, "the folded score tile needs window == block"
    rows = Q_BLOCKS_PER_STEP * ATTN_BLOCK
    steps = seq_len // rows
    cur = lambda b, n: (b * steps + n, 0)
    prev = lambda b, n: ((b * steps + n) * Q_BLOCKS_PER_STEP - jnp.minimum(n, 1), 0)
    return pl.pallas_call(
        _attn_body,
        grid=(batch, steps),
        in_specs=[
            pl.BlockSpec(memory_space=pltpu.SMEM),
            pl.BlockSpec((rows, ATTN_WIDTH), cur),
            pl.BlockSpec((ATTN_BLOCK, KV_WIDTH), prev),
            pl.BlockSpec((rows, KV_WIDTH), cur),
            pl.BlockSpec((ATTN_BLOCK, KV_WIDTH), prev),
            pl.BlockSpec((rows, KV_WIDTH), cur),
        ],
        out_specs=pl.BlockSpec((rows, ATTN_WIDTH), cur),
        out_shape=jax.ShapeDtypeStruct((T, ATTN_WIDTH), BF16),
        compiler_params=pltpu.CompilerParams(
            dimension_semantics=("arbitrary", "arbitrary"), vmem_limit_bytes=VMEM_LIMIT_BYTES),
        name="attn",
    )(sinks, q, k, k, v, v)


def _mixout_body(x_ref, a_ref, z_ref, sa_ref, sc_ref, wa_ref, wc_ref, wo_ref, g_ref,
                 wr_ref, br_ref, x1_ref, hp_ref, idx_ref, wgt_ref, slab_ref):
    tm = x_ref.shape[0]
    y_attn = jnp.dot(a_ref[...], wa_ref[...], preferred_element_type=F32)
    y_conv = jnp.dot(z_ref[...], wc_ref[...], preferred_element_type=F32)
    merged = sa_ref[...].astype(F32) * y_attn + sc_ref[...].astype(F32) * y_conv
    x1 = x_ref[...] + jnp.dot(merged.astype(BF16), wo_ref[...], preferred_element_type=F32)
    x1_ref[...] = x1
    h = _rms_scale(x1, g_ref[...])
    _store_token_major(slab_ref, h)
    hp_ref[...] = slab_ref[...].astype(BF16)
    h_hi = h.astype(BF16)
    h_lo = (h - h_hi.astype(F32)).astype(BF16)
    p_hi = jnp.dot(h_hi, wr_ref[...], preferred_element_type=F32)
    p_lo = jnp.dot(h_lo, wr_ref[...], preferred_element_type=F32)
    lg = p_hi + pltpu.roll(p_hi, LANES - N_EXPERTS, 1) + p_lo
    logits = lg.T[0:N_EXPERTS, :] + br_ref[...]
    e_iota = lax.broadcasted_iota(jnp.int32, (N_EXPERTS, tm), 0)
    vals, idxs = [], []
    for _ in range(TOP_K):
        m = jnp.max(logits, axis=0, keepdims=True)
        idx = jnp.min(jnp.where(logits == m, e_iota, N_EXPERTS), axis=0, keepdims=True)
        vals.append(m)
        idxs.append(idx)
        logits = jnp.where(e_iota == idx, -jnp.inf, logits)
    ex = [jnp.exp(v - vals[0]) for v in vals]
    tot = ex[0] + ex[1] + ex[2] + ex[3]
    idx_ref[...] = jnp.concatenate(idxs, axis=0)
    wgt_ref[...] = jnp.concatenate([e / tot for e in ex], axis=0)


def _mixout(x2, attn, z, sa, sc, wa_bf, wc_bf, wo_bf, g_ffn, wr_t, b_router):
    T = x2.shape[0]
    tm = TM_MIX
    const = lambda i: (0, 0)
    row_blk = lambda i: (i, 0)
    col_blk = lambda i: (0, i)
    act = pl.BlockSpec((tm, D_MODEL), row_blk)
    wsq = pl.BlockSpec((D_MODEL, D_MODEL), const)
    return pl.pallas_call(
        _mixout_body,
        grid=(T // tm,),
        in_specs=[act, act, act, act, act, wsq, wsq, wsq,
                  pl.BlockSpec((1, D_MODEL), const),
                  pl.BlockSpec((D_MODEL, LANES), const),
                  pl.BlockSpec((N_EXPERTS, 1), const)],
        out_specs=[act,
                   pl.BlockSpec((tm * SLABS, LANES), row_blk),
                   pl.BlockSpec((TOP_K, tm), col_blk),
                   pl.BlockSpec((TOP_K, tm), col_blk)],
        out_shape=[jax.ShapeDtypeStruct((T, D_MODEL), F32),
                   jax.ShapeDtypeStruct((T * SLABS, LANES), BF16),
                   jax.ShapeDtypeStruct((TOP_K, T), jnp.int32),
                   jax.ShapeDtypeStruct((TOP_K, T), F32)],
        scratch_shapes=[pltpu.VMEM((tm * SLABS, LANES), F32)],
        compiler_params=pltpu.CompilerParams(
            dimension_semantics=("arbitrary",), vmem_limit_bytes=VMEM_LIMIT_BYTES),
        name="mixout",
    )(x2, attn, z, sa, sc, wa_bf, wc_bf, wo_bf, g_ffn, wr_t, b_router)


def _rank_body(idx_ref, rank_ref, cnt_ref, carry_ref):
    tc = idx_ref.shape[1]
    i = pl.program_id(0)

    @pl.when(i == 0)
    def _():
        carry_ref[...] = jnp.zeros_like(carry_ref)

    idx = idx_ref[...]
    e_iota = lax.broadcasted_iota(jnp.int32, (N_EXPERTS, tc), 0)
    sel = [e_iota == idx[k:k + 1, :] for k in range(TOP_K)]
    member = (sel[0] | sel[1] | sel[2] | sel[3])
    onehot = jnp.where(member, 1.0, 0.0).astype(BF16)
    r = lax.broadcasted_iota(jnp.int32, (tc, tc), 0)
    c = lax.broadcasted_iota(jnp.int32, (tc, tc), 1)
    before = jnp.where(r < c, 1.0, 0.0).astype(BF16)
    carry = carry_ref[:, 0:1]
    prefix = jnp.dot(onehot, before, preferred_element_type=F32) + carry
    ranks = [jnp.sum(jnp.where(sel[k], prefix, 0.0), axis=0, keepdims=True)
             for k in range(TOP_K)]
    rank_ref[...] = jnp.concatenate(ranks, axis=0).astype(jnp.int32)
    total = carry + jnp.sum(onehot.astype(F32), axis=1, keepdims=True)
    carry_ref[...] = jnp.broadcast_to(total, carry_ref.shape)
    cnt_ref[...] = jnp.broadcast_to(total, cnt_ref.shape).astype(jnp.int32)


def _rank(idx_t):
    T = idx_t.shape[1]
    tc = TC_RANK
    return pl.pallas_call(
        _rank_body,
        grid=(T // tc,),
        in_specs=[pl.BlockSpec((TOP_K, tc), lambda i: (0, i))],
        out_specs=[pl.BlockSpec((TOP_K, tc), lambda i: (0, i)),
                   pl.BlockSpec((N_EXPERTS, LANES), lambda i: (0, 0))],
        out_shape=[jax.ShapeDtypeStruct((TOP_K, T), jnp.int32),
                   jax.ShapeDtypeStruct((N_EXPERTS, LANES), jnp.int32)],
        scratch_shapes=[pltpu.VMEM((N_EXPERTS, LANES), F32)],
        compiler_params=pltpu.CompilerParams(dimension_semantics=("arbitrary",)),
        name="rank",
    )(idx_t)


def _invperm_body(dest_ref, fill_ref, end_ref, src_ref, *, n_assign, n_slots):
    def mark_tile(start):
        def mark(q, carry):
            p = start + q
            src_ref[p] = n_assign + (p & (2 * TM_EXPERT - 1))
            return carry
        lax.fori_loop(0, TM_EXPERT, mark, 0, unroll=16)

    def mark_expert_tail(e, carry):
        mark_tile(fill_ref[e])
        return carry

    lax.fori_loop(0, N_EXPERTS, mark_expert_tail, 0)

    def mark_unused(t, carry):
        mark_tile(t * TM_EXPERT)
        return carry

    lax.fori_loop(end_ref[N_EXPERTS - 1] // TM_EXPERT, n_slots // TM_EXPERT + 1, mark_unused, 0)

    def place(a, carry):
        src_ref[dest_ref[a]] = a
        return carry

    lax.fori_loop(0, n_assign, place, 0, unroll=64)


def _invperm(dest_flat, fill_start, pad_end, n_slots):
    return pl.pallas_call(
        functools.partial(_invperm_body, n_assign=dest_flat.shape[0], n_slots=n_slots),
        grid_spec=pltpu.PrefetchScalarGridSpec(
            num_scalar_prefetch=3,
            grid=(1,),
            in_specs=[],
            out_specs=pl.BlockSpec(memory_space=pltpu.SMEM),
        ),
        out_shape=jax.ShapeDtypeStruct((n_slots + TM_EXPERT,), jnp.int32),
        compiler_params=pltpu.CompilerParams(dimension_semantics=("arbitrary",)),
        name="invperm",
    )(dest_flat, fill_start, pad_end)


def _expert_body(blk_e_ref, nxt_e_ref, n_used_ref, src_ref, hp_ref, wg_hbm, wu_hbm, wd_hbm,
                 bg_ref, bu_ref, bd_ref, y_ref, stage_bf, stage, xbuf, ybuf, wst, w_bf, wsem, ssem, *,
                 n_tok):
    tm = TM_EXPERT
    i = pl.program_id(0)
    n_used = n_used_ref[0]
    slot = i % 2
    e = blk_e_ref[i]
    e_prev = blk_e_ref[jnp.maximum(i - 1, 0)]

    def weight_copies(ex):
        return [pltpu.make_async_copy(w.at[ex], wst.at[m], wsem.at[m])
                for m, w in enumerate((wg_hbm, wu_hbm, wd_hbm))]

    def rows_of(buf):
        return pl.ds(pl.multiple_of(buf * tm, tm), tm)

    def gather_rows(tile, rows):
        for r in rows:
            a = src_ref[tile * tm + r]
            stage_bf[pl.ds(r * SLABS, SLABS), :] = hp_ref[a & (n_tok - 1)]

    def finish_gather(buf):
        stage[...] = stage_bf[...].astype(F32)
        xbuf[rows_of(buf), :] = _load_token_major(stage, tm).astype(BF16)

    def scatter_copy(tile, buf, r, to_spare):
        a = src_ref[tile * tm + r]
        if to_spare is not None:
            a = jnp.where(to_spare, TOP_K * n_tok + buf * tm + r, a)
        return pltpu.make_async_copy(ybuf.at[buf, pl.ds(r * SLABS, SLABS)], y_ref.at[a],
                                     ssem.at[buf])

    @pl.when(i == 0)
    def _():
        for c in weight_copies(e):
            c.start()
        gather_rows(0, range(tm))
        finish_gather(0)
        ybuf[...] = jnp.zeros_like(ybuf)
        spare = [pltpu.make_async_copy(ybuf.at[b, pl.ds(r * SLABS, SLABS)],
                                       y_ref.at[TOP_K * n_tok + b * tm + r], ssem.at[b])
                 for b in range(2) for r in range(tm)]
        for c in spare:
            c.start()
        for c in spare:
            c.wait()

    @pl.when(i < n_used)
    def _():
        @pl.when(i >= 1)
        def _():
            for r in range(tm):
                scatter_copy(jnp.maximum(i - 2, 0), slot, r, i == 1).wait()

        @pl.when((i == 0) | (e != e_prev))
        def _():
            for c in weight_copies(e):
                c.wait()
            rows = D_MODEL // SUBLANES
            for m in range(3):
                def cast_rows(j, carry, m=m):
                    sl = pl.ds(pl.multiple_of(j * rows, rows), rows)
                    w_bf[m, sl, :] = wst[m, sl, :].astype(BF16)
                    return carry
                lax.fori_loop(0, SUBLANES, cast_rows, 0)
            nxt_e = nxt_e_ref[i]

            @pl.when(nxt_e >= 0)
            def _():
                for c in weight_copies(nxt_e):
                    c.start()

        x = xbuf[rows_of(slot), :]
        nxt = jnp.minimum(i + 1, n_used - 1)

        def move_rows(groups):
            for grp in groups:
                rows = range(grp * MOVE_GROUP, (grp + 1) * MOVE_GROUP)
                for r in rows:
                    scatter_copy(jnp.maximum(i - 1, 0), 1 - slot, r, i == 0).start()
                gather_rows(nxt, rows)

        n_groups = tm // MOVE_GROUP
        move_rows(range(0, 3 * n_groups // 8))
        g = jnp.dot(x, w_bf[0], preferred_element_type=F32) + bg_ref[pl.ds(e, 1), :]
        move_rows(range(3 * n_groups // 8, 6 * n_groups // 8))
        u = jnp.dot(x, w_bf[1], preferred_element_type=F32) + bu_ref[pl.ds(e, 1), :]
        move_rows(range(6 * n_groups // 8, n_groups))
        finish_gather(1 - slot)
        g = jnp.minimum(g, SWIGLU_LIMIT)
        u = jnp.clip(u, -SWIGLU_LIMIT, SWIGLU_LIMIT)
        a = g * jax.nn.sigmoid(SWIGLU_ALPHA * g) * (u + 1.0)
        y = jnp.dot(a.astype(BF16), w_bf[2], preferred_element_type=F32) + bd_ref[pl.ds(e, 1), :]
        _store_token_major(ybuf.at[slot], y)

        @pl.when(i == n_used - 1)
        def _():
            for r in range(tm):
                scatter_copy(i, slot, r, None).start()
            for r in range(tm):
                scatter_copy(jnp.maximum(i - 1, 0), 1 - slot, r, i == 0).wait()
            for r in range(tm):
                scatter_copy(i, slot, r, None).wait()


def _experts(blk_e, nxt_e, n_used, src, hp, w_gate, b_gate, w_up, b_up, w_down, b_down, n_tiles):
    tm = TM_EXPERT
    d_ff = w_gate.shape[2]
    assert d_ff == D_MODEL
    n_tok = hp.shape[0]
    assert n_tok & (n_tok - 1) == 0, "assignment ids are split with a power-of-two mask"
    assert n_tiles >= 2
    const2 = lambda i, *_: (0, 0)
    bias = pl.BlockSpec((N_EXPERTS, D_MODEL), const2)
    hbm = pl.BlockSpec(memory_space=pl.ANY)
    return pl.pallas_call(
        functools.partial(_expert_body, n_tok=n_tok),
        grid_spec=pltpu.PrefetchScalarGridSpec(
            num_scalar_prefetch=4,
            grid=(n_tiles,),
            in_specs=[
                pl.BlockSpec((n_tok, SLABS, LANES), lambda i, *_: (0, 0, 0),
                             pipeline_mode=pl.Buffered(1)),
                hbm, hbm, hbm, bias, bias, bias,
            ],
            out_specs=pl.BlockSpec(memory_space=pl.ANY),
            scratch_shapes=[pltpu.VMEM((tm * SLABS, LANES), BF16),
                            pltpu.VMEM((tm * SLABS, LANES), F32),
                            pltpu.VMEM((2 * tm, D_MODEL), BF16),
                            pltpu.VMEM((2, tm * SLABS, LANES), F32),
                            pltpu.VMEM((3, D_MODEL, D_MODEL), F32),
                            pltpu.VMEM((3, D_MODEL, D_MODEL), BF16),
                            pltpu.SemaphoreType.DMA((3,)),
                            pltpu.SemaphoreType.DMA((2,))],
        ),
        out_shape=jax.ShapeDtypeStruct((TOP_K * n_tok + 2 * tm, SLABS, LANES), F32),
        compiler_params=pltpu.CompilerParams(
            dimension_semantics=("arbitrary",), vmem_limit_bytes=EXPERT_VMEM_LIMIT_BYTES),
        name="experts",
    )(blk_e, nxt_e, n_used, src, hp, w_gate, w_up, w_down, b_gate, b_up, b_down)


def _combine_body(y0_ref, y1_ref, y2_ref, y3_ref, x1_ref, wgt_ref, g_ref, o_ref):
    tm = x1_ref.shape[0]
    wgt = wgt_ref[...]
    acc = x1_ref[...]
    for k, yk_ref in enumerate((y0_ref, y1_ref, y2_ref, y3_ref)):
        acc = acc + wgt[:, k:k + 1] * _load_token_major(yk_ref, tm)
    o_ref[...] = _rms_scale(acc, g_ref[...])


def _combine(y_slabs, x1, wgt_rows, g_final):
    T = x1.shape[0]
    tm = TM_MOVE
    steps = T // tm
    row_blk = lambda i: (i, 0)
    y_specs = [pl.BlockSpec((tm * SLABS, LANES), functools.partial(lambda i, k: (k * steps + i, 0), k=k))
               for k in range(TOP_K)]
    return pl.pallas_call(
        _combine_body,
        grid=(steps,),
        in_specs=y_specs + [
            pl.BlockSpec((tm, D_MODEL), row_blk),
            pl.BlockSpec((tm, TOP_K), row_blk),
            pl.BlockSpec((1, D_MODEL), lambda i: (0, 0)),
        ],
        out_specs=pl.BlockSpec((tm, D_MODEL), row_blk),
        out_shape=jax.ShapeDtypeStruct((T, D_MODEL), F32),
        compiler_params=pltpu.CompilerParams(
            dimension_semantics=("arbitrary",), vmem_limit_bytes=VMEM_LIMIT_BYTES),
        name="combine",
    )(y_slabs, y_slabs, y_slabs, y_slabs, x1, wgt_rows, g_final)


def _rope_tables(seq_len):
    half = HEAD_DIM // 2
    inv_freq = ROPE_THETA ** (-jnp.arange(half, dtype=F32) / half)
    ang = jnp.arange(seq_len, dtype=jnp.int32).astype(F32)[:, None] * inv_freq[None, :]
    cos = jnp.cos(ang)
    sin = jnp.sin(ang)
    reps = LANES // HEAD_DIM
    cos_t = jnp.tile(jnp.concatenate([cos, cos], axis=-1), (1, reps))
    sin_t = jnp.tile(jnp.concatenate([-sin, sin], axis=-1), (1, reps))
    return cos_t, sin_t


def _router_hi_lo(w_router):
    hi = w_router.astype(BF16)
    lo = (w_router - hi.astype(F32)).astype(BF16)
    pad = jnp.zeros((w_router.shape[0], LANES - 2 * N_EXPERTS), BF16)
    return jnp.concatenate([hi, lo, pad], axis=1)


def _layer(x2, batch, seq_len, g_mix, w_in, b_in, sinks, w_conv, w_attn_o, w_conv_o, w_out,
           g_ffn, w_router, b_router, w_gate, b_gate, w_up, b_up, w_down, b_down, g_out):
    T = x2.shape[0]
    cos_t, sin_t = _rope_tables(seq_len)
    q, k, v, z, sa, sc = _inproj(x2, g_mix[None, :], w_in.astype(BF16), b_in[None, :],
                                 cos_t, sin_t, w_conv, seq_len)
    attn = _attention(q, k, v, sinks, batch, seq_len)
    x1, hp, idx_t, wgt_t = _mixout(
        x2, attn, z, sa, sc, w_attn_o.astype(BF16), w_conv_o.astype(BF16), w_out.astype(BF16),
        g_ffn[None, :], _router_hi_lo(w_router), b_router[:, None])

    rank_t, cnt = _rank(idx_t)
    counts = cnt[:, 0]
    padded = (counts + TM_EXPERT - 1) // TM_EXPERT * TM_EXPERT
    pad_end = jnp.cumsum(padded)
    pad_start = pad_end - padded
    n_tiles = (T * TOP_K) // TM_EXPERT + N_EXPERTS
    tile_row = jnp.arange(n_tiles, dtype=jnp.int32) * TM_EXPERT
    blk_e = jnp.minimum(jnp.sum(pad_end[None, :] <= tile_row[:, None], axis=1),
                        N_EXPERTS - 1).astype(jnp.int32)
    n_used = (pad_end[-1:] // TM_EXPERT).astype(jnp.int32)
    e_ids = jnp.arange(N_EXPERTS, dtype=jnp.int32)
    dest = rank_t + jnp.sum(
        jnp.where(idx_t[:, :, None] == e_ids[None, None, :], pad_start[None, None, :], 0), axis=-1)
    dest_flat = dest.reshape(-1).astype(jnp.int32)
    fill_start = (pad_start + counts).astype(jnp.int32)

    next_tile = pad_end[blk_e] // TM_EXPERT
    nxt_e = jnp.where(next_tile < n_used[0],
                      blk_e[jnp.minimum(next_tile, n_tiles - 1)], -1).astype(jnp.int32)

    src = _invperm(dest_flat, fill_start, pad_end.astype(jnp.int32), n_tiles * TM_EXPERT)
    y_slabs = _experts(blk_e, nxt_e, n_used, src, hp.reshape(T, SLABS, LANES),
                       w_gate, b_gate, w_up, b_up,
                       w_down, b_down, n_tiles)
    return _combine(y_slabs.reshape(-1, LANES), x1, wgt_t.T, g_out[None, :])


def kernel(x, g_mix, w_in, b_in, sinks, w_conv, w_attn_o, w_conv_o, w_out, g_ffn, w_router,
           b_router, w_gate, b_gate, w_up, b_up, w_down, b_down, g_final):
    batch, seq_len, d = x.shape
    depth = g_mix.shape[0]
    assert depth == 1, "the final norm is fused into the single layer's combine step"
    x2 = x.reshape(batch * seq_len, d)
    out = _layer(x2, batch, seq_len, g_mix[0], w_in[0], b_in[0], sinks[0], w_conv[0],
                 w_attn_o[0], w_conv_o[0], w_out[0], g_ffn[0], w_router[0], b_router[0],
                 w_gate[0], b_gate[0], w_up[0], b_up[0], w_down[0], b_down[0], g_final)
    return out.reshape(batch, seq_len, d)
```

```python
import functools

import jax
import jax.numpy as jnp
import numpy as np
from jax import lax
from jax.experimental import pallas as pl
from jax.experimental.pallas import tpu as pltpu

D_MODEL = 1024
HEAD_DIM = 64
N_Q_HEADS = 16
N_KV_HEADS = 4
Q_PER_KV = N_Q_HEADS // N_KV_HEADS
ATTN_WIDTH = N_Q_HEADS * HEAD_DIM
KV_WIDTH = N_KV_HEADS * HEAD_DIM
ATTN_BLOCK = 128
WINDOW = 128
ROPE_THETA = 10000.0
CONV_WIDTH = D_MODEL
CONV_KERNEL = 3
N_EXPERTS = 32
TOP_K = 4
SWIGLU_LIMIT = 7.0
SWIGLU_ALPHA = 1.702
RMS_EPS = 1e-5

OFF_Q = 0
OFF_K = OFF_Q + ATTN_WIDTH
OFF_V = OFF_K + KV_WIDTH
OFF_CB = OFF_V + KV_WIDTH
OFF_CC = OFF_CB + CONV_WIDTH
OFF_CX = OFF_CC + CONV_WIDTH
OFF_GA = OFF_CX + CONV_WIDTH
OFF_GC = OFF_GA + D_MODEL
IN_WIDTH = OFF_GC + D_MODEL

LANES = 128
SUBLANES = 8
VMEM_LIMIT_BYTES = 56 * 1024 * 1024
EXPERT_VMEM_LIMIT_BYTES = 60 * 1024 * 1024

TM_INPROJ = 512
COL_CHUNK = 512
Q_BLOCKS_PER_STEP = 8
TM_MIX = 512
TC_RANK = 512
TM_EXPERT = 256
TM_MOVE = 512

BF16 = jnp.bfloat16
F32 = jnp.float32
NEG_BIG = -1e30


def _rms_scale(x, g):
    ms = jnp.mean(x * x, axis=-1, keepdims=True)
    return (x * lax.rsqrt(ms + RMS_EPS)) * g


SLABS = D_MODEL // LANES
assert SLABS == SUBLANES, "a token slab must be exactly one (8, 128) tile of 32-bit words"


def _store_token_major(ref, val):
    tm = val.shape[0]
    for s in range(SLABS):
        ref[pl.ds(s, tm, stride=SLABS), :] = val[:, s * LANES:(s + 1) * LANES]


def _load_token_major(ref, tm):
    return jnp.concatenate(
        [ref[pl.ds(s, tm, stride=SLABS), :] for s in range(SLABS)], axis=1)


def _inproj_body(x_ref, g_ref, w_ref, b_ref, cos_ref, sin_ref, wc_ref,
                 q_ref, k_ref, v_ref, z_ref, sa_ref, sc_ref, carry_ref, *, tiles_per_seq):
    tm = x_ref.shape[0]
    i = pl.program_id(0)
    h = _rms_scale(x_ref[...], g_ref[...]).astype(BF16)

    def proj(c0, width):
        return (jnp.dot(h, w_ref[:, c0:c0 + width], preferred_element_type=F32)
                + b_ref[:, c0:c0 + width])

    cos = cos_ref[...]
    sin = sin_ref[...]
    lane = lax.broadcasted_iota(jnp.int32, (tm, LANES), 1)
    first_half = (lane & (HEAD_DIM // 2)) == 0

    def rope(t):
        partner = jnp.where(first_half,
                            pltpu.roll(t, LANES - HEAD_DIM // 2, 1),
                            pltpu.roll(t, HEAD_DIM // 2, 1))
        return t * cos + partner * sin

    for c in range(0, ATTN_WIDTH, COL_CHUNK):
        acc = proj(OFF_Q + c, COL_CHUNK)
        for j in range(0, COL_CHUNK, LANES):
            q_ref[:, c + j:c + j + LANES] = (
                rope(acc[:, j:j + LANES]) * (HEAD_DIM ** -0.5)).astype(BF16)

    acc = proj(OFF_K, 2 * KV_WIDTH)
    for j in range(0, KV_WIDTH, LANES):
        k_ref[:, j:j + LANES] = rope(acc[:, j:j + LANES]).astype(BF16)
    v_ref[...] = acc[:, KV_WIDTH:].astype(BF16)

    seq_start = (i % tiles_per_seq) == 0
    head = 2 * SUBLANES
    row = lax.broadcasted_iota(jnp.int32, (head, COL_CHUNK), 0)
    for c in range(0, CONV_WIDTH, COL_CHUNK):
        u = proj(OFF_CC + c, COL_CHUNK) * proj(OFF_CX + c, COL_CHUNK)
        cb = proj(OFF_CB + c, COL_CHUNK)
        w0 = wc_ref[0:1, c:c + COL_CHUNK]
        w1 = wc_ref[1:2, c:c + COL_CHUNK]
        w2 = wc_ref[2:3, c:c + COL_CHUNK]
        y = w0 * pltpu.roll(u, 2, 0) + w1 * pltpu.roll(u, 1, 0) + w2 * u
        z_ref[:, c:c + COL_CHUNK] = (cb * y).astype(BF16)
        prev = jnp.where(seq_start, 0.0, carry_ref[:, c:c + COL_CHUNK])
        pad = jnp.zeros((SUBLANES, COL_CHUNK), F32)
        uh = u[0:head]
        u1 = jnp.where(row < 1, jnp.concatenate([pltpu.roll(prev, 1, 0), pad], 0),
                       pltpu.roll(uh, 1, 0))
        u2 = jnp.where(row < 2, jnp.concatenate([pltpu.roll(prev, 2, 0), pad], 0),
                       pltpu.roll(uh, 2, 0))
        yh = w0 * u2 + w1 * u1 + w2 * uh
        z_ref[0:head, c:c + COL_CHUNK] = (cb[0:head] * yh).astype(BF16)
        carry_ref[:, c:c + COL_CHUNK] = u[tm - SUBLANES:tm]

    for c in range(0, D_MODEL, COL_CHUNK):
        sa_ref[:, c:c + COL_CHUNK] = jax.nn.sigmoid(proj(OFF_GA + c, COL_CHUNK)).astype(BF16)
        sc_ref[:, c:c + COL_CHUNK] = jax.nn.sigmoid(proj(OFF_GC + c, COL_CHUNK)).astype(BF16)


def _inproj(x2, g_mix, w_in_bf, b_in, cos_t, sin_t, w_conv, seq_len):
    T = x2.shape[0]
    tm = TM_INPROJ
    tiles_per_seq = seq_len // tm
    const = lambda i: (0, 0)
    row_blk = lambda i: (i, 0)
    pos_blk = lambda i: (i % tiles_per_seq, 0)
    return pl.pallas_call(
        functools.partial(_inproj_body, tiles_per_seq=tiles_per_seq),
        grid=(T // tm,),
        in_specs=[
            pl.BlockSpec((tm, D_MODEL), row_blk),
            pl.BlockSpec((1, D_MODEL), const),
            pl.BlockSpec((D_MODEL, IN_WIDTH), const, pipeline_mode=pl.Buffered(1)),
            pl.BlockSpec((1, IN_WIDTH), const),
            pl.BlockSpec((tm, LANES), pos_blk),
            pl.BlockSpec((tm, LANES), pos_blk),
            pl.BlockSpec((CONV_KERNEL, CONV_WIDTH), const),
        ],
        out_specs=[
            pl.BlockSpec((tm, ATTN_WIDTH), row_blk),
            pl.BlockSpec((tm, KV_WIDTH), row_blk),
            pl.BlockSpec((tm, KV_WIDTH), row_blk),
            pl.BlockSpec((tm, CONV_WIDTH), row_blk),
            pl.BlockSpec((tm, D_MODEL), row_blk),
            pl.BlockSpec((tm, D_MODEL), row_blk),
        ],
        out_shape=[
            jax.ShapeDtypeStruct((T, ATTN_WIDTH), BF16),
            jax.ShapeDtypeStruct((T, KV_WIDTH), BF16),
            jax.ShapeDtypeStruct((T, KV_WIDTH), BF16),
            jax.ShapeDtypeStruct((T, CONV_WIDTH), BF16),
            jax.ShapeDtypeStruct((T, D_MODEL), BF16),
            jax.ShapeDtypeStruct((T, D_MODEL), BF16),
        ],
        scratch_shapes=[pltpu.VMEM((SUBLANES, CONV_WIDTH), F32)],
        compiler_params=pltpu.CompilerParams(
            dimension_semantics=("arbitrary",), vmem_limit_bytes=VMEM_LIMIT_BYTES),
        name="inproj",
    )(x2, g_mix, w_in_bf, b_in, cos_t, sin_t, w_conv)


def _attn_body(sinks_ref, q_ref, kp_ref, kc_ref, vp_ref, vc_ref, o_ref):
    n = pl.program_id(1)
    blk = ATTN_BLOCK
    cols = Q_PER_KV * blk
    j = lax.broadcasted_iota(jnp.int32, (blk, cols), 0)
    qi = lax.broadcasted_iota(jnp.int32, (blk, cols), 1) % blk
    tri = j > qi
    no_prev = tri & (n == 0)
    k3 = jnp.concatenate([kp_ref[...], kc_ref[...]], axis=0)
    v3 = jnp.concatenate([vp_ref[...], vc_ref[...]], axis=0)
    v3_t = v3.astype(F32).T.astype(BF16)
    for qb in range(Q_BLOCKS_PER_STEP):
        k_all = k3[qb * blk:(qb + 2) * blk]
        v_t = v3_t[:, qb * blk:(qb + 2) * blk]
        for kh in range(N_KV_HEADS):
            k_h = k_all[:, kh * HEAD_DIM:(kh + 1) * HEAD_DIM]
            vt_h = v_t[kh * HEAD_DIM:(kh + 1) * HEAD_DIM, :]
            heads = [kh * Q_PER_KV + g for g in range(Q_PER_KV)]
            q_g = jnp.concatenate(
                [q_ref[qb * blk:(qb + 1) * blk, hq * HEAD_DIM:(hq + 1) * HEAD_DIM]
                 for hq in heads], axis=0)
            sink = jnp.concatenate(
                [jnp.full((1, blk), sinks_ref[hq], F32) for hq in heads], axis=1)
            s = lax.dot_general(k_h, q_g, (((1,), (1,)), ((), ())), preferred_element_type=F32)
            fold = jnp.where(tri, s[:blk], s[blk:])
            if qb == 0:
                fold = jnp.where(no_prev, NEG_BIG, fold)
            m = jnp.maximum(jnp.max(fold, axis=0, keepdims=True), sink)
            p = jnp.exp(fold - m)
            denom = jnp.sum(p, axis=0, keepdims=True) + jnp.exp(sink - m)
            p2 = jnp.concatenate([jnp.where(tri, p, 0.0), jnp.where(tri, 0.0, p)], axis=0)
            o_t = jnp.dot(vt_h, p2.astype(BF16), preferred_element_type=F32) / denom
            for g in range(0, Q_PER_KV, 2):
                pair = jnp.concatenate(
                    [o_t[:, g * blk:(g + 1) * blk], o_t[:, (g + 1) * blk:(g + 2) * blk]], axis=0)
                c0 = heads[g] * HEAD_DIM
                o_ref[qb * blk:(qb + 1) * blk, c0:c0 + 2 * HEAD_DIM] = pair.T.astype(BF16)


def _attention(q, k, v, sinks, batch, seq_len):
    T = q.shape[0]
    assert WINDOW == ATTN_BLOCK, "the folded score tile needs window == block"
    rows = Q_BLOCKS_PER_STEP * ATTN_BLOCK
    steps = seq_len // rows
    cur = lambda b, n: (b * steps + n, 0)
    prev = lambda b, n: ((b * steps + n) * Q_BLOCKS_PER_STEP - jnp.minimum(n, 1), 0)
    return pl.pallas_call(
        _attn_body,
        grid=(batch, steps),
        in_specs=[
            pl.BlockSpec(memory_space=pltpu.SMEM),
            pl.BlockSpec((rows, ATTN_WIDTH), cur),
            pl.BlockSpec((ATTN_BLOCK, KV_WIDTH), prev),
            pl.BlockSpec((rows, KV_WIDTH), cur),
            pl.BlockSpec((ATTN_BLOCK, KV_WIDTH), prev),
            pl.BlockSpec((rows, KV_WIDTH), cur),
        ],
        out_specs=pl.BlockSpec((rows, ATTN_WIDTH), cur),
        out_shape=jax.ShapeDtypeStruct((T, ATTN_WIDTH), BF16),
        compiler_params=pltpu.CompilerParams(
            dimension_semantics=("arbitrary", "arbitrary"), vmem_limit_bytes=VMEM_LIMIT_BYTES),
        name="attn",
    )(sinks, q, k, k, v, v)


def _mixout_body(x_ref, a_ref, z_ref, sa_ref, sc_ref, wa_ref, wc_ref, wo_ref, g_ref,
                 wr_ref, br_ref, x1_ref, hp_ref, idx_ref, wgt_ref, slab_ref):
    tm = x_ref.shape[0]
    y_attn = jnp.dot(a_ref[...], wa_ref[...], preferred_element_type=F32)
    y_conv = jnp.dot(z_ref[...], wc_ref[...], preferred_element_type=F32)
    merged = sa_ref[...].astype(F32) * y_attn + sc_ref[...].astype(F32) * y_conv
    x1 = x_ref[...] + jnp.dot(merged.astype(BF16), wo_ref[...], preferred_element_type=F32)
    x1_ref[...] = x1
    h = _rms_scale(x1, g_ref[...])
    _store_token_major(slab_ref, h)
    hp_ref[...] = slab_ref[...].astype(BF16)
    h_hi = h.astype(BF16)
    h_lo = (h - h_hi.astype(F32)).astype(BF16)
    p_hi = jnp.dot(h_hi, wr_ref[...], preferred_element_type=F32)
    p_lo = jnp.dot(h_lo, wr_ref[...], preferred_element_type=F32)
    lg = p_hi + pltpu.roll(p_hi, LANES - N_EXPERTS, 1) + p_lo
    logits = lg.T[0:N_EXPERTS, :] + br_ref[...]
    e_iota = lax.broadcasted_iota(jnp.int32, (N_EXPERTS, tm), 0)
    vals, idxs = [], []
    for _ in range(TOP_K):
        m = jnp.max(logits, axis=0, keepdims=True)
        idx = jnp.min(jnp.where(logits == m, e_iota, N_EXPERTS), axis=0, keepdims=True)
        vals.append(m)
        idxs.append(idx)
        logits = jnp.where(e_iota == idx, -jnp.inf, logits)
    ex = [jnp.exp(v - vals[0]) for v in vals]
    tot = ex[0] + ex[1] + ex[2] + ex[3]
    idx_ref[...] = jnp.concatenate(idxs, axis=0)
    wgt_ref[...] = jnp.concatenate([e / tot for e in ex], axis=0)


def _mixout(x2, attn, z, sa, sc, wa_bf, wc_bf, wo_bf, g_ffn, wr_t, b_router):
    T = x2.shape[0]
    tm = TM_MIX
    const = lambda i: (0, 0)
    row_blk = lambda i: (i, 0)
    col_blk = lambda i: (0, i)
    act = pl.BlockSpec((tm, D_MODEL), row_blk)
    wsq = pl.BlockSpec((D_MODEL, D_MODEL), const)
    return pl.pallas_call(
        _mixout_body,
        grid=(T // tm,),
        in_specs=[act, act, act, act, act, wsq, wsq, wsq,
                  pl.BlockSpec((1, D_MODEL), const),
                  pl.BlockSpec((D_MODEL, LANES), const),
                  pl.BlockSpec((N_EXPERTS, 1), const)],
        out_specs=[act,
                   pl.BlockSpec((tm * SLABS, LANES), row_blk),
                   pl.BlockSpec((TOP_K, tm), col_blk),
                   pl.BlockSpec((TOP_K, tm), col_blk)],
        out_shape=[jax.ShapeDtypeStruct((T, D_MODEL), F32),
                   jax.ShapeDtypeStruct((T * SLABS, LANES), BF16),
                   jax.ShapeDtypeStruct((TOP_K, T), jnp.int32),
                   jax.ShapeDtypeStruct((TOP_K, T), F32)],
        scratch_shapes=[pltpu.VMEM((tm * SLABS, LANES), F32)],
        compiler_params=pltpu.CompilerParams(
            dimension_semantics=("arbitrary",), vmem_limit_bytes=VMEM_LIMIT_BYTES),
        name="mixout",
    )(x2, attn, z, sa, sc, wa_bf, wc_bf, wo_bf, g_ffn, wr_t, b_router)


def _rank_body(idx_ref, rank_ref, cnt_ref, carry_ref):
    tc = idx_ref.shape[1]
    i = pl.program_id(0)

    @pl.when(i == 0)
    def _():
        carry_ref[...] = jnp.zeros_like(carry_ref)

    idx = idx_ref[...]
    e_iota = lax.broadcasted_iota(jnp.int32, (N_EXPERTS, tc), 0)
    sel = [e_iota == idx[k:k + 1, :] for k in range(TOP_K)]
    member = (sel[0] | sel[1] | sel[2] | sel[3])
    onehot = jnp.where(member, 1.0, 0.0).astype(BF16)
    r = lax.broadcasted_iota(jnp.int32, (tc, tc), 0)
    c = lax.broadcasted_iota(jnp.int32, (tc, tc), 1)
    before = jnp.where(r < c, 1.0, 0.0).astype(BF16)
    carry = carry_ref[:, 0:1]
    prefix = jnp.dot(onehot, before, preferred_element_type=F32) + carry
    ranks = [jnp.sum(jnp.where(sel[k], prefix, 0.0), axis=0, keepdims=True)
             for k in range(TOP_K)]
    rank_ref[...] = jnp.concatenate(ranks, axis=0).astype(jnp.int32)
    total = carry + jnp.sum(onehot.astype(F32), axis=1, keepdims=True)
    carry_ref[...] = jnp.broadcast_to(total, carry_ref.shape)
    cnt_ref[...] = jnp.broadcast_to(total, cnt_ref.shape).astype(jnp.int32)


def _rank(idx_t):
    T = idx_t.shape[1]
    tc = TC_RANK
    return pl.pallas_call(
        _rank_body,
        grid=(T // tc,),
        in_specs=[pl.BlockSpec((TOP_K, tc), lambda i: (0, i))],
        out_specs=[pl.BlockSpec((TOP_K, tc), lambda i: (0, i)),
                   pl.BlockSpec((N_EXPERTS, LANES), lambda i: (0, 0))],
        out_shape=[jax.ShapeDtypeStruct((TOP_K, T), jnp.int32),
                   jax.ShapeDtypeStruct((N_EXPERTS, LANES), jnp.int32)],
        scratch_shapes=[pltpu.VMEM((N_EXPERTS, LANES), F32)],
        compiler_params=pltpu.CompilerParams(dimension_semantics=("arbitrary",)),
        name="rank",
    )(idx_t)


def _invperm_body(dest_ref, fill_ref, end_ref, src_ref, *, n_assign, n_slots):
    def mark_tile(start):
        def mark(q, carry):
            p = start + q
            src_ref[p] = n_assign + (p & (2 * TM_EXPERT - 1))
            return carry
        lax.fori_loop(0, TM_EXPERT, mark, 0, unroll=16)

    def mark_expert_tail(e, carry):
        mark_tile(fill_ref[e])
        return carry

    lax.fori_loop(0, N_EXPERTS, mark_expert_tail, 0)

    def mark_unused(t, carry):
        mark_tile(t * TM_EXPERT)
        return carry

    lax.fori_loop(end_ref[N_EXPERTS - 1] // TM_EXPERT, n_slots // TM_EXPERT + 1, mark_unused, 0)

    def place(a, carry):
        src_ref[dest_ref[a]] = a
        return carry

    lax.fori_loop(0, n_assign, place, 0, unroll=64)


def _invperm(dest_flat, fill_start, pad_end, n_slots):
    return pl.pallas_call(
        functools.partial(_invperm_body, n_assign=dest_flat.shape[0], n_slots=n_slots),
        grid_spec=pltpu.PrefetchScalarGridSpec(
            num_scalar_prefetch=3,
            grid=(1,),
            in_specs=[],
            out_specs=pl.BlockSpec(memory_space=pltpu.SMEM),
        ),
        out_shape=jax.ShapeDtypeStruct((n_slots + TM_EXPERT,), jnp.int32),
        compiler_params=pltpu.CompilerParams(dimension_semantics=("arbitrary",)),
        name="invperm",
    )(dest_flat, fill_start, pad_end)


def _expert_body(blk_e_ref, nxt_e_ref, n_used_ref, src_ref, hp_ref, wg_hbm, wu_hbm, wd_hbm,
                 bg_ref, bu_ref, bd_ref, y_ref, stage_bf, stage, xbuf, ybuf, wst, w_bf, wsem, ssem, *,
                 n_tok):
    tm = TM_EXPERT
    i = pl.program_id(0)
    n_used = n_used_ref[0]
    slot = i % 2
    e = blk_e_ref[i]
    e_prev = blk_e_ref[jnp.maximum(i - 1, 0)]

    def weight_copies(ex):
        return [pltpu.make_async_copy(w.at[ex], wst.at[m], wsem.at[m])
                for m, w in enumerate((wg_hbm, wu_hbm, wd_hbm))]

    def rows_of(buf):
        return pl.ds(pl.multiple_of(buf * tm, tm), tm)

    def gather_rows(tile, rows):
        for r in rows:
            a = src_ref[tile * tm + r]
            stage_bf[pl.ds(r * SLABS, SLABS), :] = hp_ref[a & (n_tok - 1)]

    def finish_gather(buf):
        stage[...] = stage_bf[...].astype(F32)
        xbuf[rows_of(buf), :] = _load_token_major(stage, tm).astype(BF16)

    def scatter_copy(tile, buf, r, to_spare):
        a = src_ref[tile * tm + r]
        if to_spare is not None:
            a = jnp.where(to_spare, TOP_K * n_tok + buf * tm + r, a)
        return pltpu.make_async_copy(ybuf.at[buf, pl.ds(r * SLABS, SLABS)], y_ref.at[a],
                                     ssem.at[buf])

    @pl.when(i == 0)
    def _():
        for c in weight_copies(e):
            c.start()
        gather_rows(0, range(tm))
        finish_gather(0)
        ybuf[...] = jnp.zeros_like(ybuf)
        spare = [pltpu.make_async_copy(ybuf.at[b, pl.ds(r * SLABS, SLABS)],
                                       y_ref.at[TOP_K * n_tok + b * tm + r], ssem.at[b])
                 for b in range(2) for r in range(tm)]
        for c in spare:
            c.start()
        for c in spare:
            c.wait()

    @pl.when(i < n_used)
    def _():
        @pl.when(i >= 1)
        def _():
            for r in range(tm):
                scatter_copy(jnp.maximum(i - 2, 0), slot, r, i == 1).wait()

        @pl.when((i == 0) | (e != e_prev))
        def _():
            for c in weight_copies(e):
                c.wait()
            rows = D_MODEL // SUBLANES
            for m in range(3):
                def cast_rows(j, carry, m=m):
                    sl = pl.ds(pl.multiple_of(j * rows, rows), rows)
                    w_bf[m, sl, :] = wst[m, sl, :].astype(BF16)
                    return carry
                lax.fori_loop(0, SUBLANES, cast_rows, 0)
            nxt_e = nxt_e_ref[i]

            @pl.when(nxt_e >= 0)
            def _():
                for c in weight_copies(nxt_e):
                    c.start()

        x = xbuf[rows_of(slot), :]
        gather_rows(jnp.minimum(i + 1, n_used - 1), range(tm))
        finish_gather(1 - slot)
        for r in range(tm):
            scatter_copy(jnp.maximum(i - 1, 0), 1 - slot, r, i == 0).start()
        g = jnp.dot(x, w_bf[0], preferred_element_type=F32) + bg_ref[pl.ds(e, 1), :]
        u = jnp.dot(x, w_bf[1], preferred_element_type=F32) + bu_ref[pl.ds(e, 1), :]
        g = jnp.minimum(g, SWIGLU_LIMIT)
        u = jnp.clip(u, -SWIGLU_LIMIT, SWIGLU_LIMIT)
        a = g * jax.nn.sigmoid(SWIGLU_ALPHA * g) * (u + 1.0)
        y = jnp.dot(a.astype(BF16), w_bf[2], preferred_element_type=F32) + bd_ref[pl.ds(e, 1), :]
        _store_token_major(ybuf.at[slot], y)

        @pl.when(i == n_used - 1)
        def _():
            for r in range(tm):
                scatter_copy(i, slot, r, None).start()
            for r in range(tm):
                scatter_copy(jnp.maximum(i - 1, 0), 1 - slot, r, i == 0).wait()
            for r in range(tm):
                scatter_copy(i, slot, r, None).wait()


def _experts(blk_e, nxt_e, n_used, src, hp, w_gate, b_gate, w_up, b_up, w_down, b_down, n_tiles):
    tm = TM_EXPERT
    d_ff = w_gate.shape[2]
    assert d_ff == D_MODEL
    n_tok = hp.shape[0]
    assert n_tok & (n_tok - 1) == 0, "assignment ids are split with a power-of-two mask"
    assert n_tiles >= 2
    const2 = lambda i, *_: (0, 0)
    bias = pl.BlockSpec((N_EXPERTS, D_MODEL), const2)
    hbm = pl.BlockSpec(memory_space=pl.ANY)
    return pl.pallas_call(
        functools.partial(_expert_body, n_tok=n_tok),
        grid_spec=pltpu.PrefetchScalarGridSpec(
            num_scalar_prefetch=4,
            grid=(n_tiles,),
            in_specs=[
                pl.BlockSpec((n_tok, SLABS, LANES), lambda i, *_: (0, 0, 0),
                             pipeline_mode=pl.Buffered(1)),
                hbm, hbm, hbm, bias, bias, bias,
            ],
            out_specs=pl.BlockSpec(memory_space=pl.ANY),
            scratch_shapes=[pltpu.VMEM((tm * SLABS, LANES), BF16),
                            pltpu.VMEM((tm * SLABS, LANES), F32),
                            pltpu.VMEM((2 * tm, D_MODEL), BF16),
                            pltpu.VMEM((2, tm * SLABS, LANES), F32),
                            pltpu.VMEM((3, D_MODEL, D_MODEL), F32),
                            pltpu.VMEM((3, D_MODEL, D_MODEL), BF16),
                            pltpu.SemaphoreType.DMA((3,)),
                            pltpu.SemaphoreType.DMA((2,))],
        ),
        out_shape=jax.ShapeDtypeStruct((TOP_K * n_tok + 2 * tm, SLABS, LANES), F32),
        compiler_params=pltpu.CompilerParams(
            dimension_semantics=("arbitrary",), vmem_limit_bytes=EXPERT_VMEM_LIMIT_BYTES),
        name="experts",
    )(blk_e, nxt_e, n_used, src, hp, w_gate, w_up, w_down, b_gate, b_up, b_down)


def _combine_body(y0_ref, y1_ref, y2_ref, y3_ref, x1_ref, wgt_ref, g_ref, o_ref):
    tm = x1_ref.shape[0]
    wgt = wgt_ref[...]
    acc = x1_ref[...]
    for k, yk_ref in enumerate((y0_ref, y1_ref, y2_ref, y3_ref)):
        acc = acc + wgt[:, k:k + 1] * _load_token_major(yk_ref, tm)
    o_ref[...] = _rms_scale(acc, g_ref[...])


def _combine(y_slabs, x1, wgt_rows, g_final):
    T = x1.shape[0]
    tm = TM_MOVE
    steps = T // tm
    row_blk = lambda i: (i, 0)
    y_specs = [pl.BlockSpec((tm * SLABS, LANES), functools.partial(lambda i, k: (k * steps + i, 0), k=k))
               for k in range(TOP_K)]
    return pl.pallas_call(
        _combine_body,
        grid=(steps,),
        in_specs=y_specs + [
            pl.BlockSpec((tm, D_MODEL), row_blk),
            pl.BlockSpec((tm, TOP_K), row_blk),
            pl.BlockSpec((1, D_MODEL), lambda i: (0, 0)),
        ],
        out_specs=pl.BlockSpec((tm, D_MODEL), row_blk),
        out_shape=jax.ShapeDtypeStruct((T, D_MODEL), F32),
        compiler_params=pltpu.CompilerParams(
            dimension_semantics=("arbitrary",), vmem_limit_bytes=VMEM_LIMIT_BYTES),
        name="combine",
    )(y_slabs, y_slabs, y_slabs, y_slabs, x1, wgt_rows, g_final)


def _rope_tables(seq_len):
    half = HEAD_DIM // 2
    inv_freq = ROPE_THETA ** (-np.arange(half, dtype=np.float64) / half)
    ang = np.arange(seq_len, dtype=np.float64)[:, None] * inv_freq[None, :]
    cos = np.cos(ang).astype(np.float32)
    sin = np.sin(ang).astype(np.float32)
    reps = LANES // HEAD_DIM
    cos_t = np.tile(np.concatenate([cos, cos], axis=-1), (1, reps))
    sin_t = np.tile(np.concatenate([-sin, sin], axis=-1), (1, reps))
    return jnp.asarray(cos_t), jnp.asarray(sin_t)


def _router_hi_lo(w_router):
    hi = w_router.astype(BF16)
    lo = (w_router - hi.astype(F32)).astype(BF16)
    pad = jnp.zeros((w_router.shape[0], LANES - 2 * N_EXPERTS), BF16)
    return jnp.concatenate([hi, lo, pad], axis=1)


def _layer(x2, batch, seq_len, g_mix, w_in, b_in, sinks, w_conv, w_attn_o, w_conv_o, w_out,
           g_ffn, w_router, b_router, w_gate, b_gate, w_up, b_up, w_down, b_down, g_out):
    T = x2.shape[0]
    cos_t, sin_t = _rope_tables(seq_len)
    q, k, v, z, sa, sc = _inproj(x2, g_mix[None, :], w_in.astype(BF16), b_in[None, :],
                                 cos_t, sin_t, w_conv, seq_len)
    attn = _attention(q, k, v, sinks, batch, seq_len)
    x1, hp, idx_t, wgt_t = _mixout(
        x2, attn, z, sa, sc, w_attn_o.astype(BF16), w_conv_o.astype(BF16), w_out.astype(BF16),
        g_ffn[None, :], _router_hi_lo(w_router), b_router[:, None])

    rank_t, cnt = _rank(idx_t)
    counts = cnt[:, 0]
    padded = (counts + TM_EXPERT - 1) // TM_EXPERT * TM_EXPERT
    pad_end = jnp.cumsum(padded)
    pad_start = pad_end - padded
    n_tiles = (T * TOP_K) // TM_EXPERT + N_EXPERTS
    tile_row = jnp.arange(n_tiles, dtype=jnp.int32) * TM_EXPERT
    blk_e = jnp.minimum(jnp.sum(pad_end[None, :] <= tile_row[:, None], axis=1),
                        N_EXPERTS - 1).astype(jnp.int32)
    n_used = (pad_end[-1:] // TM_EXPERT).astype(jnp.int32)
    e_ids = jnp.arange(N_EXPERTS, dtype=jnp.int32)
    dest = rank_t + jnp.sum(
        jnp.where(idx_t[:, :, None] == e_ids[None, None, :], pad_start[None, None, :], 0), axis=-1)
    dest_flat = dest.reshape(-1).astype(jnp.int32)
    fill_start = (pad_start + counts).astype(jnp.int32)

    group_end = jnp.sum(jnp.where(blk_e[:, None] == e_ids[None, :], pad_end[None, :], 0), axis=1)
    nxt_e = jnp.where(
        group_end < pad_end[-1],
        jnp.minimum(jnp.sum(pad_end[None, :] <= group_end[:, None], axis=1), N_EXPERTS - 1),
        -1).astype(jnp.int32)

    src = _invperm(dest_flat, fill_start, pad_end.astype(jnp.int32), n_tiles * TM_EXPERT)
    y_slabs = _experts(blk_e, nxt_e, n_used, src, hp.reshape(T, SLABS, LANES),
                       w_gate, b_gate, w_up, b_up, w_down, b_down, n_tiles)
    return _combine(y_slabs.reshape(-1, LANES), x1, wgt_t.T, g_out[None, :])


def kernel(x, g_mix, w_in, b_in, sinks, w_conv, w_attn_o, w_conv_o, w_out, g_ffn, w_router,
           b_router, w_gate, b_gate, w_up, b_up, w_down, b_down, g_final):
    batch, seq_len, d = x.shape
    depth = g_mix.shape[0]
    assert depth == 1, "the final norm is fused into the single layer's combine step"
    x2 = x.reshape(batch * seq_len, d)
    out = _layer(x2, batch, seq_len, g_mix[0], w_in[0], b_in[0], sinks[0], w_conv[0],
                 w_attn_o[0], w_conv_o[0], w_out[0], g_ffn[0], w_router[0], b_router[0],
                 w_gate[0], b_gate[0], w_up[0], b_up[0], w_down[0], b_down[0], g_final)
    return out.reshape(batch, seq_len, d)
```

```python
import functools

import jax
import jax.numpy as jnp
import numpy as np
from jax import lax
from jax.experimental import pallas as pl
from jax.experimental.pallas import tpu as pltpu

D_MODEL = 1024
HEAD_DIM = 64
N_Q_HEADS = 16
N_KV_HEADS = 4
Q_PER_KV = N_Q_HEADS // N_KV_HEADS
ATTN_WIDTH = N_Q_HEADS * HEAD_DIM
KV_WIDTH = N_KV_HEADS * HEAD_DIM
ATTN_BLOCK = 128
WINDOW = 128
ROPE_THETA = 10000.0
CONV_WIDTH = D_MODEL
CONV_KERNEL = 3
N_EXPERTS = 32
TOP_K = 4
SWIGLU_LIMIT = 7.0
SWIGLU_ALPHA = 1.702
RMS_EPS = 1e-5

OFF_Q = 0
OFF_K = OFF_Q + ATTN_WIDTH
OFF_V = OFF_K + KV_WIDTH
OFF_CB = OFF_V + KV_WIDTH
OFF_CC = OFF_CB + CONV_WIDTH
OFF_CX = OFF_CC + CONV_WIDTH
OFF_GA = OFF_CX + CONV_WIDTH
OFF_GC = OFF_GA + D_MODEL
IN_WIDTH = OFF_GC + D_MODEL

LANES = 128
SUBLANES = 8
VMEM_LIMIT_BYTES = 56 * 1024 * 1024
EXPERT_VMEM_LIMIT_BYTES = 60 * 1024 * 1024

TM_INPROJ = 512
COL_CHUNK = 512
Q_BLOCKS_PER_STEP = 8
TM_MIX = 512
TC_RANK = 512
TM_EXPERT = 256
TM_MOVE = 512
PIECE_ROWS = 64
assert TM_MOVE == TC_RANK

BF16 = jnp.bfloat16
F32 = jnp.float32
NEG_BIG = -1e30


def _rms_scale(x, g):
    ms = jnp.mean(x * x, axis=-1, keepdims=True)
    return (x * lax.rsqrt(ms + RMS_EPS)) * g


SLABS = D_MODEL // LANES
assert SLABS == SUBLANES, "a token slab must be exactly one (8, 128) tile of 32-bit words"


def _store_token_major(ref, val):
    tm = val.shape[0]
    for s in range(SLABS):
        ref[pl.ds(s, tm, stride=SLABS), :] = val[:, s * LANES:(s + 1) * LANES]


def _load_token_major(ref, tm):
    return jnp.concatenate(
        [ref[pl.ds(s, tm, stride=SLABS), :] for s in range(SLABS)], axis=1)


def _inproj_body(x_ref, g_ref, w_ref, b_ref, cos_ref, sin_ref, wc_ref,
                 q_ref, k_ref, v_ref, z_ref, sa_ref, sc_ref, carry_ref, *, tiles_per_seq):
    tm = x_ref.shape[0]
    i = pl.program_id(0)
    h = _rms_scale(x_ref[...], g_ref[...]).astype(BF16)

    def proj(c0, width):
        return (jnp.dot(h, w_ref[:, c0:c0 + width], preferred_element_type=F32)
                + b_ref[:, c0:c0 + width])

    cos = cos_ref[...]
    sin = sin_ref[...]
    lane = lax.broadcasted_iota(jnp.int32, (tm, LANES), 1)
    first_half = (lane & (HEAD_DIM // 2)) == 0

    def rope(t):
        partner = jnp.where(first_half,
                            pltpu.roll(t, LANES - HEAD_DIM // 2, 1),
                            pltpu.roll(t, HEAD_DIM // 2, 1))
        return t * cos + partner * sin

    for c in range(0, ATTN_WIDTH, COL_CHUNK):
        acc = proj(OFF_Q + c, COL_CHUNK)
        for j in range(0, COL_CHUNK, LANES):
            q_ref[:, c + j:c + j + LANES] = (
                rope(acc[:, j:j + LANES]) * (HEAD_DIM ** -0.5)).astype(BF16)

    acc = proj(OFF_K, 2 * KV_WIDTH)
    for j in range(0, KV_WIDTH, LANES):
        k_ref[:, j:j + LANES] = rope(acc[:, j:j + LANES]).astype(BF16)
    v_ref[...] = acc[:, KV_WIDTH:].astype(BF16)

    seq_start = (i % tiles_per_seq) == 0
    head = 2 * SUBLANES
    row = lax.broadcasted_iota(jnp.int32, (head, COL_CHUNK), 0)
    for c in range(0, CONV_WIDTH, COL_CHUNK):
        u = proj(OFF_CC + c, COL_CHUNK) * proj(OFF_CX + c, COL_CHUNK)
        cb = proj(OFF_CB + c, COL_CHUNK)
        w0 = wc_ref[0:1, c:c + COL_CHUNK]
        w1 = wc_ref[1:2, c:c + COL_CHUNK]
        w2 = wc_ref[2:3, c:c + COL_CHUNK]
        y = w0 * pltpu.roll(u, 2, 0) + w1 * pltpu.roll(u, 1, 0) + w2 * u
        z_ref[:, c:c + COL_CHUNK] = (cb * y).astype(BF16)
        prev = jnp.where(seq_start, 0.0, carry_ref[:, c:c + COL_CHUNK])
        pad = jnp.zeros((SUBLANES, COL_CHUNK), F32)
        uh = u[0:head]
        u1 = jnp.where(row < 1, jnp.concatenate([pltpu.roll(prev, 1, 0), pad], 0),
                       pltpu.roll(uh, 1, 0))
        u2 = jnp.where(row < 2, jnp.concatenate([pltpu.roll(prev, 2, 0), pad], 0),
                       pltpu.roll(uh, 2, 0))
        yh = w0 * u2 + w1 * u1 + w2 * uh
        z_ref[0:head, c:c + COL_CHUNK] = (cb[0:head] * yh).astype(BF16)
        carry_ref[:, c:c + COL_CHUNK] = u[tm - SUBLANES:tm]

    for c in range(0, D_MODEL, COL_CHUNK):
        sa_ref[:, c:c + COL_CHUNK] = jax.nn.sigmoid(proj(OFF_GA + c, COL_CHUNK)).astype(BF16)
        sc_ref[:, c:c + COL_CHUNK] = jax.nn.sigmoid(proj(OFF_GC + c, COL_CHUNK)).astype(BF16)


def _inproj(x2, g_mix, w_in_bf, b_in, cos_t, sin_t, w_conv, seq_len):
    T = x2.shape[0]
    tm = TM_INPROJ
    tiles_per_seq = seq_len // tm
    const = lambda i: (0, 0)
    row_blk = lambda i: (i, 0)
    pos_blk = lambda i: (i % tiles_per_seq, 0)
    return pl.pallas_call(
        functools.partial(_inproj_body, tiles_per_seq=tiles_per_seq),
        grid=(T // tm,),
        in_specs=[
            pl.BlockSpec((tm, D_MODEL), row_blk),
            pl.BlockSpec((1, D_MODEL), const),
            pl.BlockSpec((D_MODEL, IN_WIDTH), const, pipeline_mode=pl.Buffered(1)),
            pl.BlockSpec((1, IN_WIDTH), const),
            pl.BlockSpec((tm, LANES), pos_blk),
            pl.BlockSpec((tm, LANES), pos_blk),
            pl.BlockSpec((CONV_KERNEL, CONV_WIDTH), const),
        ],
        out_specs=[
            pl.BlockSpec((tm, ATTN_WIDTH), row_blk),
            pl.BlockSpec((tm, KV_WIDTH), row_blk),
            pl.BlockSpec((tm, KV_WIDTH), row_blk),
            pl.BlockSpec((tm, CONV_WIDTH), row_blk),
            pl.BlockSpec((tm, D_MODEL), row_blk),
            pl.BlockSpec((tm, D_MODEL), row_blk),
        ],
        out_shape=[
            jax.ShapeDtypeStruct((T, ATTN_WIDTH), BF16),
            jax.ShapeDtypeStruct((T, KV_WIDTH), BF16),
            jax.ShapeDtypeStruct((T, KV_WIDTH), BF16),
            jax.ShapeDtypeStruct((T, CONV_WIDTH), BF16),
            jax.ShapeDtypeStruct((T, D_MODEL), BF16),
            jax.ShapeDtypeStruct((T, D_MODEL), BF16),
        ],
        scratch_shapes=[pltpu.VMEM((SUBLANES, CONV_WIDTH), F32)],
        compiler_params=pltpu.CompilerParams(
            dimension_semantics=("arbitrary",), vmem_limit_bytes=VMEM_LIMIT_BYTES),
        name="inproj",
    )(x2, g_mix, w_in_bf, b_in, cos_t, sin_t, w_conv)


def _attn_body(sinks_ref, q_ref, kp_ref, kc_ref, vp_ref, vc_ref, o_ref):
    n = pl.program_id(1)
    blk = ATTN_BLOCK
    cols = Q_PER_KV * blk
    j = lax.broadcasted_iota(jnp.int32, (blk, cols), 0)
    qi = lax.broadcasted_iota(jnp.int32, (blk, cols), 1) % blk
    tri = j > qi
    no_prev = tri & (n == 0)
    k3 = jnp.concatenate([kp_ref[...], kc_ref[...]], axis=0)
    v3 = jnp.concatenate([vp_ref[...], vc_ref[...]], axis=0)
    v3_t = v3.astype(F32).T.astype(BF16)
    for qb in range(Q_BLOCKS_PER_STEP):
        k_all = k3[qb * blk:(qb + 2) * blk]
        v_t = v3_t[:, qb * blk:(qb + 2) * blk]
        for kh in range(N_KV_HEADS):
            k_h = k_all[:, kh * HEAD_DIM:(kh + 1) * HEAD_DIM]
            vt_h = v_t[kh * HEAD_DIM:(kh + 1) * HEAD_DIM, :]
            heads = [kh * Q_PER_KV + g for g in range(Q_PER_KV)]
            q_g = jnp.concatenate(
                [q_ref[qb * blk:(qb + 1) * blk, hq * HEAD_DIM:(hq + 1) * HEAD_DIM]
                 for hq in heads], axis=0)
            sink = jnp.concatenate(
                [jnp.full((1, blk), sinks_ref[hq], F32) for hq in heads], axis=1)
            s = lax.dot_general(k_h, q_g, (((1,), (1,)), ((), ())), preferred_element_type=F32)
            fold = jnp.where(tri, s[:blk], s[blk:])
            if qb == 0:
                fold = jnp.where(no_prev, NEG_BIG, fold)
            m = jnp.maximum(jnp.max(fold, axis=0, keepdims=True), sink)
            p = jnp.exp(fold - m)
            denom = jnp.sum(p, axis=0, keepdims=True) + jnp.exp(sink - m)
            p2 = jnp.concatenate([jnp.where(tri, p, 0.0), jnp.where(tri, 0.0, p)], axis=0)
            o_t = jnp.dot(vt_h, p2.astype(BF16), preferred_element_type=F32) / denom
            for g in range(0, Q_PER_KV, 2):
                pair = jnp.concatenate(
                    [o_t[:, g * blk:(g + 1) * blk], o_t[:, (g + 1) * blk:(g + 2) * blk]], axis=0)
                c0 = heads[g] * HEAD_DIM
                o_ref[qb * blk:(qb + 1) * blk, c0:c0 + 2 * HEAD_DIM] = pair.T.astype(BF16)


def _attention(q, k, v, sinks, batch, seq_len):
    T = q.shape[0]
    assert WINDOW == ATTN_BLOCK, "the folded score tile needs window == block"
    rows = Q_BLOCKS_PER_STEP * ATTN_BLOCK
    steps = seq_len // rows
    cur = lambda b, n: (b * steps + n, 0)
    prev = lambda b, n: ((b * steps + n) * Q_BLOCKS_PER_STEP - jnp.minimum(n, 1), 0)
    return pl.pallas_call(
        _attn_body,
        grid=(batch, steps),
        in_specs=[
            pl.BlockSpec(memory_space=pltpu.SMEM),
            pl.BlockSpec((rows, ATTN_WIDTH), cur),
            pl.BlockSpec((ATTN_BLOCK, KV_WIDTH), prev),
            pl.BlockSpec((rows, KV_WIDTH), cur),
            pl.BlockSpec((ATTN_BLOCK, KV_WIDTH), prev),
            pl.BlockSpec((rows, KV_WIDTH), cur),
        ],
        out_specs=pl.BlockSpec((rows, ATTN_WIDTH), cur),
        out_shape=jax.ShapeDtypeStruct((T, ATTN_WIDTH), BF16),
        compiler_params=pltpu.CompilerParams(
            dimension_semantics=("arbitrary", "arbitrary"), vmem_limit_bytes=VMEM_LIMIT_BYTES),
        name="attn",
    )(sinks, q, k, k, v, v)


def _mixout_body(x_ref, a_ref, z_ref, sa_ref, sc_ref, wa_ref, wc_ref, wo_ref, g_ref,
                 wr_ref, br_ref, x1_ref, hp_ref, idx_ref, wgt_ref, slab_ref):
    tm = x_ref.shape[0]
    y_attn = jnp.dot(a_ref[...], wa_ref[...], preferred_element_type=F32)
    y_conv = jnp.dot(z_ref[...], wc_ref[...], preferred_element_type=F32)
    merged = sa_ref[...].astype(F32) * y_attn + sc_ref[...].astype(F32) * y_conv
    x1 = x_ref[...] + jnp.dot(merged.astype(BF16), wo_ref[...], preferred_element_type=F32)
    x1_ref[...] = x1
    h = _rms_scale(x1, g_ref[...])
    _store_token_major(slab_ref, h)
    hp_ref[...] = slab_ref[...].astype(BF16)
    h_hi = h.astype(BF16)
    h_lo = (h - h_hi.astype(F32)).astype(BF16)
    p_hi = jnp.dot(h_hi, wr_ref[...], preferred_element_type=F32)
    p_lo = jnp.dot(h_lo, wr_ref[...], preferred_element_type=F32)
    lg = p_hi + pltpu.roll(p_hi, LANES - N_EXPERTS, 1) + p_lo
    logits = lg.T[0:N_EXPERTS, :] + br_ref[...]
    e_iota = lax.broadcasted_iota(jnp.int32, (N_EXPERTS, tm), 0)
    vals, idxs = [], []
    for _ in range(TOP_K):
        m = jnp.max(logits, axis=0, keepdims=True)
        idx = jnp.min(jnp.where(logits == m, e_iota, N_EXPERTS), axis=0, keepdims=True)
        vals.append(m)
        idxs.append(idx)
        logits = jnp.where(e_iota == idx, -jnp.inf, logits)
    ex = [jnp.exp(v - vals[0]) for v in vals]
    tot = ex[0] + ex[1] + ex[2] + ex[3]
    idx_ref[...] = jnp.concatenate(idxs, axis=0)
    wgt_ref[...] = jnp.concatenate([e / tot for e in ex], axis=0)


def _mixout(x2, attn, z, sa, sc, wa_bf, wc_bf, wo_bf, g_ffn, wr_t, b_router):
    T = x2.shape[0]
    tm = TM_MIX
    const = lambda i: (0, 0)
    row_blk = lambda i: (i, 0)
    col_blk = lambda i: (0, i)
    act = pl.BlockSpec((tm, D_MODEL), row_blk)
    wsq = pl.BlockSpec((D_MODEL, D_MODEL), const)
    return pl.pallas_call(
        _mixout_body,
        grid=(T // tm,),
        in_specs=[act, act, act, act, act, wsq, wsq, wsq,
                  pl.BlockSpec((1, D_MODEL), const),
                  pl.BlockSpec((D_MODEL, LANES), const),
                  pl.BlockSpec((N_EXPERTS, 1), const)],
        out_specs=[act,
                   pl.BlockSpec((tm * SLABS, LANES), row_blk),
                   pl.BlockSpec((TOP_K, tm), col_blk),
                   pl.BlockSpec((TOP_K, tm), col_blk)],
        out_shape=[jax.ShapeDtypeStruct((T, D_MODEL), F32),
                   jax.ShapeDtypeStruct((T * SLABS, LANES), BF16),
                   jax.ShapeDtypeStruct((TOP_K, T), jnp.int32),
                   jax.ShapeDtypeStruct((TOP_K, T), F32)],
        scratch_shapes=[pltpu.VMEM((tm * SLABS, LANES), F32)],
        compiler_params=pltpu.CompilerParams(
            dimension_semantics=("arbitrary",), vmem_limit_bytes=VMEM_LIMIT_BYTES),
        name="mixout",
    )(x2, attn, z, sa, sc, wa_bf, wc_bf, wo_bf, g_ffn, wr_t, b_router)


def _rank_body(idx_ref, rank_ref, cnt_ref, before_ref, carry_ref):
    tc = idx_ref.shape[1]
    i = pl.program_id(0)

    @pl.when(i == 0)
    def _():
        carry_ref[...] = jnp.zeros_like(carry_ref)

    idx = idx_ref[...]
    e_iota = lax.broadcasted_iota(jnp.int32, (N_EXPERTS, tc), 0)
    sel = [e_iota == idx[k:k + 1, :] for k in range(TOP_K)]
    member = (sel[0] | sel[1] | sel[2] | sel[3])
    onehot = jnp.where(member, 1.0, 0.0).astype(BF16)
    r = lax.broadcasted_iota(jnp.int32, (tc, tc), 0)
    c = lax.broadcasted_iota(jnp.int32, (tc, tc), 1)
    before = jnp.where(r < c, 1.0, 0.0).astype(BF16)
    carry = carry_ref[:, 0:1]
    before_ref[...] = carry_ref[...].astype(jnp.int32)
    prefix =jnp.dot(onehot, before, preferred_element_type=F32) + carry
    ranks = [jnp.sum(jnp.where(sel[k], prefix, 0.0), axis=0, keepdims=True)
             for k in range(TOP_K)]
    rank_ref[...] = jnp.concatenate(ranks, axis=0).astype(jnp.int32)
    total = carry + jnp.sum(onehot.astype(F32), axis=1, keepdims=True)
    carry_ref[...] = jnp.broadcast_to(total, carry_ref.shape)
    cnt_ref[...] = jnp.broadcast_to(total, cnt_ref.shape).astype(jnp.int32)


def _rank(idx_t):
    T = idx_t.shape[1]
    tc = TC_RANK
    return pl.pallas_call(
        _rank_body,
        grid=(T // tc,),
        in_specs=[pl.BlockSpec((TOP_K, tc), lambda i: (0, i))],
        out_specs=[pl.BlockSpec((TOP_K, tc), lambda i: (0, i)),
                   pl.BlockSpec((N_EXPERTS, LANES), lambda i: (0, 0)),
                   pl.BlockSpec((N_EXPERTS, LANES), lambda i: (i, 0))],
        out_shape=[jax.ShapeDtypeStruct((TOP_K, T), jnp.int32),
                   jax.ShapeDtypeStruct((N_EXPERTS, LANES), jnp.int32),
                   jax.ShapeDtypeStruct((T // tc * N_EXPERTS, LANES), jnp.int32)],
        scratch_shapes=[pltpu.VMEM((N_EXPERTS, LANES), F32)],
        compiler_params=pltpu.CompilerParams(dimension_semantics=("arbitrary",)),
        name="rank",
    )(idx_t)


def _invperm_body(dest_ref, fill_ref, end_ref, src_ref, *, n_assign, n_slots):
    def mark_tile(start):
        def mark(q, carry):
            p = start + q
            src_ref[p] = n_assign + (p & (2 * TM_EXPERT - 1))
            return carry
        lax.fori_loop(0, TM_EXPERT, mark, 0, unroll=16)

    def mark_expert_tail(e, carry):
        mark_tile(fill_ref[e])
        return carry

    lax.fori_loop(0, N_EXPERTS, mark_expert_tail, 0)

    def mark_unused(t, carry):
        mark_tile(t * TM_EXPERT)
        return carry

    lax.fori_loop(end_ref[N_EXPERTS - 1] // TM_EXPERT, n_slots // TM_EXPERT + 1, mark_unused, 0)

    def place(a, carry):
        src_ref[dest_ref[a]] = a
        return carry

    lax.fori_loop(0, n_assign, place, 0, unroll=64)


def _invperm(dest_flat, fill_start, pad_end, n_slots):
    return pl.pallas_call(
        functools.partial(_invperm_body, n_assign=dest_flat.shape[0], n_slots=n_slots),
        grid_spec=pltpu.PrefetchScalarGridSpec(
            num_scalar_prefetch=3,
            grid=(1,),
            in_specs=[],
            out_specs=pl.BlockSpec(memory_space=pltpu.SMEM),
        ),
        out_shape=jax.ShapeDtypeStruct((n_slots + TM_EXPERT,), jnp.int32),
        compiler_params=pltpu.CompilerParams(dimension_semantics=("arbitrary",)),
        name="invperm",
    )(dest_flat, fill_start, pad_end)


def _expert_body(blk_e_ref, nxt_e_ref, n_used_ref, src_ref, hp_ref, wg_hbm, wu_hbm, wd_hbm,
                 bg_ref, bu_ref, bd_ref, y_ref, stage_bf, stage, xbuf, wst, w_bf, wsem, *, n_tok):
    tm = TM_EXPERT
    i = pl.program_id(0)
    n_used = n_used_ref[0]
    slot = i % 2
    e = blk_e_ref[i]
    e_prev = blk_e_ref[jnp.maximum(i - 1, 0)]

    def weight_copies(ex):
        return [pltpu.make_async_copy(w.at[ex], wst.at[m], wsem.at[m])
                for m, w in enumerate((wg_hbm, wu_hbm, wd_hbm))]

    def rows_of(buf):
        return pl.ds(pl.multiple_of(buf * tm, tm), tm)

    def gather_rows(tile, rows):
        for r in rows:
            a = src_ref[tile * tm + r]
            stage_bf[pl.ds(r * SLABS, SLABS), :] = hp_ref[a & (n_tok - 1)]

    def finish_gather(buf):
        stage[...] = stage_bf[...].astype(F32)
        xbuf[rows_of(buf), :] = _load_token_major(stage, tm).astype(BF16)

    @pl.when(i == 0)
    def _():
        for c in weight_copies(e):
            c.start()
        gather_rows(0, range(tm))
        finish_gather(0)

    @pl.when(i >= n_used)
    def _():
        y_ref[...] = jnp.zeros_like(y_ref)

    @pl.when(i < n_used)
    def _():
        @pl.when((i == 0) | (e != e_prev))
        def _():
            for c in weight_copies(e):
                c.wait()
            rows = D_MODEL // SUBLANES
            for m in range(3):
                def cast_rows(j, carry, m=m):
                    sl = pl.ds(pl.multiple_of(j * rows, rows), rows)
                    w_bf[m, sl, :] = wst[m, sl, :].astype(BF16)
                    return carry
                lax.fori_loop(0, SUBLANES, cast_rows, 0)
            nxt_e = nxt_e_ref[i]

            @pl.when(nxt_e >= 0)
            def _():
                for c in weight_copies(nxt_e):
                    c.start()

        x = xbuf[rows_of(slot), :]
        gather_rows(jnp.minimum(i + 1, n_used - 1), range(tm))
        finish_gather(1 - slot)
        g = jnp.dot(x, w_bf[0], preferred_element_type=F32) + bg_ref[pl.ds(e, 1), :]
        u = jnp.dot(x, w_bf[1], preferred_element_type=F32) + bu_ref[pl.ds(e, 1), :]
        g = jnp.minimum(g, SWIGLU_LIMIT)
        u = jnp.clip(u, -SWIGLU_LIMIT, SWIGLU_LIMIT)
        a = g * jax.nn.sigmoid(SWIGLU_ALPHA * g) * (u + 1.0)
        y = jnp.dot(a.astype(BF16), w_bf[2], preferred_element_type=F32) + bd_ref[pl.ds(e, 1), :]
        _store_token_major(y_ref, y)


def _experts(blk_e, nxt_e, n_used, src, hp, w_gate, b_gate, w_up, b_up, w_down, b_down, n_tiles):
    tm = TM_EXPERT
    d_ff = w_gate.shape[2]
    assert d_ff == D_MODEL
    n_tok = hp.shape[0]
    assert n_tok & (n_tok - 1) == 0, "assignment ids are split with a power-of-two mask"
    assert n_tiles >= 2
    const2 = lambda i, *_: (0, 0)
    bias = pl.BlockSpec((N_EXPERTS, D_MODEL), const2)
    hbm = pl.BlockSpec(memory_space=pl.ANY)
    return pl.pallas_call(
        functools.partial(_expert_body, n_tok=n_tok),
        grid_spec=pltpu.PrefetchScalarGridSpec(
            num_scalar_prefetch=4,
            grid=(n_tiles,),
            in_specs=[
                pl.BlockSpec((n_tok, SLABS, LANES), lambda i, *_: (0, 0, 0),
                             pipeline_mode=pl.Buffered(1)),
                hbm, hbm, hbm, bias, bias, bias,
            ],
            out_specs=pl.BlockSpec((tm * SLABS, LANES), lambda i, *_: (i, 0)),
            scratch_shapes=[pltpu.VMEM((tm * SLABS, LANES), BF16),
                            pltpu.VMEM((tm * SLABS, LANES), F32),
                            pltpu.VMEM((2 * tm, D_MODEL), BF16),
                            pltpu.VMEM((3, D_MODEL, D_MODEL), F32),
                            pltpu.VMEM((3, D_MODEL, D_MODEL), BF16),
                            pltpu.SemaphoreType.DMA((3,))],
        ),
        out_shape=jax.ShapeDtypeStruct((n_tiles * tm * SLABS, LANES), F32),
        compiler_params=pltpu.CompilerParams(
            dimension_semantics=("arbitrary",), vmem_limit_bytes=EXPERT_VMEM_LIMIT_BYTES),
        name="experts",
    )(blk_e, nxt_e, n_used, src, hp, w_gate, w_up, w_down, b_gate, b_up, b_down)


def _combine_body(piece_src_ref, n_pieces_ref, srow_ref, wgt_ref, y_hbm, x1_ref, g_ref, o_ref,
                  stage, tokmaj, sem, *, n_tok):
    tm = x1_ref.shape[0]
    b = pl.program_id(0)
    slot = b % 2
    piece = PIECE_ROWS * SLABS
    max_pieces = stage.shape[1] // piece

    def piece_copy(blk, buf, j):
        src = pl.multiple_of(piece_src_ref[blk * max_pieces + j] * SLABS, SLABS)
        dst = pl.multiple_of(j * piece, piece)
        return pltpu.make_async_copy(y_hbm.at[pl.ds(src, piece)], stage.at[buf, pl.ds(dst, piece)],
                                     sem.at[buf])

    def start_pieces(blk, buf):
        def body(j, carry):
            piece_copy(blk, buf, j).start()
            return carry
        lax.fori_loop(0, n_pieces_ref[blk], body, 0)

    def wait_pieces(blk, buf):
        def body(j, carry):
            piece_copy(blk, buf, j).wait()
            return carry
        lax.fori_loop(0, n_pieces_ref[blk], body, 0)

    @pl.when(b == 0)
    def _():
        start_pieces(0, 0)

    @pl.when(b + 1 < pl.num_programs(0))
    def _():
        start_pieces(b + 1, 1 - slot)

    wait_pieces(b, slot)

    def token(t, carry):
        acc = None
        for k in range(TOP_K):
            a = k * n_tok + b * tm + t
            row = pl.multiple_of(srow_ref[a] * SLABS, SLABS)
            term = wgt_ref[a] * stage[slot, pl.ds(row, SLABS), :]
            acc = term if acc is None else acc + term
        tokmaj[pl.ds(pl.multiple_of(t * SLABS, SLABS), SLABS), :] = acc
        return carry

    lax.fori_loop(0, tm, token, 0, unroll=8)
    o_ref[...] = _rms_scale(x1_ref[...] + _load_token_major(tokmaj, tm), g_ref[...])


def _combine(piece_src, n_pieces, srow, wgt, y_sorted, x1, g_final):
    T = x1.shape[0]
    tm = TM_MOVE
    max_pieces = piece_src.shape[0] // (T // tm)
    row_blk = lambda i, *_: (i, 0)
    return pl.pallas_call(
        functools.partial(_combine_body, n_tok=T),
        grid_spec=pltpu.PrefetchScalarGridSpec(
            num_scalar_prefetch=4,
            grid=(T // tm,),
            in_specs=[
                pl.BlockSpec(memory_space=pl.ANY),
                pl.BlockSpec((tm, D_MODEL), row_blk),
                pl.BlockSpec((1, D_MODEL), lambda i, *_: (0, 0)),
            ],
            out_specs=pl.BlockSpec((tm, D_MODEL), row_blk),
            scratch_shapes=[pltpu.VMEM((2, max_pieces * PIECE_ROWS * SLABS, LANES), F32),
                            pltpu.VMEM((tm * SLABS, LANES), F32),
                            pltpu.SemaphoreType.DMA((2,))],
        ),
        out_shape=jax.ShapeDtypeStruct((T, D_MODEL), F32),
        compiler_params=pltpu.CompilerParams(
            dimension_semantics=("arbitrary",), vmem_limit_bytes=VMEM_LIMIT_BYTES),
        name="combine",
    )(piece_src, n_pieces, srow, wgt, y_sorted, x1, g_final)


def _rope_tables(seq_len):
    half = HEAD_DIM // 2
    inv_freq = ROPE_THETA ** (-np.arange(half, dtype=np.float64) / half)
    ang = np.arange(seq_len, dtype=np.float64)[:, None] * inv_freq[None, :]
    cos = np.cos(ang).astype(np.float32)
    sin = np.sin(ang).astype(np.float32)
    reps = LANES // HEAD_DIM
    cos_t = np.tile(np.concatenate([cos, cos], axis=-1), (1, reps))
    sin_t = np.tile(np.concatenate([-sin, sin], axis=-1), (1, reps))
    return jnp.asarray(cos_t), jnp.asarray(sin_t)


def _router_hi_lo(w_router):
    hi = w_router.astype(BF16)
    lo = (w_router - hi.astype(F32)).astype(BF16)
    pad = jnp.zeros((w_router.shape[0], LANES - 2 * N_EXPERTS), BF16)
    return jnp.concatenate([hi, lo, pad], axis=1)


def _layer(x2, batch, seq_len, g_mix, w_in, b_in, sinks, w_conv, w_attn_o, w_conv_o, w_out,
           g_ffn, w_router, b_router, w_gate, b_gate, w_up, b_up, w_down, b_down, g_out):
    T = x2.shape[0]
    cos_t, sin_t = _rope_tables(seq_len)
    q, k, v, z, sa, sc = _inproj(x2, g_mix[None, :], w_in.astype(BF16), b_in[None, :],
                                 cos_t, sin_t, w_conv, seq_len)
    attn = _attention(q, k, v, sinks, batch, seq_len)
    x1, hp, idx_t, wgt_t = _mixout(
        x2, attn, z, sa, sc, w_attn_o.astype(BF16), w_conv_o.astype(BF16), w_out.astype(BF16),
        g_ffn[None, :], _router_hi_lo(w_router), b_router[:, None])

    rank_t, cnt, before = _rank(idx_t)
    counts = cnt[:, 0]
    padded = (counts + TM_EXPERT - 1) // TM_EXPERT * TM_EXPERT
    pad_end = jnp.cumsum(padded)
    pad_start = pad_end - padded
    n_tiles = (T * TOP_K) // TM_EXPERT + N_EXPERTS + 1
    tile_row = jnp.arange(n_tiles, dtype=jnp.int32) * TM_EXPERT
    blk_e = jnp.minimum(jnp.sum(pad_end[None, :] <= tile_row[:, None], axis=1),
                        N_EXPERTS - 1).astype(jnp.int32)
    n_used = (pad_end[-1:] // TM_EXPERT).astype(jnp.int32)
    e_ids = jnp.arange(N_EXPERTS, dtype=jnp.int32)
    dest = rank_t + jnp.sum(
        jnp.where(idx_t[:, :, None] == e_ids[None, None, :], pad_start[None, None, :], 0), axis=-1)
    dest_flat = dest.reshape(-1).astype(jnp.int32)
    fill_start = (pad_start + counts).astype(jnp.int32)

    group_end = jnp.sum(jnp.where(blk_e[:, None] == e_ids[None, :], pad_end[None, :], 0), axis=1)
    nxt_e = jnp.where(
        group_end < pad_end[-1],
        jnp.minimum(jnp.sum(pad_end[None, :] <= group_end[:, None], axis=1), N_EXPERTS - 1),
        -1).astype(jnp.int32)

    src = _invperm(dest_flat, fill_start, pad_end.astype(jnp.int32), n_tiles * TM_EXPERT)
    y_sorted = _experts(blk_e, nxt_e, n_used, src, hp.reshape(T, SLABS, LANES),
                        w_gate, b_gate, w_up, b_up, w_down, b_down, n_tiles)

    n_blocks = T // TM_MOVE
    max_pieces = TOP_K * TM_MOVE // PIECE_ROWS + N_EXPERTS
    lo = before.reshape(n_blocks, N_EXPERTS, LANES)[:, :, 0]
    cnt_be = jnp.concatenate([lo[1:], counts[None, :]], axis=0) - lo
    pieces_be = (cnt_be + PIECE_ROWS - 1) // PIECE_ROWS
    pend = jnp.cumsum(pieces_be, axis=1)
    pbase = pend - pieces_be
    n_pieces = pend[:, -1].astype(jnp.int32)
    j = jnp.arange(max_pieces, dtype=jnp.int32)
    e_of_piece = jnp.minimum(jnp.sum(pend[:, None, :] <= j[None, :, None], axis=2), N_EXPERTS - 1)
    onehot = e_of_piece[:, :, None] == e_ids[None, None, :]
    pick = lambda tab: jnp.sum(jnp.where(onehot, tab[:, None, :], 0), axis=2)
    piece_src = pick(pad_start[None, :] + lo) + PIECE_ROWS * (j[None, :] - pick(pbase))
    piece_src = jnp.where(j[None, :] < n_pieces[:, None], piece_src, 0).astype(jnp.int32)
    tab = jnp.broadcast_to((PIECE_ROWS * pbase - lo)[:, None, :],
                           (n_blocks, TM_MOVE, N_EXPERTS)).reshape(T, N_EXPERTS)
    srow = rank_t + jnp.sum(jnp.where(idx_t[:, :, None] == e_ids[None, None, :], tab[None], 0), axis=-1)
    return _combine(piece_src.reshape(-1), n_pieces, srow.reshape(-1).astype(jnp.int32),
                    wgt_t.reshape(-1), y_sorted, x1, g_out[None, :])


def kernel(x, g_mix, w_in, b_in, sinks, w_conv, w_attn_o, w_conv_o, w_out, g_ffn, w_router,
           b_router, w_gate, b_gate, w_up, b_up, w_down, b_down, g_final):
    batch, seq_len, d = x.shape
    depth = g_mix.shape[0]
    assert depth == 1, "the final norm is fused into the single layer's combine step"
    x2 = x.reshape(batch * seq_len, d)
    out = _layer(x2, batch, seq_len, g_mix[0], w_in[0], b_in[0], sinks[0], w_conv[0],
                 w_attn_o[0], w_conv_o[0], w_out[0], g_ffn[0], w_router[0], b_router[0],
                 w_gate[0], b_gate[0], w_up[0], b_up[0], w_down[0], b_down[0], g_final)
    return out.reshape(batch, seq_len, d)
```

```python
import functools

import jax
import jax.numpy as jnp
import numpy as np
from jax import lax
from jax.experimental import pallas as pl
from jax.experimental.pallas import tpu as pltpu

D_MODEL = 1024
HEAD_DIM = 64
N_Q_HEADS = 16
N_KV_HEADS = 4
Q_PER_KV = N_Q_HEADS // N_KV_HEADS
ATTN_WIDTH = N_Q_HEADS * HEAD_DIM
KV_WIDTH = N_KV_HEADS * HEAD_DIM
ATTN_BLOCK = 128
WINDOW = 128
ROPE_THETA = 10000.0
CONV_WIDTH = D_MODEL
CONV_KERNEL = 3
N_EXPERTS = 32
TOP_K = 4
SWIGLU_LIMIT = 7.0
SWIGLU_ALPHA = 1.702
RMS_EPS = 1e-5

OFF_Q = 0
OFF_K = OFF_Q + ATTN_WIDTH
OFF_V = OFF_K + KV_WIDTH
OFF_CB = OFF_V + KV_WIDTH
OFF_CC = OFF_CB + CONV_WIDTH
OFF_CX = OFF_CC + CONV_WIDTH
OFF_GA = OFF_CX + CONV_WIDTH
OFF_GC = OFF_GA + D_MODEL
IN_WIDTH = OFF_GC + D_MODEL

LANES = 128
SUBLANES = 8
VMEM_LIMIT_BYTES = 56 * 1024 * 1024
EXPERT_VMEM_LIMIT_BYTES = 60 * 1024 * 1024

TM_INPROJ = 512
COL_CHUNK = 512
Q_BLOCKS_PER_STEP = 8
TM_MIX = 512
TC_RANK = 512
TM_EXPERT = 256
TM_MOVE = 512
PIECE_ROWS = 64
assert TM_MOVE == TC_RANK

BF16 = jnp.bfloat16
F32 = jnp.float32
NEG_BIG = -1e30


def _rms_scale(x, g):
    ms = jnp.mean(x * x, axis=-1, keepdims=True)
    return (x * lax.rsqrt(ms + RMS_EPS)) * g


SLABS = D_MODEL // LANES
assert SLABS == SUBLANES, "a token slab must be exactly one (8, 128) tile of 32-bit words"


def _store_token_major(ref, val):
    tm = val.shape[0]
    for s in range(SLABS):
        ref[pl.ds(s, tm, stride=SLABS), :] = val[:, s * LANES:(s + 1) * LANES]


def _load_token_major(ref, tm):
    return jnp.concatenate(
        [ref[pl.ds(s, tm, stride=SLABS), :] for s in range(SLABS)], axis=1)


def _inproj_body(x_ref, g_ref, w_ref, b_ref, cos_ref, sin_ref, wc_ref,
                 q_ref, k_ref, v_ref, z_ref, sa_ref, sc_ref, carry_ref, *, tiles_per_seq):
    tm = x_ref.shape[0]
    i = pl.program_id(0)
    h = _rms_scale(x_ref[...], g_ref[...]).astype(BF16)

    def proj(c0, width):
        return (jnp.dot(h, w_ref[:, c0:c0 + width], preferred_element_type=F32)
                + b_ref[:, c0:c0 + width])

    cos = cos_ref[...]
    sin = sin_ref[...]
    lane = lax.broadcasted_iota(jnp.int32, (tm, LANES), 1)
    first_half = (lane & (HEAD_DIM // 2)) == 0

    def rope(t):
        partner = jnp.where(first_half,
                            pltpu.roll(t, LANES - HEAD_DIM // 2, 1),
                            pltpu.roll(t, HEAD_DIM // 2, 1))
        return t * cos + partner * sin

    for c in range(0, ATTN_WIDTH, COL_CHUNK):
        acc = proj(OFF_Q + c, COL_CHUNK)
        for j in range(0, COL_CHUNK, LANES):
            q_ref[:, c + j:c + j + LANES] = (
                rope(acc[:, j:j + LANES]) * (HEAD_DIM ** -0.5)).astype(BF16)

    acc = proj(OFF_K, 2 * KV_WIDTH)
    for j in range(0, KV_WIDTH, LANES):
        k_ref[:, j:j + LANES] = rope(acc[:, j:j + LANES]).astype(BF16)
    v_ref[...] = acc[:, KV_WIDTH:].astype(BF16)

    seq_start = (i % tiles_per_seq) == 0
    head = 2 * SUBLANES
    row = lax.broadcasted_iota(jnp.int32, (head, COL_CHUNK), 0)
    for c in range(0, CONV_WIDTH, COL_CHUNK):
        u = proj(OFF_CC + c, COL_CHUNK) * proj(OFF_CX + c, COL_CHUNK)
        cb = proj(OFF_CB + c, COL_CHUNK)
        w0 = wc_ref[0:1, c:c + COL_CHUNK]
        w1 = wc_ref[1:2, c:c + COL_CHUNK]
        w2 = wc_ref[2:3, c:c + COL_CHUNK]
        y = w0 * pltpu.roll(u, 2, 0) + w1 * pltpu.roll(u, 1, 0) + w2 * u
        z_ref[:, c:c + COL_CHUNK] = (cb * y).astype(BF16)
        prev = jnp.where(seq_start, 0.0, carry_ref[:, c:c + COL_CHUNK])
        pad = jnp.zeros((SUBLANES, COL_CHUNK), F32)
        uh = u[0:head]
        u1 = jnp.where(row < 1, jnp.concatenate([pltpu.roll(prev, 1, 0), pad], 0),
                       pltpu.roll(uh, 1, 0))
        u2 = jnp.where(row < 2, jnp.concatenate([pltpu.roll(prev, 2, 0), pad], 0),
                       pltpu.roll(uh, 2, 0))
        yh = w0 * u2 + w1 * u1 + w2 * uh
        z_ref[0:head, c:c + COL_CHUNK] = (cb[0:head] * yh).astype(BF16)
        carry_ref[:, c:c + COL_CHUNK] = u[tm - SUBLANES:tm]

    for c in range(0, D_MODEL, COL_CHUNK):
        sa_ref[:, c:c + COL_CHUNK] = jax.nn.sigmoid(proj(OFF_GA + c, COL_CHUNK)).astype(BF16)
        sc_ref[:, c:c + COL_CHUNK] = jax.nn.sigmoid(proj(OFF_GC + c, COL_CHUNK)).astype(BF16)


def _inproj(x2, g_mix, w_in_bf, b_in, cos_t, sin_t, w_conv, seq_len):
    T = x2.shape[0]
    tm = TM_INPROJ
    tiles_per_seq = seq_len // tm
    const = lambda i: (0, 0)
    row_blk = lambda i: (i, 0)
    pos_blk = lambda i: (i % tiles_per_seq, 0)
    return pl.pallas_call(
        functools.partial(_inproj_body, tiles_per_seq=tiles_per_seq),
        grid=(T // tm,),
        in_specs=[
            pl.BlockSpec((tm, D_MODEL), row_blk),
            pl.BlockSpec((1, D_MODEL), const),
            pl.BlockSpec((D_MODEL, IN_WIDTH), const, pipeline_mode=pl.Buffered(1)),
            pl.BlockSpec((1, IN_WIDTH), const),
            pl.BlockSpec((tm, LANES), pos_blk),
            pl.BlockSpec((tm, LANES), pos_blk),
            pl.BlockSpec((CONV_KERNEL, CONV_WIDTH), const),
        ],
        out_specs=[
            pl.BlockSpec((tm, ATTN_WIDTH), row_blk),
            pl.BlockSpec((tm, KV_WIDTH), row_blk),
            pl.BlockSpec((tm, KV_WIDTH), row_blk),
            pl.BlockSpec((tm, CONV_WIDTH), row_blk),
            pl.BlockSpec((tm, D_MODEL), row_blk),
            pl.BlockSpec((tm, D_MODEL), row_blk),
        ],
        out_shape=[
            jax.ShapeDtypeStruct((T, ATTN_WIDTH), BF16),
            jax.ShapeDtypeStruct((T, KV_WIDTH), BF16),
            jax.ShapeDtypeStruct((T, KV_WIDTH), BF16),
            jax.ShapeDtypeStruct((T, CONV_WIDTH), BF16),
            jax.ShapeDtypeStruct((T, D_MODEL), BF16),
            jax.ShapeDtypeStruct((T, D_MODEL), BF16),
        ],
        scratch_shapes=[pltpu.VMEM((SUBLANES, CONV_WIDTH), F32)],
        compiler_params=pltpu.CompilerParams(
            dimension_semantics=("arbitrary",), vmem_limit_bytes=VMEM_LIMIT_BYTES),
        name="inproj",
    )(x2, g_mix, w_in_bf, b_in, cos_t, sin_t, w_conv)


def _attn_body(sinks_ref, q_ref, kp_ref, kc_ref, vp_ref, vc_ref, o_ref):
    n = pl.program_id(1)
    blk = ATTN_BLOCK
    cols = Q_PER_KV * blk
    j = lax.broadcasted_iota(jnp.int32, (blk, cols), 0)
    qi = lax.broadcasted_iota(jnp.int32, (blk, cols), 1) % blk
    tri = j > qi
    no_prev = tri & (n == 0)
    k3 = jnp.concatenate([kp_ref[...], kc_ref[...]], axis=0)
    v3 = jnp.concatenate([vp_ref[...], vc_ref[...]], axis=0)
    v3_t = v3.astype(F32).T.astype(BF16)
    for qb in range(Q_BLOCKS_PER_STEP):
        k_all = k3[qb * blk:(qb + 2) * blk]
        v_t = v3_t[:, qb * blk:(qb + 2) * blk]
        for kh in range(N_KV_HEADS):
            k_h = k_all[:, kh * HEAD_DIM:(kh + 1) * HEAD_DIM]
            vt_h = v_t[kh * HEAD_DIM:(kh + 1) * HEAD_DIM, :]
            heads = [kh * Q_PER_KV + g for g in range(Q_PER_KV)]
            q_g = jnp.concatenate(
                [q_ref[qb * blk:(qb + 1) * blk, hq * HEAD_DIM:(hq + 1) * HEAD_DIM]
                 for hq in heads], axis=0)
            sink = jnp.concatenate(
                [jnp.full((1, blk), sinks_ref[hq], F32) for hq in heads], axis=1)
            s = lax.dot_general(k_h, q_g, (((1,), (1,)), ((), ())), preferred_element_type=F32)
            fold = jnp.where(tri, s[:blk], s[blk:])
            if qb == 0:
                fold = jnp.where(no_prev, NEG_BIG, fold)
            m = jnp.maximum(jnp.max(fold, axis=0, keepdims=True), sink)
            p = jnp.exp(fold - m)
            denom = jnp.sum(p, axis=0, keepdims=True) + jnp.exp(sink - m)
            p2 = jnp.concatenate([jnp.where(tri, p, 0.0), jnp.where(tri, 0.0, p)], axis=0)
            o_t = jnp.dot(vt_h, p2.astype(BF16), preferred_element_type=F32) / denom
            for g in range(0, Q_PER_KV, 2):
                pair = jnp.concatenate(
                    [o_t[:, g * blk:(g + 1) * blk], o_t[:, (g + 1) * blk:(g + 2) * blk]], axis=0)
                c0 = heads[g] * HEAD_DIM
                o_ref[qb * blk:(qb + 1) * blk, c0:c0 + 2 * HEAD_DIM] = pair.T.astype(BF16)


def _attention(q, k, v, sinks, batch, seq_len):
    T = q.shape[0]
    assert WINDOW == ATTN_BLOCK, "the folded score tile needs window == block"
    rows = Q_BLOCKS_PER_STEP * ATTN_BLOCK
    steps = seq_len // rows
    cur = lambda b, n: (b * steps + n, 0)
    prev = lambda b, n: ((b * steps + n) * Q_BLOCKS_PER_STEP - jnp.minimum(n, 1), 0)
    return pl.pallas_call(
        _attn_body,
        grid=(batch, steps),
        in_specs=[
            pl.BlockSpec(memory_space=pltpu.SMEM),
            pl.BlockSpec((rows, ATTN_WIDTH), cur),
            pl.BlockSpec((ATTN_BLOCK, KV_WIDTH), prev),
            pl.BlockSpec((rows, KV_WIDTH), cur),
            pl.BlockSpec((ATTN_BLOCK, KV_WIDTH), prev),
            pl.BlockSpec((rows, KV_WIDTH), cur),
        ],
        out_specs=pl.BlockSpec((rows, ATTN_WIDTH), cur),
        out_shape=jax.ShapeDtypeStruct((T, ATTN_WIDTH), BF16),
        compiler_params=pltpu.CompilerParams(
            dimension_semantics=("arbitrary", "arbitrary"), vmem_limit_bytes=VMEM_LIMIT_BYTES),
        name="attn",
    )(sinks, q, k, k, v, v)


def _mixout_body(x_ref, a_ref, z_ref, sa_ref, sc_ref, wa_ref, wc_ref, wo_ref, g_ref,
                 wr_ref, br_ref, x1_ref, hp_ref, idx_ref, wgt_ref, slab_ref):
    tm = x_ref.shape[0]
    y_attn = jnp.dot(a_ref[...], wa_ref[...], preferred_element_type=F32)
    y_conv = jnp.dot(z_ref[...], wc_ref[...], preferred_element_type=F32)
    merged = sa_ref[...].astype(F32) * y_attn + sc_ref[...].astype(F32) * y_conv
    x1 = x_ref[...] + jnp.dot(merged.astype(BF16), wo_ref[...], preferred_element_type=F32)
    x1_ref[...] = x1
    h = _rms_scale(x1, g_ref[...])
    _store_token_major(slab_ref, h)
    hp_ref[...] = slab_ref[...].astype(BF16)
    h_hi = h.astype(BF16)
    h_lo = (h - h_hi.astype(F32)).astype(BF16)
    p_hi = jnp.dot(h_hi, wr_ref[...], preferred_element_type=F32)
    p_lo = jnp.dot(h_lo, wr_ref[...], preferred_element_type=F32)
    lg = p_hi + pltpu.roll(p_hi, LANES - N_EXPERTS, 1) + p_lo
    logits = lg.T[0:N_EXPERTS, :] + br_ref[...]
    e_iota = lax.broadcasted_iota(jnp.int32, (N_EXPERTS, tm), 0)
    vals, idxs = [], []
    for _ in range(TOP_K):
        m = jnp.max(logits, axis=0, keepdims=True)
        idx = jnp.min(jnp.where(logits == m, e_iota, N_EXPERTS), axis=0, keepdims=True)
        vals.append(m)
        idxs.append(idx)
        logits = jnp.where(e_iota == idx, -jnp.inf, logits)
    ex = [jnp.exp(v - vals[0]) for v in vals]
    tot = ex[0] + ex[1] + ex[2] + ex[3]
    idx_ref[...] = jnp.concatenate(idxs, axis=0)
    wgt_ref[...] = jnp.concatenate([e / tot for e in ex], axis=0)


def _mixout(x2, attn, z, sa, sc, wa_bf, wc_bf, wo_bf, g_ffn, wr_t, b_router):
    T = x2.shape[0]
    tm = TM_MIX
    const = lambda i: (0, 0)
    row_blk = lambda i: (i, 0)
    col_blk = lambda i: (0, i)
    act = pl.BlockSpec((tm, D_MODEL), row_blk)
    wsq = pl.BlockSpec((D_MODEL, D_MODEL), const)
    return pl.pallas_call(
        _mixout_body,
        grid=(T // tm,),
        in_specs=[act, act, act, act, act, wsq, wsq, wsq,
                  pl.BlockSpec((1, D_MODEL), const),
                  pl.BlockSpec((D_MODEL, LANES), const),
                  pl.BlockSpec((N_EXPERTS, 1), const)],
        out_specs=[act,
                   pl.BlockSpec((tm * SLABS, LANES), row_blk),
                   pl.BlockSpec((TOP_K, tm), col_blk),
                   pl.BlockSpec((TOP_K, tm), col_blk)],
        out_shape=[jax.ShapeDtypeStruct((T, D_MODEL), F32),
                   jax.ShapeDtypeStruct((T * SLABS, LANES), BF16),
                   jax.ShapeDtypeStruct((TOP_K, T), jnp.int32),
                   jax.ShapeDtypeStruct((TOP_K, T), F32)],
        scratch_shapes=[pltpu.VMEM((tm * SLABS, LANES), F32)],
        compiler_params=pltpu.CompilerParams(
            dimension_semantics=("arbitrary",), vmem_limit_bytes=VMEM_LIMIT_BYTES),
        name="mixout",
    )(x2, attn, z, sa, sc, wa_bf, wc_bf, wo_bf, g_ffn, wr_t, b_router)


def _rank_body(idx_ref, rank_ref, cnt_ref, before_ref, carry_ref):
    tc = idx_ref.shape[1]
    i = pl.program_id(0)

    @pl.when(i == 0)
    def _():
        carry_ref[...] = jnp.zeros_like(carry_ref)

    idx = idx_ref[...]
    e_iota = lax.broadcasted_iota(jnp.int32, (N_EXPERTS, tc), 0)
    sel = [e_iota == idx[k:k + 1, :] for k in range(TOP_K)]
    member = (sel[0] | sel[1] | sel[2] | sel[3])
    onehot = jnp.where(member, 1.0, 0.0).astype(BF16)
    r = lax.broadcasted_iota(jnp.int32, (tc, tc), 0)
    c = lax.broadcasted_iota(jnp.int32, (tc, tc), 1)
    before = jnp.where(r < c, 1.0, 0.0).astype(BF16)
    carry = carry_ref[:, 0:1]
    before_ref[...] = carry_ref[...].astype(jnp.int32)
    prefix = jnp.dot(onehot, before, preferred_element_type=F32) + carry
    ranks = [jnp.sum(jnp.where(sel[k], prefix, 0.0), axis=0, keepdims=True)
             for k in range(TOP_K)]
    rank_ref[...] = jnp.concatenate(ranks, axis=0).astype(jnp.int32)
    total = carry + jnp.sum(onehot.astype(F32), axis=1, keepdims=True)
    carry_ref[...] = jnp.broadcast_to(total, carry_ref.shape)
    cnt_ref[...] = jnp.broadcast_to(total, cnt_ref.shape).astype(jnp.int32)


def _rank(idx_t):
    T = idx_t.shape[1]
    tc = TC_RANK
    return pl.pallas_call(
        _rank_body,
        grid=(T // tc,),
        in_specs=[pl.BlockSpec((TOP_K, tc), lambda i: (0, i))],
        out_specs=[pl.BlockSpec((TOP_K, tc), lambda i: (0, i)),
                   pl.BlockSpec((N_EXPERTS, LANES), lambda i: (0, 0)),
                   pl.BlockSpec((N_EXPERTS, LANES), lambda i: (i, 0))],
        out_shape=[jax.ShapeDtypeStruct((TOP_K, T), jnp.int32),
                   jax.ShapeDtypeStruct((N_EXPERTS, LANES), jnp.int32),
                   jax.ShapeDtypeStruct((T // tc * N_EXPERTS, LANES), jnp.int32)],
        scratch_shapes=[pltpu.VMEM((N_EXPERTS, LANES), F32)],
        compiler_params=pltpu.CompilerParams(dimension_semantics=("arbitrary",)),
        name="rank",
    )(idx_t)


def _invperm_body(dest_ref, fill_ref, end_ref, src_ref, *, n_assign, n_slots):
    def mark_tile(start):
        def mark(q, carry):
            p = start + q
            src_ref[p] = n_assign + (p & (2 * TM_EXPERT - 1))
            return carry
        lax.fori_loop(0, TM_EXPERT, mark, 0, unroll=16)

    def mark_expert_tail(e, carry):
        mark_tile(fill_ref[e])
        return carry

    lax.fori_loop(0, N_EXPERTS, mark_expert_tail, 0)

    def mark_unused(t, carry):
        mark_tile(t * TM_EXPERT)
        return carry

    lax.fori_loop(end_ref[N_EXPERTS - 1] // TM_EXPERT, n_slots // TM_EXPERT + 1, mark_unused, 0)

    def place(a, carry):
        src_ref[dest_ref[a]] = a
        return carry

    lax.fori_loop(0, n_assign, place, 0, unroll=64)


def _invperm(dest_flat, fill_start, pad_end, n_slots):
    return pl.pallas_call(
        functools.partial(_invperm_body, n_assign=dest_flat.shape[0], n_slots=n_slots),
        grid_spec=pltpu.PrefetchScalarGridSpec(
            num_scalar_prefetch=3,
            grid=(1,),
            in_specs=[],
            out_specs=pl.BlockSpec(memory_space=pltpu.SMEM),
        ),
        out_shape=jax.ShapeDtypeStruct((n_slots + TM_EXPERT,), jnp.int32),
        compiler_params=pltpu.CompilerParams(dimension_semantics=("arbitrary",)),
        name="invperm",
    )(dest_flat, fill_start, pad_end)


def _expert_body(blk_e_ref, nxt_e_ref, n_used_ref, src_ref, hp_ref, wg_hbm, wu_hbm, wd_hbm,
                 bg_ref, bu_ref, bd_ref, y_ref, stage_bf, stage, xbuf, wst, w_bf, wsem, *, n_tok):
    tm = TM_EXPERT
    i = pl.program_id(0)
    n_used = n_used_ref[0]
    slot = i % 2
    e = blk_e_ref[i]
    e_prev = blk_e_ref[jnp.maximum(i - 1, 0)]

    def weight_copies(ex):
        return [pltpu.make_async_copy(w.at[ex], wst.at[m], wsem.at[m])
                for m, w in enumerate((wg_hbm, wu_hbm, wd_hbm))]

    def rows_of(buf):
        return pl.ds(pl.multiple_of(buf * tm, tm), tm)

    def gather_rows(tile, rows):
        for r in rows:
            a = src_ref[tile * tm + r]
            stage_bf[pl.ds(r * SLABS, SLABS), :] = hp_ref[a & (n_tok - 1)]

    def finish_gather(buf):
        stage[...] = stage_bf[...].astype(F32)
        xbuf[rows_of(buf), :] = _load_token_major(stage, tm).astype(BF16)

    @pl.when(i == 0)
    def _():
        for c in weight_copies(e):
            c.start()
        gather_rows(0, range(tm))
        finish_gather(0)

    @pl.when(i >= n_used)
    def _():
        y_ref[...] = jnp.zeros_like(y_ref)

    @pl.when(i < n_used)
    def _():
        @pl.when((i == 0) | (e != e_prev))
        def _():
            for c in weight_copies(e):
                c.wait()
            rows = D_MODEL // SUBLANES
            for m in range(3):
                def cast_rows(j, carry, m=m):
                    sl = pl.ds(pl.multiple_of(j * rows, rows), rows)
                    w_bf[m, sl, :] = wst[m, sl, :].astype(BF16)
                    return carry
                lax.fori_loop(0, SUBLANES, cast_rows, 0)
            nxt_e = nxt_e_ref[i]

            @pl.when(nxt_e >= 0)
            def _():
                for c in weight_copies(nxt_e):
                    c.start()

        x = xbuf[rows_of(slot), :]
        gather_rows(jnp.minimum(i + 1, n_used - 1), range(tm))
        finish_gather(1 - slot)
        g = jnp.dot(x, w_bf[0], preferred_element_type=F32) + bg_ref[pl.ds(e, 1), :]
        u = jnp.dot(x, w_bf[1], preferred_element_type=F32) + bu_ref[pl.ds(e, 1), :]
        g = jnp.minimum(g, SWIGLU_LIMIT)
        u = jnp.clip(u, -SWIGLU_LIMIT, SWIGLU_LIMIT)
        a = g * jax.nn.sigmoid(SWIGLU_ALPHA * g) * (u + 1.0)
        y = jnp.dot(a.astype(BF16), w_bf[2], preferred_element_type=F32) + bd_ref[pl.ds(e, 1), :]
        _store_token_major(y_ref, y)


def _experts(blk_e, nxt_e, n_used, src, hp, w_gate, b_gate, w_up, b_up, w_down, b_down, n_tiles):
    tm = TM_EXPERT
    d_ff = w_gate.shape[2]
    assert d_ff == D_MODEL
    n_tok = hp.shape[0]
    assert n_tok & (n_tok - 1) == 0, "assignment ids are split with a power-of-two mask"
    assert n_tiles >= 2
    const2 = lambda i, *_: (0, 0)
    bias = pl.BlockSpec((N_EXPERTS, D_MODEL), const2)
    hbm = pl.BlockSpec(memory_space=pl.ANY)
    return pl.pallas_call(
        functools.partial(_expert_body, n_tok=n_tok),
        grid_spec=pltpu.PrefetchScalarGridSpec(
            num_scalar_prefetch=4,
            grid=(n_tiles,),
            in_specs=[
                pl.BlockSpec((n_tok, SLABS, LANES), lambda i, *_: (0, 0, 0),
                             pipeline_mode=pl.Buffered(1)),
                hbm, hbm, hbm, bias, bias, bias,
            ],
            out_specs=pl.BlockSpec((tm * SLABS, LANES), lambda i, *_: (i, 0)),
            scratch_shapes=[pltpu.VMEM((tm * SLABS, LANES), BF16),
                            pltpu.VMEM((tm * SLABS, LANES), F32),
                            pltpu.VMEM((2 * tm, D_MODEL), BF16),
                            pltpu.VMEM((3, D_MODEL, D_MODEL), F32),
                            pltpu.VMEM((3, D_MODEL, D_MODEL), BF16),
                            pltpu.SemaphoreType.DMA((3,))],
        ),
        out_shape=jax.ShapeDtypeStruct((n_tiles * tm * SLABS, LANES), F32),
        compiler_params=pltpu.CompilerParams(
            dimension_semantics=("arbitrary",), vmem_limit_bytes=EXPERT_VMEM_LIMIT_BYTES),
        name="experts",
    )(blk_e, nxt_e, n_used, src, hp, w_gate, w_up, w_down, b_gate, b_up, b_down)


def _combine_body(piece_src_ref, n_pieces_ref, srow_ref, wgt_ref, y_hbm, x1_ref, g_ref, o_ref,
                  stage, tokmaj, sem, *, n_tok):
    tm = x1_ref.shape[0]
    b = pl.program_id(0)
    slot = b % 2
    piece = PIECE_ROWS * SLABS
    max_pieces = stage.shape[1] // piece

    def piece_copy(blk, buf, j):
        src = pl.multiple_of(piece_src_ref[blk * max_pieces + j] * SLABS, SLABS)
        dst = pl.multiple_of(j * piece, piece)
        return pltpu.make_async_copy(y_hbm.at[pl.ds(src, piece)], stage.at[buf, pl.ds(dst, piece)],
                                     sem.at[buf])

    def start_pieces(blk, buf):
        def body(j, carry):
            piece_copy(blk, buf, j).start()
            return carry
        lax.fori_loop(0, n_pieces_ref[blk], body, 0)

    def wait_pieces(blk, buf):
        def body(j, carry):
            piece_copy(blk, buf, j).wait()
            return carry
        lax.fori_loop(0, n_pieces_ref[blk], body, 0)

    @pl.when(b == 0)
    def _():
        start_pieces(0, 0)

    @pl.when(b + 1 < pl.num_programs(0))
    def _():
        start_pieces(b + 1, 1 - slot)

    wait_pieces(b, slot)

    def token(t, carry):
        base = b * tm + t
        acc = None
        for k in range(TOP_K):
            row = pl.multiple_of(srow_ref[k * n_tok + base], SLABS)
            term = wgt_ref[k * n_tok + base] * stage[slot, pl.ds(row, SLABS), :]
            acc = term if acc is None else acc + term
        tokmaj[pl.ds(pl.multiple_of(t * SLABS, SLABS), SLABS), :] = acc
        return carry

    lax.fori_loop(0, tm, token, 0, unroll=32)
    o_ref[...] = _rms_scale(x1_ref[...] + _load_token_major(tokmaj, tm), g_ref[...])


def _combine(piece_src, n_pieces, srow, wgt, y_sorted, x1, g_final):
    T = x1.shape[0]
    tm = TM_MOVE
    max_pieces = piece_src.shape[0] // (T // tm)
    row_blk = lambda i, *_: (i, 0)
    return pl.pallas_call(
        functools.partial(_combine_body, n_tok=T),
        grid_spec=pltpu.PrefetchScalarGridSpec(
            num_scalar_prefetch=4,
            grid=(T // tm,),
            in_specs=[
                pl.BlockSpec(memory_space=pl.ANY),
                pl.BlockSpec((tm, D_MODEL), row_blk),
                pl.BlockSpec((1, D_MODEL), lambda i, *_: (0, 0)),
            ],
            out_specs=pl.BlockSpec((tm, D_MODEL), row_blk),
            scratch_shapes=[pltpu.VMEM((2, max_pieces * PIECE_ROWS * SLABS, LANES), F32),
                            pltpu.VMEM((tm * SLABS, LANES), F32),
                            pltpu.SemaphoreType.DMA((2,))],
        ),
        out_shape=jax.ShapeDtypeStruct((T, D_MODEL), F32),
        compiler_params=pltpu.CompilerParams(
            dimension_semantics=("arbitrary",), vmem_limit_bytes=VMEM_LIMIT_BYTES),
        name="combine",
    )(piece_src, n_pieces, srow, wgt, y_sorted, x1, g_final)


def _rope_tables(seq_len):
    half = HEAD_DIM // 2
    inv_freq = ROPE_THETA ** (-np.arange(half, dtype=np.float64) / half)
    ang = np.arange(seq_len, dtype=np.float64)[:, None] * inv_freq[None, :]
    cos = np.cos(ang).astype(np.float32)
    sin = np.sin(ang).astype(np.float32)
    reps = LANES // HEAD_DIM
    cos_t = np.tile(np.concatenate([cos, cos], axis=-1), (1, reps))
    sin_t = np.tile(np.concatenate([-sin, sin], axis=-1), (1, reps))
    return jnp.asarray(cos_t), jnp.asarray(sin_t)


def _router_hi_lo(w_router):
    hi = w_router.astype(BF16)
    lo = (w_router - hi.astype(F32)).astype(BF16)
    pad = jnp.zeros((w_router.shape[0], LANES - 2 * N_EXPERTS), BF16)
    return jnp.concatenate([hi, lo, pad], axis=1)


def _layer(x2, batch, seq_len, g_mix, w_in, b_in, sinks, w_conv, w_attn_o, w_conv_o, w_out,
           g_ffn, w_router, b_router, w_gate, b_gate, w_up, b_up, w_down, b_down, g_out):
    T = x2.shape[0]
    cos_t, sin_t = _rope_tables(seq_len)
    q, k, v, z, sa, sc = _inproj(x2, g_mix[None, :], w_in.astype(BF16), b_in[None, :],
                                 cos_t, sin_t, w_conv, seq_len)
    attn = _attention(q, k, v, sinks, batch, seq_len)
    x1, hp, idx_t, wgt_t = _mixout(
        x2, attn, z, sa, sc, w_attn_o.astype(BF16), w_conv_o.astype(BF16), w_out.astype(BF16),
        g_ffn[None, :], _router_hi_lo(w_router), b_router[:, None])

    rank_t, cnt, before = _rank(idx_t)
    counts = cnt[:, 0]
    padded = (counts + TM_EXPERT - 1) // TM_EXPERT * TM_EXPERT
    pad_end = jnp.cumsum(padded)
    pad_start = pad_end - padded
    n_tiles = (T * TOP_K) // TM_EXPERT + N_EXPERTS + 1
    tile_row = jnp.arange(n_tiles, dtype=jnp.int32) * TM_EXPERT
    blk_e = jnp.minimum(jnp.sum(pad_end[None, :] <= tile_row[:, None], axis=1),
                        N_EXPERTS - 1).astype(jnp.int32)
    n_used = (pad_end[-1:] // TM_EXPERT).astype(jnp.int32)
    e_ids = jnp.arange(N_EXPERTS, dtype=jnp.int32)
    dest = rank_t + jnp.sum(
        jnp.where(idx_t[:, :, None] == e_ids[None, None, :], pad_start[None, None, :], 0), axis=-1)
    dest_flat = dest.reshape(-1).astype(jnp.int32)
    fill_start = (pad_start + counts).astype(jnp.int32)

    group_end = jnp.sum(jnp.where(blk_e[:, None] == e_ids[None, :], pad_end[None, :], 0), axis=1)
    nxt_e = jnp.where(
        group_end < pad_end[-1],
        jnp.minimum(jnp.sum(pad_end[None, :] <= group_end[:, None], axis=1), N_EXPERTS - 1),
        -1).astype(jnp.int32)

    src = _invperm(dest_flat, fill_start, pad_end.astype(jnp.int32), n_tiles * TM_EXPERT)
    y_sorted = _experts(blk_e, nxt_e, n_used, src, hp.reshape(T, SLABS, LANES),
                        w_gate, b_gate, w_up, b_up, w_down, b_down, n_tiles)

    n_blocks = T // TM_MOVE
    max_pieces = TOP_K * TM_MOVE // PIECE_ROWS + N_EXPERTS
    lo = before.reshape(n_blocks, N_EXPERTS, LANES)[:, :, 0]
    cnt_be = jnp.concatenate([lo[1:], counts[None, :]], axis=0) - lo
    pieces_be = (cnt_be + PIECE_ROWS - 1) // PIECE_ROWS
    pend = jnp.cumsum(pieces_be, axis=1)
    pbase = pend - pieces_be
    n_pieces = pend[:, -1].astype(jnp.int32)
    j = jnp.arange(max_pieces, dtype=jnp.int32)
    e_of_piece = jnp.minimum(jnp.sum(pend[:, None, :] <= j[None, :, None], axis=2), N_EXPERTS - 1)
    onehot = e_of_piece[:, :, None] == e_ids[None, None, :]
    pick = lambda tab: jnp.sum(jnp.where(onehot, tab[:, None, :], 0), axis=2)
    piece_src = pick(pad_start[None, :] + lo) + PIECE_ROWS * (j[None, :] - pick(pbase))
    piece_src = jnp.where(j[None, :] < n_pieces[:, None], piece_src, 0).astype(jnp.int32)
    tab = jnp.broadcast_to((PIECE_ROWS * pbase - lo)[:, None, :],
                           (n_blocks, TM_MOVE, N_EXPERTS)).reshape(T, N_EXPERTS)
    srow = rank_t + jnp.sum(jnp.where(idx_t[:, :, None] == e_ids[None, None, :], tab[None], 0), axis=-1)
    return _combine(piece_src.reshape(-1), n_pieces, (srow * SLABS).reshape(-1).astype(jnp.int32),
                    wgt_t.reshape(-1), y_sorted, x1, g_out[None, :])


def kernel(x, g_mix, w_in, b_in, sinks, w_conv, w_attn_o, w_conv_o, w_out, g_ffn, w_router,
           b_router, w_gate, b_gate, w_up, b_up, w_down, b_down, g_final):
    batch, seq_len, d = x.shape
    depth = g_mix.shape[0]
    assert depth == 1, "the final norm is fused into the single layer's combine step"
    x2 = x.reshape(batch * seq_len, d)
    out = _layer(x2, batch, seq_len, g_mix[0], w_in[0], b_in[0], sinks[0], w_conv[0],
                 w_attn_o[0], w_conv_o[0], w_out[0], g_ffn[0], w_router[0], b_router[0],
                 w_gate[0], b_gate[0], w_up[0], b_up[0], w_down[0], b_down[0], g_final)
    return out.reshape(batch, seq_len, d)
```

```python
import functools

import jax
import jax.numpy as jnp
import numpy as np
from jax import lax
from jax.experimental import pallas as pl
from jax.experimental.pallas import tpu as pltpu

D_MODEL = 1024
HEAD_DIM = 64
N_Q_HEADS = 16
N_KV_HEADS = 4
Q_PER_KV = N_Q_HEADS // N_KV_HEADS
ATTN_WIDTH = N_Q_HEADS * HEAD_DIM
KV_WIDTH = N_KV_HEADS * HEAD_DIM
ATTN_BLOCK = 128
WINDOW = 128
ROPE_THETA = 10000.0
CONV_WIDTH = D_MODEL
CONV_KERNEL = 3
N_EXPERTS = 32
TOP_K = 4
SWIGLU_LIMIT = 7.0
SWIGLU_ALPHA = 1.702
RMS_EPS = 1e-5

OFF_Q = 0
OFF_K = OFF_Q + ATTN_WIDTH
OFF_V = OFF_K + KV_WIDTH
OFF_CB = OFF_V + KV_WIDTH
OFF_CC = OFF_CB + CONV_WIDTH
OFF_CX = OFF_CC + CONV_WIDTH
OFF_GA = OFF_CX + CONV_WIDTH
OFF_GC = OFF_GA + D_MODEL
IN_WIDTH = OFF_GC + D_MODEL

LANES = 128
SUBLANES = 8
VMEM_LIMIT_BYTES = 56 * 1024 * 1024
EXPERT_VMEM_LIMIT_BYTES = 60 * 1024 * 1024

TM_INPROJ = 512
COL_CHUNK = 512
Q_BLOCKS_PER_STEP = 8
TM_MIX = 512
TC_RANK = 512
TM_EXPERT = 256
TM_MOVE = 512
PIECE_ROWS = 64
assert TM_MOVE == TC_RANK

BF16 = jnp.bfloat16
F32 = jnp.float32
NEG_BIG = -1e30


def _rms_scale(x, g):
    ms = jnp.mean(x * x, axis=-1, keepdims=True)
    return (x * lax.rsqrt(ms + RMS_EPS)) * g


SLABS = D_MODEL // LANES
assert SLABS == SUBLANES, "a token slab must be exactly one (8, 128) tile of 32-bit words"


def _store_token_major(ref, val):
    tm = val.shape[0]
    for s in range(SLABS):
        ref[pl.ds(s, tm, stride=SLABS), :] = val[:, s * LANES:(s + 1) * LANES]


def _load_token_major(ref, tm):
    return jnp.concatenate(
        [ref[pl.ds(s, tm, stride=SLABS), :] for s in range(SLABS)], axis=1)


def _inproj_body(x_ref, g_ref, w_ref, b_ref, cos_ref, sin_ref, wc_ref,
                 q_ref, k_ref, v_ref, z_ref, sa_ref, sc_ref, carry_ref, *, tiles_per_seq):
    tm = x_ref.shape[0]
    i = pl.program_id(0)
    h = _rms_scale(x_ref[...], g_ref[...]).astype(BF16)

    def proj(c0, width):
        return (jnp.dot(h, w_ref[:, c0:c0 + width], preferred_element_type=F32)
                + b_ref[:, c0:c0 + width])

    cos = cos_ref[...]
    sin = sin_ref[...]
    lane = lax.broadcasted_iota(jnp.int32, (tm, LANES), 1)
    first_half = (lane & (HEAD_DIM // 2)) == 0

    def rope(t):
        partner = jnp.where(first_half,
                            pltpu.roll(t, LANES - HEAD_DIM // 2, 1),
                            pltpu.roll(t, HEAD_DIM // 2, 1))
        return t * cos + partner * sin

    for c in range(0, ATTN_WIDTH, COL_CHUNK):
        acc = proj(OFF_Q + c, COL_CHUNK)
        for j in range(0, COL_CHUNK, LANES):
            q_ref[:, c + j:c + j + LANES] = (
                rope(acc[:, j:j + LANES]) * (HEAD_DIM ** -0.5)).astype(BF16)

    acc = proj(OFF_K, 2 * KV_WIDTH)
    for j in range(0, KV_WIDTH, LANES):
        k_ref[:, j:j + LANES] = rope(acc[:, j:j + LANES]).astype(BF16)
    v_ref[...] = acc[:, KV_WIDTH:].astype(BF16)

    seq_start = (i % tiles_per_seq) == 0
    head = 2 * SUBLANES
    row = lax.broadcasted_iota(jnp.int32, (head, COL_CHUNK), 0)
    for c in range(0, CONV_WIDTH, COL_CHUNK):
        u = proj(OFF_CC + c, COL_CHUNK) * proj(OFF_CX + c, COL_CHUNK)
        cb = proj(OFF_CB + c, COL_CHUNK)
        w0 = wc_ref[0:1, c:c + COL_CHUNK]
        w1 = wc_ref[1:2, c:c + COL_CHUNK]
        w2 = wc_ref[2:3, c:c + COL_CHUNK]
        y = w0 * pltpu.roll(u, 2, 0) + w1 * pltpu.roll(u, 1, 0) + w2 * u
        z_ref[:, c:c + COL_CHUNK] = (cb * y).astype(BF16)
        prev = jnp.where(seq_start, 0.0, carry_ref[:, c:c + COL_CHUNK])
        pad = jnp.zeros((SUBLANES, COL_CHUNK), F32)
        uh = u[0:head]
        u1 = jnp.where(row < 1, jnp.concatenate([pltpu.roll(prev, 1, 0), pad], 0),
                       pltpu.roll(uh, 1, 0))
        u2 = jnp.where(row < 2, jnp.concatenate([pltpu.roll(prev, 2, 0), pad], 0),
                       pltpu.roll(uh, 2, 0))
        yh = w0 * u2 + w1 * u1 + w2 * uh
        z_ref[0:head, c:c + COL_CHUNK] = (cb[0:head] * yh).astype(BF16)
        carry_ref[:, c:c + COL_CHUNK] = u[tm - SUBLANES:tm]

    for c in range(0, D_MODEL, COL_CHUNK):
        sa_ref[:, c:c + COL_CHUNK] = jax.nn.sigmoid(proj(OFF_GA + c, COL_CHUNK)).astype(BF16)
        sc_ref[:, c:c + COL_CHUNK] = jax.nn.sigmoid(proj(OFF_GC + c, COL_CHUNK)).astype(BF16)


def _inproj(x2, g_mix, w_in_bf, b_in, cos_t, sin_t, w_conv, seq_len):
    T = x2.shape[0]
    tm = TM_INPROJ
    tiles_per_seq = seq_len // tm
    const = lambda i: (0, 0)
    row_blk = lambda i: (i, 0)
    pos_blk = lambda i: (i % tiles_per_seq, 0)
    return pl.pallas_call(
        functools.partial(_inproj_body, tiles_per_seq=tiles_per_seq),
        grid=(T // tm,),
        in_specs=[
            pl.BlockSpec((tm, D_MODEL), row_blk),
            pl.BlockSpec((1, D_MODEL), const),
            pl.BlockSpec((D_MODEL, IN_WIDTH), const, pipeline_mode=pl.Buffered(1)),
            pl.BlockSpec((1, IN_WIDTH), const),
            pl.BlockSpec((tm, LANES), pos_blk),
            pl.BlockSpec((tm, LANES), pos_blk),
            pl.BlockSpec((CONV_KERNEL, CONV_WIDTH), const),
        ],
        out_specs=[
            pl.BlockSpec((tm, ATTN_WIDTH), row_blk),
            pl.BlockSpec((tm, KV_WIDTH), row_blk),
            pl.BlockSpec((tm, KV_WIDTH), row_blk),
            pl.BlockSpec((tm, CONV_WIDTH), row_blk),
            pl.BlockSpec((tm, D_MODEL), row_blk),
            pl.BlockSpec((tm, D_MODEL), row_blk),
        ],
        out_shape=[
            jax.ShapeDtypeStruct((T, ATTN_WIDTH), BF16),
            jax.ShapeDtypeStruct((T, KV_WIDTH), BF16),
            jax.ShapeDtypeStruct((T, KV_WIDTH), BF16),
            jax.ShapeDtypeStruct((T, CONV_WIDTH), BF16),
            jax.ShapeDtypeStruct((T, D_MODEL), BF16),
            jax.ShapeDtypeStruct((T, D_MODEL), BF16),
        ],
        scratch_shapes=[pltpu.VMEM((SUBLANES, CONV_WIDTH), F32)],
        compiler_params=pltpu.CompilerParams(
            dimension_semantics=("arbitrary",), vmem_limit_bytes=VMEM_LIMIT_BYTES),
        name="inproj",
    )(x2, g_mix, w_in_bf, b_in, cos_t, sin_t, w_conv)


def _attn_body(sinks_ref, q_ref, kp_ref, kc_ref, vp_ref, vc_ref, o_ref):
    n = pl.program_id(1)
    blk = ATTN_BLOCK
    cols = Q_PER_KV * blk
    j = lax.broadcasted_iota(jnp.int32, (blk, cols), 0)
    qi = lax.broadcasted_iota(jnp.int32, (blk, cols), 1) % blk
    tri = j > qi
    no_prev = tri & (n == 0)
    k3 = jnp.concatenate([kp_ref[...], kc_ref[...]], axis=0)
    v3 = jnp.concatenate([vp_ref[...], vc_ref[...]], axis=0)
    v3_t = v3.astype(F32).T.astype(BF16)
    for qb in range(Q_BLOCKS_PER_STEP):
        k_all = k3[qb * blk:(qb + 2) * blk]
        v_t = v3_t[:, qb * blk:(qb + 2) * blk]
        for kh in range(N_KV_HEADS):
            k_h = k_all[:, kh * HEAD_DIM:(kh + 1) * HEAD_DIM]
            vt_h = v_t[kh * HEAD_DIM:(kh + 1) * HEAD_DIM, :]
            heads = [kh * Q_PER_KV + g for g in range(Q_PER_KV)]
            q_g = jnp.concatenate(
                [q_ref[qb * blk:(qb + 1) * blk, hq * HEAD_DIM:(hq + 1) * HEAD_DIM]
                 for hq in heads], axis=0)
            sink = jnp.concatenate(
                [jnp.full((1, blk), sinks_ref[hq], F32) for hq in heads], axis=1)
            s = lax.dot_general(k_h, q_g, (((1,), (1,)), ((), ())), preferred_element_type=F32)
            fold = jnp.where(tri, s[:blk], s[blk:])
            if qb == 0:
                fold = jnp.where(no_prev, NEG_BIG, fold)
            m = jnp.maximum(jnp.max(fold, axis=0, keepdims=True), sink)
            p = jnp.exp(fold - m)
            denom = jnp.sum(p, axis=0, keepdims=True) + jnp.exp(sink - m)
            p2 = jnp.concatenate([jnp.where(tri, p, 0.0), jnp.where(tri, 0.0, p)], axis=0)
            o_t = jnp.dot(vt_h, p2.astype(BF16), preferred_element_type=F32) / denom
            for g in range(0, Q_PER_KV, 2):
                pair = jnp.concatenate(
                    [o_t[:, g * blk:(g + 1) * blk], o_t[:, (g + 1) * blk:(g + 2) * blk]], axis=0)
                c0 = heads[g] * HEAD_DIM
                o_ref[qb * blk:(qb + 1) * blk, c0:c0 + 2 * HEAD_DIM] = pair.T.astype(BF16)


def _attention(q, k, v, sinks, batch, seq_len):
    T = q.shape[0]
    assert WINDOW == ATTN_BLOCK, "the folded score tile needs window == block"
    rows = Q_BLOCKS_PER_STEP * ATTN_BLOCK
    steps = seq_len // rows
    cur = lambda b, n: (b * steps + n, 0)
    prev = lambda b, n: ((b * steps + n) * Q_BLOCKS_PER_STEP - jnp.minimum(n, 1), 0)
    return pl.pallas_call(
        _attn_body,
        grid=(batch, steps),
        in_specs=[
            pl.BlockSpec(memory_space=pltpu.SMEM),
            pl.BlockSpec((rows, ATTN_WIDTH), cur),
            pl.BlockSpec((ATTN_BLOCK, KV_WIDTH), prev),
            pl.BlockSpec((rows, KV_WIDTH), cur),
            pl.BlockSpec((ATTN_BLOCK, KV_WIDTH), prev),
            pl.BlockSpec((rows, KV_WIDTH), cur),
        ],
        out_specs=pl.BlockSpec((rows, ATTN_WIDTH), cur),
        out_shape=jax.ShapeDtypeStruct((T, ATTN_WIDTH), BF16),
        compiler_params=pltpu.CompilerParams(
            dimension_semantics=("arbitrary", "arbitrary"), vmem_limit_bytes=VMEM_LIMIT_BYTES),
        name="attn",
    )(sinks, q, k, k, v, v)


def _mixout_body(x_ref, a_ref, z_ref, sa_ref, sc_ref, wa_ref, wc_ref, wo_ref, g_ref,
                 wr_ref, br_ref, x1_ref, hp_ref, idx_ref, wgt_ref, slab_ref):
    tm = x_ref.shape[0]
    y_attn = jnp.dot(a_ref[...], wa_ref[...], preferred_element_type=F32)
    y_conv = jnp.dot(z_ref[...], wc_ref[...], preferred_element_type=F32)
    merged = sa_ref[...].astype(F32) * y_attn + sc_ref[...].astype(F32) * y_conv
    x1 = x_ref[...] + jnp.dot(merged.astype(BF16), wo_ref[...], preferred_element_type=F32)
    x1_ref[...] = x1
    h = _rms_scale(x1, g_ref[...])
    _store_token_major(slab_ref, h)
    hp_ref[...] = slab_ref[...].astype(BF16)
    h_hi = h.astype(BF16)
    h_lo = (h - h_hi.astype(F32)).astype(BF16)
    p_hi = jnp.dot(h_hi, wr_ref[...], preferred_element_type=F32)
    p_lo = jnp.dot(h_lo, wr_ref[...], preferred_element_type=F32)
    lg = p_hi + pltpu.roll(p_hi, LANES - N_EXPERTS, 1) + p_lo
    logits = lg.T[0:N_EXPERTS, :] + br_ref[...]
    e_iota = lax.broadcasted_iota(jnp.int32, (N_EXPERTS, tm), 0)
    vals, idxs = [], []
    for _ in range(TOP_K):
        m = jnp.max(logits, axis=0, keepdims=True)
        idx = jnp.min(jnp.where(logits == m, e_iota, N_EXPERTS), axis=0, keepdims=True)
        vals.append(m)
        idxs.append(idx)
        logits = jnp.where(e_iota == idx, -jnp.inf, logits)
    ex = [jnp.exp(v - vals[0]) for v in vals]
    tot = ex[0] + ex[1] + ex[2] + ex[3]
    idx_ref[...] = jnp.concatenate(idxs, axis=0)
    wgt_ref[...] = jnp.concatenate([e / tot for e in ex], axis=0)


def _mixout(x2, attn, z, sa, sc, wa_bf, wc_bf, wo_bf, g_ffn, wr_t, b_router):
    T = x2.shape[0]
    tm = TM_MIX
    const = lambda i: (0, 0)
    row_blk = lambda i: (i, 0)
    col_blk = lambda i: (0, i)
    act = pl.BlockSpec((tm, D_MODEL), row_blk)
    wsq = pl.BlockSpec((D_MODEL, D_MODEL), const)
    return pl.pallas_call(
        _mixout_body,
        grid=(T // tm,),
        in_specs=[act, act, act, act, act, wsq, wsq, wsq,
                  pl.BlockSpec((1, D_MODEL), const),
                  pl.BlockSpec((D_MODEL, LANES), const),
                  pl.BlockSpec((N_EXPERTS, 1), const)],
        out_specs=[act,
                   pl.BlockSpec((tm * SLABS, LANES), row_blk),
                   pl.BlockSpec((TOP_K, tm), col_blk),
                   pl.BlockSpec((TOP_K, tm), col_blk)],
        out_shape=[jax.ShapeDtypeStruct((T, D_MODEL), F32),
                   jax.ShapeDtypeStruct((T * SLABS, LANES), BF16),
                   jax.ShapeDtypeStruct((TOP_K, T), jnp.int32),
                   jax.ShapeDtypeStruct((TOP_K, T), F32)],
        scratch_shapes=[pltpu.VMEM((tm * SLABS, LANES), F32)],
        compiler_params=pltpu.CompilerParams(
            dimension_semantics=("arbitrary",), vmem_limit_bytes=VMEM_LIMIT_BYTES),
        name="mixout",
    )(x2, attn, z, sa, sc, wa_bf, wc_bf, wo_bf, g_ffn, wr_t, b_router)


def _rank_body(idx_ref, rank_ref, cnt_ref, before_ref, carry_ref):
    tc = idx_ref.shape[1]
    i = pl.program_id(0)

    @pl.when(i == 0)
    def _():
        carry_ref[...] = jnp.zeros_like(carry_ref)

    idx = idx_ref[...]
    e_iota = lax.broadcasted_iota(jnp.int32, (N_EXPERTS, tc), 0)
    sel = [e_iota == idx[k:k + 1, :] for k in range(TOP_K)]
    member = (sel[0] | sel[1] | sel[2] | sel[3])
    onehot = jnp.where(member, 1.0, 0.0).astype(BF16)
    r = lax.broadcasted_iota(jnp.int32, (tc, tc), 0)
    c = lax.broadcasted_iota(jnp.int32, (tc, tc), 1)
    before = jnp.where(r < c, 1.0, 0.0).astype(BF16)
    carry = carry_ref[:, 0:1]
    before_ref[...] = carry_ref[...].astype(jnp.int32)
    prefix = jnp.dot(onehot, before, preferred_element_type=F32) + carry
    ranks = [jnp.sum(jnp.where(sel[k], prefix, 0.0), axis=0, keepdims=True)
             for k in range(TOP_K)]
    rank_ref[...] = jnp.concatenate(ranks, axis=0).astype(jnp.int32)
    total = carry + jnp.sum(onehot.astype(F32), axis=1, keepdims=True)
    carry_ref[...] = jnp.broadcast_to(total, carry_ref.shape)
    cnt_ref[...] = jnp.broadcast_to(total, cnt_ref.shape).astype(jnp.int32)


def _rank(idx_t):
    T = idx_t.shape[1]
    tc = TC_RANK
    return pl.pallas_call(
        _rank_body,
        grid=(T // tc,),
        in_specs=[pl.BlockSpec((TOP_K, tc), lambda i: (0, i))],
        out_specs=[pl.BlockSpec((TOP_K, tc), lambda i: (0, i)),
                   pl.BlockSpec((N_EXPERTS, LANES), lambda i: (0, 0)),
                   pl.BlockSpec((N_EXPERTS, LANES), lambda i: (i, 0))],
        out_shape=[jax.ShapeDtypeStruct((TOP_K, T), jnp.int32),
                   jax.ShapeDtypeStruct((N_EXPERTS, LANES), jnp.int32),
                   jax.ShapeDtypeStruct((T // tc * N_EXPERTS, LANES), jnp.int32)],
        scratch_shapes=[pltpu.VMEM((N_EXPERTS, LANES), F32)],
        compiler_params=pltpu.CompilerParams(dimension_semantics=("arbitrary",)),
        name="rank",
    )(idx_t)


def _invperm_body(dest_ref, fill_ref, end_ref, src_ref, *, n_assign, n_slots):
    def mark_tile(start):
        def mark(q, carry):
            src_ref[start + q] = 0
            return carry
        lax.fori_loop(0, TM_EXPERT, mark, 0, unroll=16)

    def mark_expert_tail(e, carry):
        mark_tile(fill_ref[e])
        return carry

    lax.fori_loop(0, N_EXPERTS, mark_expert_tail, 0)

    def mark_unused(t, carry):
        mark_tile(t * TM_EXPERT)
        return carry

    lax.fori_loop(end_ref[N_EXPERTS - 1] // TM_EXPERT, n_slots // TM_EXPERT + 1, mark_unused, 0)

    def place(a, carry):
        src_ref[dest_ref[a]] = a
        return carry

    lax.fori_loop(0, n_assign, place, 0, unroll=64)


def _invperm(dest_flat, fill_start, pad_end, n_slots):
    return pl.pallas_call(
        functools.partial(_invperm_body, n_assign=dest_flat.shape[0], n_slots=n_slots),
        grid_spec=pltpu.PrefetchScalarGridSpec(
            num_scalar_prefetch=3,
            grid=(1,),
            in_specs=[],
            out_specs=pl.BlockSpec(memory_space=pltpu.SMEM),
        ),
        out_shape=jax.ShapeDtypeStruct((n_slots + TM_EXPERT,), jnp.int32),
        compiler_params=pltpu.CompilerParams(dimension_semantics=("arbitrary",)),
        name="invperm",
    )(dest_flat, fill_start, pad_end)


def _expert_body(blk_e_ref, nxt_e_ref, n_used_ref, src_ref, hp_ref, wg_hbm, wu_hbm, wd_hbm,
                 bg_ref, bu_ref, bd_ref, y_ref, stage_bf, stage, xbuf, wst, w_bf, wsem, *, n_tok):
    tm = TM_EXPERT
    i = pl.program_id(0)
    n_used = n_used_ref[0]
    slot = i % 2
    e = blk_e_ref[i]
    e_prev = blk_e_ref[jnp.maximum(i - 1, 0)]

    def weight_copies(ex):
        return [pltpu.make_async_copy(w.at[ex], wst.at[m], wsem.at[m])
                for m, w in enumerate((wg_hbm, wu_hbm, wd_hbm))]

    def rows_of(buf):
        return pl.ds(pl.multiple_of(buf * tm, tm), tm)

    def gather_rows(tile, rows):
        for r in rows:
            a = src_ref[tile * tm + r]
            stage_bf[pl.ds(r * SLABS, SLABS), :] = hp_ref[a & (n_tok - 1)]

    def finish_gather(buf):
        stage[...] = stage_bf[...].astype(F32)
        xbuf[rows_of(buf), :] = _load_token_major(stage, tm).astype(BF16)

    @pl.when(i == 0)
    def _():
        for c in weight_copies(e):
            c.start()
        gather_rows(0, range(tm))
        finish_gather(0)

    @pl.when(i >= n_used)
    def _():
        y_ref[...] = jnp.zeros_like(y_ref)

    @pl.when(i < n_used)
    def _():
        @pl.when((i == 0) | (e != e_prev))
        def _():
            for c in weight_copies(e):
                c.wait()
            rows = D_MODEL // SUBLANES
            for m in range(3):
                def cast_rows(j, carry, m=m):
                    sl = pl.ds(pl.multiple_of(j * rows, rows), rows)
                    w_bf[m, sl, :] = wst[m, sl, :].astype(BF16)
                    return carry
                lax.fori_loop(0, SUBLANES, cast_rows, 0)
            nxt_e = nxt_e_ref[i]

            @pl.when(nxt_e >= 0)
            def _():
                for c in weight_copies(nxt_e):
                    c.start()

        x = xbuf[rows_of(slot), :]
        gather_rows(jnp.minimum(i + 1, n_used - 1), range(tm))
        finish_gather(1 - slot)
        g = jnp.dot(x, w_bf[0], preferred_element_type=F32) + bg_ref[pl.ds(e, 1), :]
        u = jnp.dot(x, w_bf[1], preferred_element_type=F32) + bu_ref[pl.ds(e, 1), :]
        g = jnp.minimum(g, SWIGLU_LIMIT)
        u = jnp.clip(u, -SWIGLU_LIMIT, SWIGLU_LIMIT)
        a = g * jax.nn.sigmoid(SWIGLU_ALPHA * g) * (u + 1.0)
        y = jnp.dot(a.astype(BF16), w_bf[2], preferred_element_type=F32) + bd_ref[pl.ds(e, 1), :]
        _store_token_major(y_ref, y)


def _experts(blk_e, nxt_e, n_used, src, hp, w_gate, b_gate, w_up, b_up, w_down, b_down, n_tiles):
    tm = TM_EXPERT
    d_ff = w_gate.shape[2]
    assert d_ff == D_MODEL
    n_tok = hp.shape[0]
    assert n_tok & (n_tok - 1) == 0, "assignment ids are split with a power-of-two mask"
    assert n_tiles >= 2
    const2 = lambda i, *_: (0, 0)
    bias = pl.BlockSpec((N_EXPERTS, D_MODEL), const2)
    hbm = pl.BlockSpec(memory_space=pl.ANY)
    return pl.pallas_call(
        functools.partial(_expert_body, n_tok=n_tok),
        grid_spec=pltpu.PrefetchScalarGridSpec(
            num_scalar_prefetch=4,
            grid=(n_tiles,),
            in_specs=[
                pl.BlockSpec((n_tok, SLABS, LANES), lambda i, *_: (0, 0, 0),
                             pipeline_mode=pl.Buffered(1)),
                hbm, hbm, hbm, bias, bias, bias,
            ],
            out_specs=pl.BlockSpec((tm * SLABS, LANES), lambda i, *_: (i, 0)),
            scratch_shapes=[pltpu.VMEM((tm * SLABS, LANES), BF16),
                            pltpu.VMEM((tm * SLABS, LANES), F32),
                            pltpu.VMEM((2 * tm, D_MODEL), BF16),
                            pltpu.VMEM((3, D_MODEL, D_MODEL), F32),
                            pltpu.VMEM((3, D_MODEL, D_MODEL), BF16),
                            pltpu.SemaphoreType.DMA((3,))],
        ),
        out_shape=jax.ShapeDtypeStruct((n_tiles * tm * SLABS, LANES), F32),
        compiler_params=pltpu.CompilerParams(
            dimension_semantics=("arbitrary",), vmem_limit_bytes=EXPERT_VMEM_LIMIT_BYTES),
        name="experts",
    )(blk_e, nxt_e, n_used, src, hp, w_gate, w_up, w_down, b_gate, b_up, b_down)


def _combine_body(piece_src_ref, n_pieces_ref, srow_ref, wgt_ref, y_hbm, x1_ref, g_ref, o_ref,
                  stage, tokmaj, sem, *, n_tok):
    tm = x1_ref.shape[0]
    b = pl.program_id(0)
    slot = b % 2
    piece = PIECE_ROWS * SLABS
    max_pieces = stage.shape[1] // piece

    def piece_copy(blk, buf, j):
        src = pl.multiple_of(piece_src_ref[blk * max_pieces + j] * SLABS, SLABS)
        dst = pl.multiple_of(j * piece, piece)
        return pltpu.make_async_copy(y_hbm.at[pl.ds(src, piece)], stage.at[buf, pl.ds(dst, piece)],
                                     sem.at[buf])

    def start_pieces(blk, buf):
        def body(j, carry):
            piece_copy(blk, buf, j).start()
            return carry
        lax.fori_loop(0, n_pieces_ref[blk], body, 0)

    def wait_pieces(blk, buf):
        def body(j, carry):
            piece_copy(blk, buf, j).wait()
            return carry
        lax.fori_loop(0, n_pieces_ref[blk], body, 0)

    @pl.when(b == 0)
    def _():
        start_pieces(0, 0)

    @pl.when(b + 1 < pl.num_programs(0))
    def _():
        start_pieces(b + 1, 1 - slot)

    wait_pieces(b, slot)

    def token(t, carry):
        base = b * tm + t
        acc = None
        for k in range(TOP_K):
            row = pl.multiple_of(srow_ref[k * n_tok + base], SLABS)
            term = wgt_ref[k * n_tok + base] * stage[slot, pl.ds(row, SLABS), :]
            acc = term if acc is None else acc + term
        tokmaj[pl.ds(pl.multiple_of(t * SLABS, SLABS), SLABS), :] = acc
        return carry

    lax.fori_loop(0, tm, token, 0, unroll=32)
    o_ref[...] = _rms_scale(x1_ref[...] + _load_token_major(tokmaj, tm), g_ref[...])


def _combine(piece_src, n_pieces, srow, wgt, y_sorted, x1, g_final):
    T = x1.shape[0]
    tm = TM_MOVE
    max_pieces = piece_src.shape[0] // (T // tm)
    row_blk = lambda i, *_: (i, 0)
    return pl.pallas_call(
        functools.partial(_combine_body, n_tok=T),
        grid_spec=pltpu.PrefetchScalarGridSpec(
            num_scalar_prefetch=4,
            grid=(T // tm,),
            in_specs=[
                pl.BlockSpec(memory_space=pl.ANY),
                pl.BlockSpec((tm, D_MODEL), row_blk),
                pl.BlockSpec((1, D_MODEL), lambda i, *_: (0, 0)),
            ],
            out_specs=pl.BlockSpec((tm, D_MODEL), row_blk),
            scratch_shapes=[pltpu.VMEM((2, max_pieces * PIECE_ROWS * SLABS, LANES), F32),
                            pltpu.VMEM((tm * SLABS, LANES), F32),
                            pltpu.SemaphoreType.DMA((2,))],
        ),
        out_shape=jax.ShapeDtypeStruct((T, D_MODEL), F32),
        compiler_params=pltpu.CompilerParams(
            dimension_semantics=("arbitrary",), vmem_limit_bytes=VMEM_LIMIT_BYTES),
        name="combine",
    )(piece_src, n_pieces, srow, wgt, y_sorted, x1, g_final)


def _rope_tables(seq_len):
    half = HEAD_DIM // 2
    inv_freq = ROPE_THETA ** (-np.arange(half, dtype=np.float64) / half)
    ang = np.arange(seq_len, dtype=np.float64)[:, None] * inv_freq[None, :]
    cos = np.cos(ang).astype(np.float32)
    sin = np.sin(ang).astype(np.float32)
    reps = LANES // HEAD_DIM
    cos_t = np.tile(np.concatenate([cos, cos], axis=-1), (1, reps))
    sin_t = np.tile(np.concatenate([-sin, sin], axis=-1), (1, reps))
    return jnp.asarray(cos_t), jnp.asarray(sin_t)


def _router_hi_lo(w_router):
    hi = w_router.astype(BF16)
    lo = (w_router - hi.astype(F32)).astype(BF16)
    pad = jnp.zeros((w_router.shape[0], LANES - 2 * N_EXPERTS), BF16)
    return jnp.concatenate([hi, lo, pad], axis=1)


def _layer(x2, batch, seq_len, g_mix, w_in, b_in, sinks, w_conv, w_attn_o, w_conv_o, w_out,
           g_ffn, w_router, b_router, w_gate, b_gate, w_up, b_up, w_down, b_down, g_out):
    T = x2.shape[0]
    cos_t, sin_t = _rope_tables(seq_len)
    q, k, v, z, sa, sc = _inproj(x2, g_mix[None, :], w_in.astype(BF16), b_in[None, :],
                                 cos_t, sin_t, w_conv, seq_len)
    attn = _attention(q, k, v, sinks, batch, seq_len)
    x1, hp, idx_t, wgt_t = _mixout(
        x2, attn, z, sa, sc, w_attn_o.astype(BF16), w_conv_o.astype(BF16), w_out.astype(BF16),
        g_ffn[None, :], _router_hi_lo(w_router), b_router[:, None])

    rank_t, cnt, before = _rank(idx_t)
    counts = cnt[:, 0]
    padded = (counts + TM_EXPERT - 1) // TM_EXPERT * TM_EXPERT
    pad_end = jnp.cumsum(padded)
    pad_start = pad_end - padded
    n_tiles = (T * TOP_K) // TM_EXPERT + N_EXPERTS + 1
    tile_row = jnp.arange(n_tiles, dtype=jnp.int32) * TM_EXPERT
    blk_e = jnp.minimum(jnp.sum(pad_end[None, :] <= tile_row[:, None], axis=1),
                        N_EXPERTS - 1).astype(jnp.int32)
    n_used = (pad_end[-1:] // TM_EXPERT).astype(jnp.int32)
    e_ids = jnp.arange(N_EXPERTS, dtype=jnp.int32)
    dest = rank_t + jnp.sum(
        jnp.where(idx_t[:, :, None] == e_ids[None, None, :], pad_start[None, None, :], 0), axis=-1)
    dest_flat = dest.reshape(-1).astype(jnp.int32)
    fill_start = (pad_start + counts).astype(jnp.int32)

    group_end = jnp.sum(jnp.where(blk_e[:, None] == e_ids[None, :], pad_end[None, :], 0), axis=1)
    nxt_e = jnp.where(
        group_end < pad_end[-1],
        jnp.minimum(jnp.sum(pad_end[None, :] <= group_end[:, None], axis=1), N_EXPERTS - 1),
        -1).astype(jnp.int32)

    src = _invperm(dest_flat, fill_start, pad_end.astype(jnp.int32), n_tiles * TM_EXPERT)
    y_sorted = _experts(blk_e, nxt_e, n_used, src, hp.reshape(T, SLABS, LANES),
                        w_gate, b_gate, w_up, b_up, w_down, b_down, n_tiles)

    n_blocks = T // TM_MOVE
    max_pieces = TOP_K * TM_MOVE // PIECE_ROWS + N_EXPERTS
    lo = before.reshape(n_blocks, N_EXPERTS, LANES)[:, :, 0]
    cnt_be = jnp.concatenate([lo[1:], counts[None, :]], axis=0) - lo
    pieces_be = (cnt_be + PIECE_ROWS - 1) // PIECE_ROWS
    pend = jnp.cumsum(pieces_be, axis=1)
    pbase = pend - pieces_be
    n_pieces = pend[:, -1].astype(jnp.int32)
    j = jnp.arange(max_pieces, dtype=jnp.int32)
    e_of_piece = jnp.minimum(jnp.sum(pend[:, None, :] <= j[None, :, None], axis=2), N_EXPERTS - 1)
    onehot = e_of_piece[:, :, None] == e_ids[None, None, :]
    pick = lambda tab: jnp.sum(jnp.where(onehot, tab[:, None, :], 0), axis=2)
    piece_src = pick(pad_start[None, :] + lo) + PIECE_ROWS * (j[None, :] - pick(pbase))
    piece_src = jnp.where(j[None, :] < n_pieces[:, None], piece_src, 0).astype(jnp.int32)
    tab = jnp.broadcast_to((PIECE_ROWS * pbase - lo)[:, None, :],
                           (n_blocks, TM_MOVE, N_EXPERTS)).reshape(T, N_EXPERTS)
    srow = rank_t + jnp.sum(jnp.where(idx_t[:, :, None] == e_ids[None, None, :], tab[None], 0), axis=-1)
    return _combine(piece_src.reshape(-1), n_pieces, (srow * SLABS).reshape(-1).astype(jnp.int32),
                    wgt_t.reshape(-1), y_sorted, x1, g_out[None, :])


def kernel(x, g_mix, w_in, b_in, sinks, w_conv, w_attn_o, w_conv_o, w_out, g_ffn, w_router,
           b_router, w_gate, b_gate, w_up, b_up, w_down, b_down, g_final):
    batch, seq_len, d = x.shape
    depth = g_mix.shape[0]
    assert depth == 1, "the final norm is fused into the single layer's combine step"
    x2 = x.reshape(batch * seq_len, d)
    out = _layer(x2, batch, seq_len, g_mix[0], w_in[0], b_in[0], sinks[0], w_conv[0],
                 w_attn_o[0], w_conv_o[0], w_out[0], g_ffn[0], w_router[0], b_router[0],
                 w_gate[0], b_gate[0], w_up[0], b_up[0], w_down[0], b_down[0], g_final)
    return out.reshape(batch, seq_len, d)
```

```python
import functools

import jax
import jax.numpy as jnp
import numpy as np
from jax import lax
from jax.experimental import pallas as pl
from jax.experimental.pallas import tpu as pltpu

D_MODEL = 1024
HEAD_DIM = 64
N_Q_HEADS = 16
N_KV_HEADS = 4
Q_PER_KV = N_Q_HEADS // N_KV_HEADS
ATTN_WIDTH = N_Q_HEADS * HEAD_DIM
KV_WIDTH = N_KV_HEADS * HEAD_DIM
ATTN_BLOCK = 128
WINDOW = 128
ROPE_THETA = 10000.0
CONV_WIDTH = D_MODEL
CONV_KERNEL = 3
N_EXPERTS = 32
TOP_K = 4
SWIGLU_LIMIT = 7.0
SWIGLU_ALPHA = 1.702
RMS_EPS = 1e-5

OFF_Q = 0
OFF_K = OFF_Q + ATTN_WIDTH
OFF_V = OFF_K + KV_WIDTH
OFF_CB = OFF_V + KV_WIDTH
OFF_CC = OFF_CB + CONV_WIDTH
OFF_CX = OFF_CC + CONV_WIDTH
OFF_GA = OFF_CX + CONV_WIDTH
OFF_GC = OFF_GA + D_MODEL
IN_WIDTH = OFF_GC + D_MODEL

LANES = 128
SUBLANES = 8
VMEM_LIMIT_BYTES = 56 * 1024 * 1024
EXPERT_VMEM_LIMIT_BYTES = 60 * 1024 * 1024

TM_INPROJ = 512
COL_CHUNK = 512
Q_BLOCKS_PER_STEP = 8
TM_MIX = 512
TC_RANK = 512
TM_EXPERT = 256
TM_MOVE = 512
PIECE_ROWS = 32
assert TM_MOVE == TC_RANK

BF16 = jnp.bfloat16
F32 = jnp.float32
NEG_BIG = -1e30


def _rms_scale(x, g):
    ms = jnp.mean(x * x, axis=-1, keepdims=True)
    return (x * lax.rsqrt(ms + RMS_EPS)) * g


SLABS = D_MODEL // LANES
assert SLABS == SUBLANES, "a token slab must be exactly one (8, 128) tile of 32-bit words"


def _store_token_major(ref, val):
    tm = val.shape[0]
    for s in range(SLABS):
        ref[pl.ds(s, tm, stride=SLABS), :] = val[:, s * LANES:(s + 1) * LANES]


def _load_token_major(ref, tm):
    return jnp.concatenate(
        [ref[pl.ds(s, tm, stride=SLABS), :] for s in range(SLABS)], axis=1)


def _inproj_body(x_ref, g_ref, w_ref, b_ref, cos_ref, sin_ref, wc_ref,
                 q_ref, k_ref, v_ref, z_ref, sa_ref, sc_ref, carry_ref, *, tiles_per_seq):
    tm = x_ref.shape[0]
    i = pl.program_id(0)
    h = _rms_scale(x_ref[...], g_ref[...]).astype(BF16)

    def proj(c0, width):
        return (jnp.dot(h, w_ref[:, c0:c0 + width], preferred_element_type=F32)
                + b_ref[:, c0:c0 + width])

    cos = cos_ref[...]
    sin = sin_ref[...]
    lane = lax.broadcasted_iota(jnp.int32, (tm, LANES), 1)
    first_half = (lane & (HEAD_DIM // 2)) == 0

    def rope(t):
        partner = jnp.where(first_half,
                            pltpu.roll(t, LANES - HEAD_DIM // 2, 1),
                            pltpu.roll(t, HEAD_DIM // 2, 1))
        return t * cos + partner * sin

    for c in range(0, ATTN_WIDTH, COL_CHUNK):
        acc = proj(OFF_Q + c, COL_CHUNK)
        for j in range(0, COL_CHUNK, LANES):
            q_ref[:, c + j:c + j + LANES] = (
                rope(acc[:, j:j + LANES]) * (HEAD_DIM ** -0.5)).astype(BF16)

    acc = proj(OFF_K, 2 * KV_WIDTH)
    for j in range(0, KV_WIDTH, LANES):
        k_ref[:, j:j + LANES] = rope(acc[:, j:j + LANES]).astype(BF16)
    v_ref[...] = acc[:, KV_WIDTH:].astype(BF16)

    seq_start = (i % tiles_per_seq) == 0
    head = 2 * SUBLANES
    row = lax.broadcasted_iota(jnp.int32, (head, COL_CHUNK), 0)
    for c in range(0, CONV_WIDTH, COL_CHUNK):
        u = proj(OFF_CC + c, COL_CHUNK) * proj(OFF_CX + c, COL_CHUNK)
        cb = proj(OFF_CB + c, COL_CHUNK)
        w0 = wc_ref[0:1, c:c + COL_CHUNK]
        w1 = wc_ref[1:2, c:c + COL_CHUNK]
        w2 = wc_ref[2:3, c:c + COL_CHUNK]
        y = w0 * pltpu.roll(u, 2, 0) + w1 * pltpu.roll(u, 1, 0) + w2 * u
        z_ref[:, c:c + COL_CHUNK] = (cb * y).astype(BF16)
        prev = jnp.where(seq_start, 0.0, carry_ref[:, c:c + COL_CHUNK])
        pad = jnp.zeros((SUBLANES, COL_CHUNK), F32)
        uh = u[0:head]
        u1 = jnp.where(row < 1, jnp.concatenate([pltpu.roll(prev, 1, 0), pad], 0),
                       pltpu.roll(uh, 1, 0))
        u2 = jnp.where(row < 2, jnp.concatenate([pltpu.roll(prev, 2, 0), pad], 0),
                       pltpu.roll(uh, 2, 0))
        yh = w0 * u2 + w1 * u1 + w2 * uh
        z_ref[0:head, c:c + COL_CHUNK] = (cb[0:head] * yh).astype(BF16)
        carry_ref[:, c:c + COL_CHUNK] = u[tm - SUBLANES:tm]

    for c in range(0, D_MODEL, COL_CHUNK):
        sa_ref[:, c:c + COL_CHUNK] = jax.nn.sigmoid(proj(OFF_GA + c, COL_CHUNK)).astype(BF16)
        sc_ref[:, c:c + COL_CHUNK] = jax.nn.sigmoid(proj(OFF_GC + c, COL_CHUNK)).astype(BF16)


def _inproj(x2, g_mix, w_in_bf, b_in, cos_t, sin_t, w_conv, seq_len):
    T = x2.shape[0]
    tm = TM_INPROJ
    tiles_per_seq = seq_len // tm
    const = lambda i: (0, 0)
    row_blk = lambda i: (i, 0)
    pos_blk = lambda i: (i % tiles_per_seq, 0)
    return pl.pallas_call(
        functools.partial(_inproj_body, tiles_per_seq=tiles_per_seq),
        grid=(T // tm,),
        in_specs=[
            pl.BlockSpec((tm, D_MODEL), row_blk),
            pl.BlockSpec((1, D_MODEL), const),
            pl.BlockSpec((D_MODEL, IN_WIDTH), const, pipeline_mode=pl.Buffered(1)),
            pl.BlockSpec((1, IN_WIDTH), const),
            pl.BlockSpec((tm, LANES), pos_blk),
            pl.BlockSpec((tm, LANES), pos_blk),
            pl.BlockSpec((CONV_KERNEL, CONV_WIDTH), const),
        ],
        out_specs=[
            pl.BlockSpec((tm, ATTN_WIDTH), row_blk),
            pl.BlockSpec((tm, KV_WIDTH), row_blk),
            pl.BlockSpec((tm, KV_WIDTH), row_blk),
            pl.BlockSpec((tm, CONV_WIDTH), row_blk),
            pl.BlockSpec((tm, D_MODEL), row_blk),
            pl.BlockSpec((tm, D_MODEL), row_blk),
        ],
        out_shape=[
            jax.ShapeDtypeStruct((T, ATTN_WIDTH), BF16),
            jax.ShapeDtypeStruct((T, KV_WIDTH), BF16),
            jax.ShapeDtypeStruct((T, KV_WIDTH), BF16),
            jax.ShapeDtypeStruct((T, CONV_WIDTH), BF16),
            jax.ShapeDtypeStruct((T, D_MODEL), BF16),
            jax.ShapeDtypeStruct((T, D_MODEL), BF16),
        ],
        scratch_shapes=[pltpu.VMEM((SUBLANES, CONV_WIDTH), F32)],
        compiler_params=pltpu.CompilerParams(
            dimension_semantics=("arbitrary",), vmem_limit_bytes=VMEM_LIMIT_BYTES),
        name="inproj",
    )(x2, g_mix, w_in_bf, b_in, cos_t, sin_t, w_conv)


def _attn_body(sinks_ref, q_ref, kp_ref, kc_ref, vp_ref, vc_ref, o_ref):
    n = pl.program_id(1)
    blk = ATTN_BLOCK
    cols = Q_PER_KV * blk
    j = lax.broadcasted_iota(jnp.int32, (blk, cols), 0)
    qi = lax.broadcasted_iota(jnp.int32, (blk, cols), 1) % blk
    tri = j > qi
    no_prev = tri & (n == 0)
    k3 = jnp.concatenate([kp_ref[...], kc_ref[...]], axis=0)
    v3 = jnp.concatenate([vp_ref[...], vc_ref[...]], axis=0)
    v3_t = v3.astype(F32).T.astype(BF16)
    for qb in range(Q_BLOCKS_PER_STEP):
        k_all = k3[qb * blk:(qb + 2) * blk]
        v_t = v3_t[:, qb * blk:(qb + 2) * blk]
        for kh in range(N_KV_HEADS):
            k_h = k_all[:, kh * HEAD_DIM:(kh + 1) * HEAD_DIM]
            vt_h = v_t[kh * HEAD_DIM:(kh + 1) * HEAD_DIM, :]
            heads = [kh * Q_PER_KV + g for g in range(Q_PER_KV)]
            q_g = jnp.concatenate(
                [q_ref[qb * blk:(qb + 1) * blk, hq * HEAD_DIM:(hq + 1) * HEAD_DIM]
                 for hq in heads], axis=0)
            sink = jnp.concatenate(
                [jnp.full((1, blk), sinks_ref[hq], F32) for hq in heads], axis=1)
            s = lax.dot_general(k_h, q_g, (((1,), (1,)), ((), ())), preferred_element_type=F32)
            fold = jnp.where(tri, s[:blk], s[blk:])
            if qb == 0:
                fold = jnp.where(no_prev, NEG_BIG, fold)
            m = jnp.maximum(jnp.max(fold, axis=0, keepdims=True), sink)
            p = jnp.exp(fold - m)
            denom = jnp.sum(p, axis=0, keepdims=True) + jnp.exp(sink - m)
            p2 = jnp.concatenate([jnp.where(tri, p, 0.0), jnp.where(tri, 0.0, p)], axis=0)
            o_t = jnp.dot(vt_h, p2.astype(BF16), preferred_element_type=F32) / denom
            for g in range(0, Q_PER_KV, 2):
                pair = jnp.concatenate(
                    [o_t[:, g * blk:(g + 1) * blk], o_t[:, (g + 1) * blk:(g + 2) * blk]], axis=0)
                c0 = heads[g] * HEAD_DIM
                o_ref[qb * blk:(qb + 1) * blk, c0:c0 + 2 * HEAD_DIM] = pair.T.astype(BF16)


def _attention(q, k, v, sinks, batch, seq_len):
    T = q.shape[0]
    assert WINDOW == ATTN_BLOCK, "the folded score tile needs window == block"
    rows = Q_BLOCKS_PER_STEP * ATTN_BLOCK
    steps = seq_len // rows
    cur = lambda b, n: (b * steps + n, 0)
    prev = lambda b, n: ((b * steps + n) * Q_BLOCKS_PER_STEP - jnp.minimum(n, 1), 0)
    return pl.pallas_call(
        _attn_body,
        grid=(batch, steps),
        in_specs=[
            pl.BlockSpec(memory_space=pltpu.SMEM),
            pl.BlockSpec((rows, ATTN_WIDTH), cur),
            pl.BlockSpec((ATTN_BLOCK, KV_WIDTH), prev),
            pl.BlockSpec((rows, KV_WIDTH), cur),
            pl.BlockSpec((ATTN_BLOCK, KV_WIDTH), prev),
            pl.BlockSpec((rows, KV_WIDTH), cur),
        ],
        out_specs=pl.BlockSpec((rows, ATTN_WIDTH), cur),
        out_shape=jax.ShapeDtypeStruct((T, ATTN_WIDTH), BF16),
        compiler_params=pltpu.CompilerParams(
            dimension_semantics=("arbitrary", "arbitrary"), vmem_limit_bytes=VMEM_LIMIT_BYTES),
        name="attn",
    )(sinks, q, k, k, v, v)


def _mixout_body(x_ref, a_ref, z_ref, sa_ref, sc_ref, wa_ref, wc_ref, wo_ref, g_ref,
                 wr_ref, br_ref, x1_ref, hp_ref, idx_ref, wgt_ref, slab_ref):
    tm = x_ref.shape[0]
    y_attn = jnp.dot(a_ref[...], wa_ref[...], preferred_element_type=F32)
    y_conv = jnp.dot(z_ref[...], wc_ref[...], preferred_element_type=F32)
    merged = sa_ref[...].astype(F32) * y_attn + sc_ref[...].astype(F32) * y_conv
    x1 = x_ref[...] + jnp.dot(merged.astype(BF16), wo_ref[...], preferred_element_type=F32)
    x1_ref[...] = x1
    h = _rms_scale(x1, g_ref[...])
    _store_token_major(slab_ref, h)
    hp_ref[...] = slab_ref[...].astype(BF16)
    h_hi = h.astype(BF16)
    h_lo = (h - h_hi.astype(F32)).astype(BF16)
    p_hi = jnp.dot(h_hi, wr_ref[...], preferred_element_type=F32)
    p_lo = jnp.dot(h_lo, wr_ref[...], preferred_element_type=F32)
    lg = p_hi + pltpu.roll(p_hi, LANES - N_EXPERTS, 1) + p_lo
    logits = lg.T[0:N_EXPERTS, :] + br_ref[...]
    e_iota = lax.broadcasted_iota(jnp.int32, (N_EXPERTS, tm), 0)
    vals, idxs = [], []
    for _ in range(TOP_K):
        m = jnp.max(logits, axis=0, keepdims=True)
        idx = jnp.min(jnp.where(logits == m, e_iota, N_EXPERTS), axis=0, keepdims=True)
        vals.append(m)
        idxs.append(idx)
        logits = jnp.where(e_iota == idx, -jnp.inf, logits)
    ex = [jnp.exp(v - vals[0]) for v in vals]
    tot = ex[0] + ex[1] + ex[2] + ex[3]
    idx_ref[...] = jnp.concatenate(idxs, axis=0)
    wgt_ref[...] = jnp.concatenate([e / tot for e in ex], axis=0)


def _mixout(x2, attn, z, sa, sc, wa_bf, wc_bf, wo_bf, g_ffn, wr_t, b_router):
    T = x2.shape[0]
    tm = TM_MIX
    const = lambda i: (0, 0)
    row_blk = lambda i: (i, 0)
    col_blk = lambda i: (0, i)
    act = pl.BlockSpec((tm, D_MODEL), row_blk)
    wsq = pl.BlockSpec((D_MODEL, D_MODEL), const)
    return pl.pallas_call(
        _mixout_body,
        grid=(T // tm,),
        in_specs=[act, act, act, act, act, wsq, wsq, wsq,
                  pl.BlockSpec((1, D_MODEL), const),
                  pl.BlockSpec((D_MODEL, LANES), const),
                  pl.BlockSpec((N_EXPERTS, 1), const)],
        out_specs=[act,
                   pl.BlockSpec((tm * SLABS, LANES), row_blk),
                   pl.BlockSpec((TOP_K, tm), col_blk),
                   pl.BlockSpec((TOP_K, tm), col_blk)],
        out_shape=[jax.ShapeDtypeStruct((T, D_MODEL), F32),
                   jax.ShapeDtypeStruct((T * SLABS, LANES), BF16),
                   jax.ShapeDtypeStruct((TOP_K, T), jnp.int32),
                   jax.ShapeDtypeStruct((TOP_K, T), F32)],
        scratch_shapes=[pltpu.VMEM((tm * SLABS, LANES), F32)],
        compiler_params=pltpu.CompilerParams(
            dimension_semantics=("arbitrary",), vmem_limit_bytes=VMEM_LIMIT_BYTES),
        name="mixout",
    )(x2, attn, z, sa, sc, wa_bf, wc_bf, wo_bf, g_ffn, wr_t, b_router)


def _rank_body(idx_ref, rank_ref, cnt_ref, before_ref, carry_ref):
    tc = idx_ref.shape[1]
    i = pl.program_id(0)

    @pl.when(i == 0)
    def _():
        carry_ref[...] = jnp.zeros_like(carry_ref)

    idx = idx_ref[...]
    e_iota = lax.broadcasted_iota(jnp.int32, (N_EXPERTS, tc), 0)
    sel = [e_iota == idx[k:k + 1, :] for k in range(TOP_K)]
    member = (sel[0] | sel[1] | sel[2] | sel[3])
    onehot = jnp.where(member, 1.0, 0.0).astype(BF16)
    r = lax.broadcasted_iota(jnp.int32, (tc, tc), 0)
    c = lax.broadcasted_iota(jnp.int32, (tc, tc), 1)
    before = jnp.where(r < c, 1.0, 0.0).astype(BF16)
    carry = carry_ref[:, 0:1]
    before_ref[...] = carry_ref[...].astype(jnp.int32)
    prefix = jnp.dot(onehot, before, preferred_element_type=F32) + carry
    ranks = [jnp.sum(jnp.where(sel[k], prefix, 0.0), axis=0, keepdims=True)
             for k in range(TOP_K)]
    rank_ref[...] = jnp.concatenate(ranks, axis=0).astype(jnp.int32)
    total = carry + jnp.sum(onehot.astype(F32), axis=1, keepdims=True)
    carry_ref[...] = jnp.broadcast_to(total, carry_ref.shape)
    cnt_ref[...] = jnp.broadcast_to(total, cnt_ref.shape).astype(jnp.int32)


def _rank(idx_t):
    T = idx_t.shape[1]
    tc = TC_RANK
    return pl.pallas_call(
        _rank_body,
        grid=(T // tc,),
        in_specs=[pl.BlockSpec((TOP_K, tc), lambda i: (0, i))],
        out_specs=[pl.BlockSpec((TOP_K, tc), lambda i: (0, i)),
                   pl.BlockSpec((N_EXPERTS, LANES), lambda i: (0, 0)),
                   pl.BlockSpec((N_EXPERTS, LANES), lambda i: (i, 0))],
        out_shape=[jax.ShapeDtypeStruct((TOP_K, T), jnp.int32),
                   jax.ShapeDtypeStruct((N_EXPERTS, LANES), jnp.int32),
                   jax.ShapeDtypeStruct((T // tc * N_EXPERTS, LANES), jnp.int32)],
        scratch_shapes=[pltpu.VMEM((N_EXPERTS, LANES), F32)],
        compiler_params=pltpu.CompilerParams(dimension_semantics=("arbitrary",)),
        name="rank",
    )(idx_t)


def _invperm_body(dest_ref, fill_ref, end_ref, src_ref, *, n_assign, n_slots):
    def mark_tile(start):
        def mark(q, carry):
            src_ref[start + q] = 0
            return carry
        lax.fori_loop(0, TM_EXPERT, mark, 0, unroll=16)

    def mark_expert_tail(e, carry):
        mark_tile(fill_ref[e])
        return carry

    lax.fori_loop(0, N_EXPERTS, mark_expert_tail, 0)

    def mark_unused(t, carry):
        mark_tile(t * TM_EXPERT)
        return carry

    lax.fori_loop(end_ref[N_EXPERTS - 1] // TM_EXPERT, n_slots // TM_EXPERT + 1, mark_unused, 0)

    def place(a, carry):
        src_ref[dest_ref[a]] = a
        return carry

    lax.fori_loop(0, n_assign, place, 0, unroll=64)


def _invperm(dest_flat, fill_start, pad_end, n_slots):
    return pl.pallas_call(
        functools.partial(_invperm_body, n_assign=dest_flat.shape[0], n_slots=n_slots),
        grid_spec=pltpu.PrefetchScalarGridSpec(
            num_scalar_prefetch=3,
            grid=(1,),
            in_specs=[],
            out_specs=pl.BlockSpec(memory_space=pltpu.SMEM),
        ),
        out_shape=jax.ShapeDtypeStruct((n_slots + TM_EXPERT,), jnp.int32),
        compiler_params=pltpu.CompilerParams(dimension_semantics=("arbitrary",)),
        name="invperm",
    )(dest_flat, fill_start, pad_end)


def _expert_body(blk_e_ref, nxt_e_ref, n_used_ref, src_ref, hp_ref, wg_hbm, wu_hbm, wd_hbm,
                 bg_ref, bu_ref, bd_ref, y_ref, stage_bf, stage, xbuf, wst, w_bf, wsem, *, n_tok):
    tm = TM_EXPERT
    i = pl.program_id(0)
    n_used = n_used_ref[0]
    slot = i % 2
    e = blk_e_ref[i]
    e_prev = blk_e_ref[jnp.maximum(i - 1, 0)]

    def weight_copies(ex):
        return [pltpu.make_async_copy(w.at[ex], wst.at[m], wsem.at[m])
                for m, w in enumerate((wg_hbm, wu_hbm, wd_hbm))]

    def rows_of(buf):
        return pl.ds(pl.multiple_of(buf * tm, tm), tm)

    def gather_rows(tile, rows):
        for r in rows:
            a = src_ref[tile * tm + r]
            stage_bf[pl.ds(r * SLABS, SLABS), :] = hp_ref[a & (n_tok - 1)]

    def finish_gather(buf):
        stage[...] = stage_bf[...].astype(F32)
        xbuf[rows_of(buf), :] = _load_token_major(stage, tm).astype(BF16)

    @pl.when(i == 0)
    def _():
        for c in weight_copies(e):
            c.start()
        gather_rows(0, range(tm))
        finish_gather(0)

    @pl.when(i >= n_used)
    def _():
        y_ref[...] = jnp.zeros_like(y_ref)

    @pl.when(i < n_used)
    def _():
        @pl.when((i == 0) | (e != e_prev))
        def _():
            for c in weight_copies(e):
                c.wait()
            rows = D_MODEL // SUBLANES
            for m in range(3):
                def cast_rows(j, carry, m=m):
                    sl = pl.ds(pl.multiple_of(j * rows, rows), rows)
                    w_bf[m, sl, :] = wst[m, sl, :].astype(BF16)
                    return carry
                lax.fori_loop(0, SUBLANES, cast_rows, 0)
            nxt_e = nxt_e_ref[i]

            @pl.when(nxt_e >= 0)
            def _():
                for c in weight_copies(nxt_e):
                    c.start()

        x = xbuf[rows_of(slot), :]
        gather_rows(jnp.minimum(i + 1, n_used - 1), range(tm))
        finish_gather(1 - slot)
        g = jnp.dot(x, w_bf[0], preferred_element_type=F32) + bg_ref[pl.ds(e, 1), :]
        u = jnp.dot(x, w_bf[1], preferred_element_type=F32) + bu_ref[pl.ds(e, 1), :]
        g = jnp.minimum(g, SWIGLU_LIMIT)
        u = jnp.clip(u, -SWIGLU_LIMIT, SWIGLU_LIMIT)
        a = g * jax.nn.sigmoid(SWIGLU_ALPHA * g) * (u + 1.0)
        y = jnp.dot(a.astype(BF16), w_bf[2], preferred_element_type=F32) + bd_ref[pl.ds(e, 1), :]
        _store_token_major(y_ref, y)


def _experts(blk_e, nxt_e, n_used, src, hp, w_gate, b_gate, w_up, b_up, w_down, b_down, n_tiles):
    tm = TM_EXPERT
    d_ff = w_gate.shape[2]
    assert d_ff == D_MODEL
    n_tok = hp.shape[0]
    assert n_tok & (n_tok - 1) == 0, "assignment ids are split with a power-of-two mask"
    assert n_tiles >= 2
    const2 = lambda i, *_: (0, 0)
    bias = pl.BlockSpec((N_EXPERTS, D_MODEL), const2)
    hbm = pl.BlockSpec(memory_space=pl.ANY)
    return pl.pallas_call(
        functools.partial(_expert_body, n_tok=n_tok),
        grid_spec=pltpu.PrefetchScalarGridSpec(
            num_scalar_prefetch=4,
            grid=(n_tiles,),
            in_specs=[
                pl.BlockSpec((n_tok, SLABS, LANES), lambda i, *_: (0, 0, 0),
                             pipeline_mode=pl.Buffered(1)),
                hbm, hbm, hbm, bias, bias, bias,
            ],
            out_specs=pl.BlockSpec((tm * SLABS, LANES), lambda i, *_: (i, 0)),
            scratch_shapes=[pltpu.VMEM((tm * SLABS, LANES), BF16),
                            pltpu.VMEM((tm * SLABS, LANES), F32),
                            pltpu.VMEM((2 * tm, D_MODEL), BF16),
                            pltpu.VMEM((3, D_MODEL, D_MODEL), F32),
                            pltpu.VMEM((3, D_MODEL, D_MODEL), BF16),
                            pltpu.SemaphoreType.DMA((3,))],
        ),
        out_shape=jax.ShapeDtypeStruct((n_tiles * tm * SLABS, LANES), F32),
        compiler_params=pltpu.CompilerParams(
            dimension_semantics=("arbitrary",), vmem_limit_bytes=EXPERT_VMEM_LIMIT_BYTES),
        name="experts",
    )(blk_e, nxt_e, n_used, src, hp, w_gate, w_up, w_down, b_gate, b_up, b_down)


def _combine_body(piece_src_ref, n_pieces_ref, srow_ref, wgt_ref, y_hbm, x1_ref, g_ref, o_ref,
                  stage, tokmaj, sem, *, n_tok):
    tm = x1_ref.shape[0]
    b = pl.program_id(0)
    slot = b % 2
    piece = PIECE_ROWS * SLABS
    max_pieces = stage.shape[1] // piece

    def piece_copy(blk, buf, j):
        src = pl.multiple_of(piece_src_ref[blk * max_pieces + j] * SLABS, SLABS)
        dst = pl.multiple_of(j * piece, piece)
        return pltpu.make_async_copy(y_hbm.at[pl.ds(src, piece)], stage.at[buf, pl.ds(dst, piece)],
                                     sem.at[buf])

    def start_pieces(blk, buf):
        def body(j, carry):
            piece_copy(blk, buf, j).start()
            return carry
        lax.fori_loop(0, n_pieces_ref[blk], body, 0)

    def wait_pieces(blk, buf):
        def body(j, carry):
            piece_copy(blk, buf, j).wait()
            return carry
        lax.fori_loop(0, n_pieces_ref[blk], body, 0)

    @pl.when(b == 0)
    def _():
        start_pieces(0, 0)

    @pl.when(b + 1 < pl.num_programs(0))
    def _():
        start_pieces(b + 1, 1 - slot)

    wait_pieces(b, slot)

    def token(t, carry):
        base = b * tm + t
        acc = None
        for k in range(TOP_K):
            row = pl.multiple_of(srow_ref[k * n_tok + base], SLABS)
            term = wgt_ref[k * n_tok + base] * stage[slot, pl.ds(row, SLABS), :]
            acc = term if acc is None else acc + term
        tokmaj[pl.ds(pl.multiple_of(t * SLABS, SLABS), SLABS), :] = acc
        return carry

    lax.fori_loop(0, tm, token, 0, unroll=32)
    o_ref[...] = _rms_scale(x1_ref[...] + _load_token_major(tokmaj, tm), g_ref[...])


def _combine(piece_src, n_pieces, srow, wgt, y_sorted, x1, g_final):
    T = x1.shape[0]
    tm = TM_MOVE
    max_pieces = piece_src.shape[0] // (T // tm)
    row_blk = lambda i, *_: (i, 0)
    return pl.pallas_call(
        functools.partial(_combine_body, n_tok=T),
        grid_spec=pltpu.PrefetchScalarGridSpec(
            num_scalar_prefetch=4,
            grid=(T // tm,),
            in_specs=[
                pl.BlockSpec(memory_space=pl.ANY),
                pl.BlockSpec((tm, D_MODEL), row_blk),
                pl.BlockSpec((1, D_MODEL), lambda i, *_: (0, 0)),
            ],
            out_specs=pl.BlockSpec((tm, D_MODEL), row_blk),
            scratch_shapes=[pltpu.VMEM((2, max_pieces * PIECE_ROWS * SLABS, LANES), F32),
                            pltpu.VMEM((tm * SLABS, LANES), F32),
                            pltpu.SemaphoreType.DMA((2,))],
        ),
        out_shape=jax.ShapeDtypeStruct((T, D_MODEL), F32),
        compiler_params=pltpu.CompilerParams(
            dimension_semantics=("arbitrary",), vmem_limit_bytes=VMEM_LIMIT_BYTES),
        name="combine",
    )(piece_src, n_pieces, srow, wgt, y_sorted, x1, g_final)


def _rope_tables(seq_len):
    half = HEAD_DIM // 2
    inv_freq = ROPE_THETA ** (-np.arange(half, dtype=np.float64) / half)
    ang = np.arange(seq_len, dtype=np.float64)[:, None] * inv_freq[None, :]
    cos = np.cos(ang).astype(np.float32)
    sin = np.sin(ang).astype(np.float32)
    reps = LANES // HEAD_DIM
    cos_t = np.tile(np.concatenate([cos, cos], axis=-1), (1, reps))
    sin_t = np.tile(np.concatenate([-sin, sin], axis=-1), (1, reps))
    return jnp.asarray(cos_t), jnp.asarray(sin_t)


def _router_hi_lo(w_router):
    hi = w_router.astype(BF16)
    lo = (w_router - hi.astype(F32)).astype(BF16)
    pad = jnp.zeros((w_router.shape[0], LANES - 2 * N_EXPERTS), BF16)
    return jnp.concatenate([hi, lo, pad], axis=1)


def _layer(x2, batch, seq_len, g_mix, w_in, b_in, sinks, w_conv, w_attn_o, w_conv_o, w_out,
           g_ffn, w_router, b_router, w_gate, b_gate, w_up, b_up, w_down, b_down, g_out):
    T = x2.shape[0]
    cos_t, sin_t = _rope_tables(seq_len)
    q, k, v, z, sa, sc = _inproj(x2, g_mix[None, :], w_in.astype(BF16), b_in[None, :],
                                 cos_t, sin_t, w_conv, seq_len)
    attn = _attention(q, k, v, sinks, batch, seq_len)
    x1, hp, idx_t, wgt_t = _mixout(
        x2, attn, z, sa, sc, w_attn_o.astype(BF16), w_conv_o.astype(BF16), w_out.astype(BF16),
        g_ffn[None, :], _router_hi_lo(w_router), b_router[:, None])

    rank_t, cnt, before = _rank(idx_t)
    counts = cnt[:, 0]
    padded = (counts + TM_EXPERT - 1) // TM_EXPERT * TM_EXPERT
    pad_end = jnp.cumsum(padded)
    pad_start = pad_end - padded
    n_tiles = (T * TOP_K) // TM_EXPERT + N_EXPERTS + 1
    tile_row = jnp.arange(n_tiles, dtype=jnp.int32) * TM_EXPERT
    blk_e = jnp.minimum(jnp.sum(pad_end[None, :] <= tile_row[:, None], axis=1),
                        N_EXPERTS - 1).astype(jnp.int32)
    n_used = (pad_end[-1:] // TM_EXPERT).astype(jnp.int32)
    e_ids = jnp.arange(N_EXPERTS, dtype=jnp.int32)
    dest = rank_t + jnp.sum(
        jnp.where(idx_t[:, :, None] == e_ids[None, None, :], pad_start[None, None, :], 0), axis=-1)
    dest_flat = dest.reshape(-1).astype(jnp.int32)
    fill_start = (pad_start + counts).astype(jnp.int32)

    group_end = jnp.sum(jnp.where(blk_e[:, None] == e_ids[None, :], pad_end[None, :], 0), axis=1)
    nxt_e = jnp.where(
        group_end < pad_end[-1],
        jnp.minimum(jnp.sum(pad_end[None, :] <= group_end[:, None], axis=1), N_EXPERTS - 1),
        -1).astype(jnp.int32)

    src = _invperm(dest_flat, fill_start, pad_end.astype(jnp.int32), n_tiles * TM_EXPERT)
    y_sorted = _experts(blk_e, nxt_e, n_used, src, hp.reshape(T, SLABS, LANES),
                        w_gate, b_gate, w_up, b_up, w_down, b_down, n_tiles)

    n_blocks = T // TM_MOVE
    max_pieces = TOP_K * TM_MOVE // PIECE_ROWS + N_EXPERTS
    lo = before.reshape(n_blocks, N_EXPERTS, LANES)[:, :, 0]
    cnt_be = jnp.concatenate([lo[1:], counts[None, :]], axis=0) - lo
    pieces_be = (cnt_be + PIECE_ROWS - 1) // PIECE_ROWS
    pend = jnp.cumsum(pieces_be, axis=1)
    pbase = pend - pieces_be
    n_pieces = pend[:, -1].astype(jnp.int32)
    j = jnp.arange(max_pieces, dtype=jnp.int32)
    e_of_piece = jnp.minimum(jnp.sum(pend[:, None, :] <= j[None, :, None], axis=2), N_EXPERTS - 1)
    onehot = e_of_piece[:, :, None] == e_ids[None, None, :]
    pick = lambda tab: jnp.sum(jnp.where(onehot, tab[:, None, :], 0), axis=2)
    piece_src = pick(pad_start[None, :] + lo) + PIECE_ROWS * (j[None, :] - pick(pbase))
    piece_src = jnp.where(j[None, :] < n_pieces[:, None], piece_src, 0).astype(jnp.int32)
    tab = jnp.broadcast_to((PIECE_ROWS * pbase - lo)[:, None, :],
                           (n_blocks, TM_MOVE, N_EXPERTS)).reshape(T, N_EXPERTS)
    srow = rank_t + jnp.sum(jnp.where(idx_t[:, :, None] == e_ids[None, None, :], tab[None], 0), axis=-1)
    return _combine(piece_src.reshape(-1), n_pieces, (srow * SLABS).reshape(-1).astype(jnp.int32),
                    wgt_t.reshape(-1), y_sorted, x1, g_out[None, :])


def kernel(x, g_mix, w_in, b_in, sinks, w_conv, w_attn_o, w_conv_o, w_out, g_ffn, w_router,
           b_router, w_gate, b_gate, w_up, b_up, w_down, b_down, g_final):
    batch, seq_len, d = x.shape
    depth = g_mix.shape[0]
    assert depth == 1, "the final norm is fused into the single layer's combine step"
    x2 = x.reshape(batch * seq_len, d)
    out = _layer(x2, batch, seq_len, g_mix[0], w_in[0], b_in[0], sinks[0], w_conv[0],
                 w_attn_o[0], w_conv_o[0], w_out[0], g_ffn[0], w_router[0], b_router[0],
                 w_gate[0], b_gate[0], w_up[0], b_up[0], w_down[0], b_down[0], g_final)
    return out.reshape(batch, seq_len, d)
```

```python
import functools

import jax
import jax.numpy as jnp
import numpy as np
from jax import lax
from jax.experimental import pallas as pl
from jax.experimental.pallas import tpu as pltpu

D_MODEL = 1024
HEAD_DIM = 64
N_Q_HEADS = 16
N_KV_HEADS = 4
Q_PER_KV = N_Q_HEADS // N_KV_HEADS
ATTN_WIDTH = N_Q_HEADS * HEAD_DIM
KV_WIDTH = N_KV_HEADS * HEAD_DIM
ATTN_BLOCK = 128
WINDOW = 128
ROPE_THETA = 10000.0
CONV_WIDTH = D_MODEL
CONV_KERNEL = 3
N_EXPERTS = 32
TOP_K = 4
SWIGLU_LIMIT = 7.0
SWIGLU_ALPHA = 1.702
RMS_EPS = 1e-5

OFF_Q = 0
OFF_K = OFF_Q + ATTN_WIDTH
OFF_V = OFF_K + KV_WIDTH
OFF_CB = OFF_V + KV_WIDTH
OFF_CC = OFF_CB + CONV_WIDTH
OFF_CX = OFF_CC + CONV_WIDTH
OFF_GA = OFF_CX + CONV_WIDTH
OFF_GC = OFF_GA + D_MODEL
IN_WIDTH = OFF_GC + D_MODEL

LANES = 128
SUBLANES = 8
VMEM_LIMIT_BYTES = 56 * 1024 * 1024
EXPERT_VMEM_LIMIT_BYTES = 60 * 1024 * 1024

TM_INPROJ = 512
COL_CHUNK = 512
Q_BLOCKS_PER_STEP = 8
TM_MIX = 512
TC_RANK = 512
TM_EXPERT = 256
FF_CHUNK = 512
TM_MOVE = 512
PIECE_ROWS = 32
assert TM_MOVE == TC_RANK

BF16 = jnp.bfloat16
F32 = jnp.float32
NEG_BIG = -1e30


def _rms_scale(x, g):
    ms = jnp.mean(x * x, axis=-1, keepdims=True)
    return (x * lax.rsqrt(ms + RMS_EPS)) * g


SLABS = D_MODEL // LANES
assert SLABS == SUBLANES, "a token slab must be exactly one (8, 128) tile of 32-bit words"


def _store_token_major(ref, val):
    tm = val.shape[0]
    for s in range(SLABS):
        ref[pl.ds(s, tm, stride=SLABS), :] = val[:, s * LANES:(s + 1) * LANES]


def _load_token_major(ref, tm):
    return jnp.concatenate(
        [ref[pl.ds(s, tm, stride=SLABS), :] for s in range(SLABS)], axis=1)


def _inproj_body(x_ref, g_ref, w_ref, b_ref, cos_ref, sin_ref, wc_ref,
                 q_ref, k_ref, v_ref, z_ref, sa_ref, sc_ref, carry_ref, *, tiles_per_seq):
    tm = x_ref.shape[0]
    i = pl.program_id(0)
    h = _rms_scale(x_ref[...], g_ref[...]).astype(BF16)

    def proj(c0, width):
        return (jnp.dot(h, w_ref[:, c0:c0 + width], preferred_element_type=F32)
                + b_ref[:, c0:c0 + width])

    cos = cos_ref[...]
    sin = sin_ref[...]
    lane = lax.broadcasted_iota(jnp.int32, (tm, LANES), 1)
    first_half = (lane & (HEAD_DIM // 2)) == 0

    def rope(t):
        partner = jnp.where(first_half,
                            pltpu.roll(t, LANES - HEAD_DIM // 2, 1),
                            pltpu.roll(t, HEAD_DIM // 2, 1))
        return t * cos + partner * sin

    for c in range(0, ATTN_WIDTH, COL_CHUNK):
        acc = proj(OFF_Q + c, COL_CHUNK)
        for j in range(0, COL_CHUNK, LANES):
            q_ref[:, c + j:c + j + LANES] = (
                rope(acc[:, j:j + LANES]) * (HEAD_DIM ** -0.5)).astype(BF16)

    acc = proj(OFF_K, 2 * KV_WIDTH)
    for j in range(0, KV_WIDTH, LANES):
        k_ref[:, j:j + LANES] = rope(acc[:, j:j + LANES]).astype(BF16)
    v_ref[...] = acc[:, KV_WIDTH:].astype(BF16)

    seq_start = (i % tiles_per_seq) == 0
    head = 2 * SUBLANES
    row = lax.broadcasted_iota(jnp.int32, (head, COL_CHUNK), 0)
    for c in range(0, CONV_WIDTH, COL_CHUNK):
        u = proj(OFF_CC + c, COL_CHUNK) * proj(OFF_CX + c, COL_CHUNK)
        cb = proj(OFF_CB + c, COL_CHUNK)
        w0 = wc_ref[0:1, c:c + COL_CHUNK]
        w1 = wc_ref[1:2, c:c + COL_CHUNK]
        w2 = wc_ref[2:3, c:c + COL_CHUNK]
        y = w0 * pltpu.roll(u, 2, 0) + w1 * pltpu.roll(u, 1, 0) + w2 * u
        z_ref[:, c:c + COL_CHUNK] = (cb * y).astype(BF16)
        prev = jnp.where(seq_start, 0.0, carry_ref[:, c:c + COL_CHUNK])
        pad = jnp.zeros((SUBLANES, COL_CHUNK), F32)
        uh = u[0:head]
        u1 = jnp.where(row < 1, jnp.concatenate([pltpu.roll(prev, 1, 0), pad], 0),
                       pltpu.roll(uh, 1, 0))
        u2 = jnp.where(row < 2, jnp.concatenate([pltpu.roll(prev, 2, 0), pad], 0),
                       pltpu.roll(uh, 2, 0))
        yh = w0 * u2 + w1 * u1 + w2 * uh
        z_ref[0:head, c:c + COL_CHUNK] = (cb[0:head] * yh).astype(BF16)
        carry_ref[:, c:c + COL_CHUNK] = u[tm - SUBLANES:tm]

    for c in range(0, D_MODEL, COL_CHUNK):
        sa_ref[:, c:c + COL_CHUNK] = jax.nn.sigmoid(proj(OFF_GA + c, COL_CHUNK)).astype(BF16)
        sc_ref[:, c:c + COL_CHUNK] = jax.nn.sigmoid(proj(OFF_GC + c, COL_CHUNK)).astype(BF16)


def _inproj(x2, g_mix, w_in_bf, b_in, cos_t, sin_t, w_conv, seq_len):
    T = x2.shape[0]
    tm = TM_INPROJ
    tiles_per_seq = seq_len // tm
    const = lambda i: (0, 0)
    row_blk = lambda i: (i, 0)
    pos_blk = lambda i: (i % tiles_per_seq, 0)
    return pl.pallas_call(
        functools.partial(_inproj_body, tiles_per_seq=tiles_per_seq),
        grid=(T // tm,),
        in_specs=[
            pl.BlockSpec((tm, D_MODEL), row_blk),
            pl.BlockSpec((1, D_MODEL), const),
            pl.BlockSpec((D_MODEL, IN_WIDTH), const, pipeline_mode=pl.Buffered(1)),
            pl.BlockSpec((1, IN_WIDTH), const),
            pl.BlockSpec((tm, LANES), pos_blk),
            pl.BlockSpec((tm, LANES), pos_blk),
            pl.BlockSpec((CONV_KERNEL, CONV_WIDTH), const),
        ],
        out_specs=[
            pl.BlockSpec((tm, ATTN_WIDTH), row_blk),
            pl.BlockSpec((tm, KV_WIDTH), row_blk),
            pl.BlockSpec((tm, KV_WIDTH), row_blk),
            pl.BlockSpec((tm, CONV_WIDTH), row_blk),
            pl.BlockSpec((tm, D_MODEL), row_blk),
            pl.BlockSpec((tm, D_MODEL), row_blk),
        ],
        out_shape=[
            jax.ShapeDtypeStruct((T, ATTN_WIDTH), BF16),
            jax.ShapeDtypeStruct((T, KV_WIDTH), BF16),
            jax.ShapeDtypeStruct((T, KV_WIDTH), BF16),
            jax.ShapeDtypeStruct((T, CONV_WIDTH), BF16),
            jax.ShapeDtypeStruct((T, D_MODEL), BF16),
            jax.ShapeDtypeStruct((T, D_MODEL), BF16),
        ],
        scratch_shapes=[pltpu.VMEM((SUBLANES, CONV_WIDTH), F32)],
        compiler_params=pltpu.CompilerParams(
            dimension_semantics=("arbitrary",), vmem_limit_bytes=VMEM_LIMIT_BYTES),
        name="inproj",
    )(x2, g_mix, w_in_bf, b_in, cos_t, sin_t, w_conv)


def _attn_body(sinks_ref, q_ref, kp_ref, kc_ref, vp_ref, vc_ref, o_ref):
    n = pl.program_id(1)
    blk = ATTN_BLOCK
    cols = Q_PER_KV * blk
    j = lax.broadcasted_iota(jnp.int32, (blk, cols), 0)
    qi = lax.broadcasted_iota(jnp.int32, (blk, cols), 1) % blk
    tri = j > qi
    no_prev = tri & (n == 0)
    k3 = jnp.concatenate([kp_ref[...], kc_ref[...]], axis=0)
    v3 = jnp.concatenate([vp_ref[...], vc_ref[...]], axis=0)
    v3_t = v3.astype(F32).T.astype(BF16)
    for qb in range(Q_BLOCKS_PER_STEP):
        k_all = k3[qb * blk:(qb + 2) * blk]
        v_t = v3_t[:, qb * blk:(qb + 2) * blk]
        for kh in range(N_KV_HEADS):
            k_h = k_all[:, kh * HEAD_DIM:(kh + 1) * HEAD_DIM]
            vt_h = v_t[kh * HEAD_DIM:(kh + 1) * HEAD_DIM, :]
            heads = [kh * Q_PER_KV + g for g in range(Q_PER_KV)]
            q_g = jnp.concatenate(
                [q_ref[qb * blk:(qb + 1) * blk, hq * HEAD_DIM:(hq + 1) * HEAD_DIM]
                 for hq in heads], axis=0)
            sink = jnp.concatenate(
                [jnp.full((1, blk), sinks_ref[hq], F32) for hq in heads], axis=1)
            s = lax.dot_general(k_h, q_g, (((1,), (1,)), ((), ())), preferred_element_type=F32)
            fold = jnp.where(tri, s[:blk], s[blk:])
            if qb == 0:
                fold = jnp.where(no_prev, NEG_BIG, fold)
            m = jnp.maximum(jnp.max(fold, axis=0, keepdims=True), sink)
            p = jnp.exp(fold - m)
            denom = jnp.sum(p, axis=0, keepdims=True) + jnp.exp(sink - m)
            p2 = jnp.concatenate([jnp.where(tri, p, 0.0), jnp.where(tri, 0.0, p)], axis=0)
            o_t = jnp.dot(vt_h, p2.astype(BF16), preferred_element_type=F32) / denom
            for g in range(0, Q_PER_KV, 2):
                pair = jnp.concatenate(
                    [o_t[:, g * blk:(g + 1) * blk], o_t[:, (g + 1) * blk:(g + 2) * blk]], axis=0)
                c0 = heads[g] * HEAD_DIM
                o_ref[qb * blk:(qb + 1) * blk, c0:c0 + 2 * HEAD_DIM] = pair.T.astype(BF16)


def _attention(q, k, v, sinks, batch, seq_len):
    T = q.shape[0]
    assert WINDOW == ATTN_BLOCK, "the folded score tile needs window == block"
    rows = Q_BLOCKS_PER_STEP * ATTN_BLOCK
    steps = seq_len // rows
    cur = lambda b, n: (b * steps + n, 0)
    prev = lambda b, n: ((b * steps + n) * Q_BLOCKS_PER_STEP - jnp.minimum(n, 1), 0)
    return pl.pallas_call(
        _attn_body,
        grid=(batch, steps),
        in_specs=[
            pl.BlockSpec(memory_space=pltpu.SMEM),
            pl.BlockSpec((rows, ATTN_WIDTH), cur),
            pl.BlockSpec((ATTN_BLOCK, KV_WIDTH), prev),
            pl.BlockSpec((rows, KV_WIDTH), cur),
            pl.BlockSpec((ATTN_BLOCK, KV_WIDTH), prev),
            pl.BlockSpec((rows, KV_WIDTH), cur),
        ],
        out_specs=pl.BlockSpec((rows, ATTN_WIDTH), cur),
        out_shape=jax.ShapeDtypeStruct((T, ATTN_WIDTH), BF16),
        compiler_params=pltpu.CompilerParams(
            dimension_semantics=("arbitrary", "arbitrary"), vmem_limit_bytes=VMEM_LIMIT_BYTES),
        name="attn",
    )(sinks, q, k, k, v, v)


def _mixout_body(x_ref, a_ref, z_ref, sa_ref, sc_ref, wa_ref, wc_ref, wo_ref, g_ref,
                 wr_ref, br_ref, x1_ref, hp_ref, idx_ref, wgt_ref, slab_ref):
    tm = x_ref.shape[0]
    y_attn = jnp.dot(a_ref[...], wa_ref[...], preferred_element_type=F32)
    y_conv = jnp.dot(z_ref[...], wc_ref[...], preferred_element_type=F32)
    merged = sa_ref[...].astype(F32) * y_attn + sc_ref[...].astype(F32) * y_conv
    x1 = x_ref[...] + jnp.dot(merged.astype(BF16), wo_ref[...], preferred_element_type=F32)
    x1_ref[...] = x1
    h = _rms_scale(x1, g_ref[...])
    _store_token_major(slab_ref, h)
    hp_ref[...] = slab_ref[...].astype(BF16)
    h_hi = h.astype(BF16)
    h_lo = (h - h_hi.astype(F32)).astype(BF16)
    p_hi = jnp.dot(h_hi, wr_ref[...], preferred_element_type=F32)
    p_lo = jnp.dot(h_lo, wr_ref[...], preferred_element_type=F32)
    lg = p_hi + pltpu.roll(p_hi, LANES - N_EXPERTS, 1) + p_lo
    logits = lg.T[0:N_EXPERTS, :] + br_ref[...]
    e_iota = lax.broadcasted_iota(jnp.int32, (N_EXPERTS, tm), 0)
    vals, idxs = [], []
    for _ in range(TOP_K):
        m = jnp.max(logits, axis=0, keepdims=True)
        idx = jnp.min(jnp.where(logits == m, e_iota, N_EXPERTS), axis=0, keepdims=True)
        vals.append(m)
        idxs.append(idx)
        logits = jnp.where(e_iota == idx, -jnp.inf, logits)
    ex = [jnp.exp(v - vals[0]) for v in vals]
    tot = ex[0] + ex[1] + ex[2] + ex[3]
    idx_ref[...] = jnp.concatenate(idxs, axis=0)
    wgt_ref[...] = jnp.concatenate([e / tot for e in ex], axis=0)


def _mixout(x2, attn, z, sa, sc, wa_bf, wc_bf, wo_bf, g_ffn, wr_t, b_router):
    T = x2.shape[0]
    tm = TM_MIX
    const = lambda i: (0, 0)
    row_blk = lambda i: (i, 0)
    col_blk = lambda i: (0, i)
    act = pl.BlockSpec((tm, D_MODEL), row_blk)
    wsq = pl.BlockSpec((D_MODEL, D_MODEL), const)
    return pl.pallas_call(
        _mixout_body,
        grid=(T // tm,),
        in_specs=[act, act, act, act, act, wsq, wsq, wsq,
                  pl.BlockSpec((1, D_MODEL), const),
                  pl.BlockSpec((D_MODEL, LANES), const),
                  pl.BlockSpec((N_EXPERTS, 1), const)],
        out_specs=[act,
                   pl.BlockSpec((tm * SLABS, LANES), row_blk),
                   pl.BlockSpec((TOP_K, tm), col_blk),
                   pl.BlockSpec((TOP_K, tm), col_blk)],
        out_shape=[jax.ShapeDtypeStruct((T, D_MODEL), F32),
                   jax.ShapeDtypeStruct((T * SLABS, LANES), BF16),
                   jax.ShapeDtypeStruct((TOP_K, T), jnp.int32),
                   jax.ShapeDtypeStruct((TOP_K, T), F32)],
        scratch_shapes=[pltpu.VMEM((tm * SLABS, LANES), F32)],
        compiler_params=pltpu.CompilerParams(
            dimension_semantics=("arbitrary",), vmem_limit_bytes=VMEM_LIMIT_BYTES),
        name="mixout",
    )(x2, attn, z, sa, sc, wa_bf, wc_bf, wo_bf, g_ffn, wr_t, b_router)


def _rank_body(idx_ref, rank_ref, cnt_ref, before_ref, carry_ref):
    tc = idx_ref.shape[1]
    i = pl.program_id(0)

    @pl.when(i == 0)
    def _():
        carry_ref[...] = jnp.zeros_like(carry_ref)

    idx = idx_ref[...]
    e_iota = lax.broadcasted_iota(jnp.int32, (N_EXPERTS, tc), 0)
    sel = [e_iota == idx[k:k + 1, :] for k in range(TOP_K)]
    member = (sel[0] | sel[1] | sel[2] | sel[3])
    onehot = jnp.where(member, 1.0, 0.0).astype(BF16)
    r = lax.broadcasted_iota(jnp.int32, (tc, tc), 0)
    c = lax.broadcasted_iota(jnp.int32, (tc, tc), 1)
    before = jnp.where(r < c, 1.0, 0.0).astype(BF16)
    carry = carry_ref[:, 0:1]
    before_ref[...] = carry_ref[...].astype(jnp.int32)
    prefix = jnp.dot(onehot, before, preferred_element_type=F32) + carry
    ranks = [jnp.sum(jnp.where(sel[k], prefix, 0.0), axis=0, keepdims=True)
             for k in range(TOP_K)]
    rank_ref[...] = jnp.concatenate(ranks, axis=0).astype(jnp.int32)
    total = carry + jnp.sum(onehot.astype(F32), axis=1, keepdims=True)
    carry_ref[...] = jnp.broadcast_to(total, carry_ref.shape)
    cnt_ref[...] = jnp.broadcast_to(total, cnt_ref.shape).astype(jnp.int32)


def _rank(idx_t):
    T = idx_t.shape[1]
    tc = TC_RANK
    return pl.pallas_call(
        _rank_body,
        grid=(T // tc,),
        in_specs=[pl.BlockSpec((TOP_K, tc), lambda i: (0, i))],
        out_specs=[pl.BlockSpec((TOP_K, tc), lambda i: (0, i)),
                   pl.BlockSpec((N_EXPERTS, LANES), lambda i: (0, 0)),
                   pl.BlockSpec((N_EXPERTS, LANES), lambda i: (i, 0))],
        out_shape=[jax.ShapeDtypeStruct((TOP_K, T), jnp.int32),
                   jax.ShapeDtypeStruct((N_EXPERTS, LANES), jnp.int32),
                   jax.ShapeDtypeStruct((T // tc * N_EXPERTS, LANES), jnp.int32)],
        scratch_shapes=[pltpu.VMEM((N_EXPERTS, LANES), F32)],
        compiler_params=pltpu.CompilerParams(dimension_semantics=("arbitrary",)),
        name="rank",
    )(idx_t)


def _invperm_body(dest_ref, fill_ref, end_ref, src_ref, *, n_assign, n_slots):
    def mark_tile(start):
        def mark(q, carry):
            src_ref[start + q] = 0
            return carry
        lax.fori_loop(0, TM_EXPERT, mark, 0, unroll=16)

    def mark_expert_tail(e, carry):
        mark_tile(fill_ref[e])
        return carry

    lax.fori_loop(0, N_EXPERTS, mark_expert_tail, 0)

    def mark_unused(t, carry):
        mark_tile(t * TM_EXPERT)
        return carry

    lax.fori_loop(end_ref[N_EXPERTS - 1] // TM_EXPERT, n_slots // TM_EXPERT + 1, mark_unused, 0)

    def place(a, carry):
        src_ref[dest_ref[a]] = a
        return carry

    lax.fori_loop(0, n_assign, place, 0, unroll=64)


def _invperm(dest_flat, fill_start, pad_end, n_slots):
    return pl.pallas_call(
        functools.partial(_invperm_body, n_assign=dest_flat.shape[0], n_slots=n_slots),
        grid_spec=pltpu.PrefetchScalarGridSpec(
            num_scalar_prefetch=3,
            grid=(1,),
            in_specs=[],
            out_specs=pl.BlockSpec(memory_space=pltpu.SMEM),
        ),
        out_shape=jax.ShapeDtypeStruct((n_slots + TM_EXPERT,), jnp.int32),
        compiler_params=pltpu.CompilerParams(dimension_semantics=("arbitrary",)),
        name="invperm",
    )(dest_flat, fill_start, pad_end)


def _expert_body(blk_e_ref, nxt_e_ref, n_used_ref, src_ref, hp_ref, wg_hbm, wu_hbm, wd_hbm,
                 bg_ref, bu_ref, bd_ref, y_ref, stage_bf, stage, xbuf, wst, w_bf, wsem, *, n_tok):
    tm = TM_EXPERT
    i = pl.program_id(0)
    n_used = n_used_ref[0]
    slot = i % 2
    e = blk_e_ref[i]
    e_prev = blk_e_ref[jnp.maximum(i - 1, 0)]

    def weight_copies(ex):
        return [pltpu.make_async_copy(w.at[ex], wst.at[m], wsem.at[m])
                for m, w in enumerate((wg_hbm, wu_hbm, wd_hbm))]

    def rows_of(buf):
        return pl.ds(pl.multiple_of(buf * tm, tm), tm)

    def gather_rows(tile, rows):
        for r in rows:
            a = src_ref[tile * tm + r]
            stage_bf[pl.ds(r * SLABS, SLABS), :] = hp_ref[a & (n_tok - 1)]

    def finish_gather(buf):
        stage[...] = stage_bf[...].astype(F32)
        xbuf[rows_of(buf), :] = _load_token_major(stage, tm).astype(BF16)

    @pl.when(i == 0)
    def _():
        for c in weight_copies(e):
            c.start()
        gather_rows(0, range(tm))
        finish_gather(0)

    @pl.when(i >= n_used)
    def _():
        y_ref[...] = jnp.zeros_like(y_ref)

    @pl.when(i < n_used)
    def _():
        @pl.when((i == 0) | (e != e_prev))
        def _():
            for c in weight_copies(e):
                c.wait()
            rows = D_MODEL // SUBLANES
            for m in range(3):
                def cast_rows(j, carry, m=m):
                    sl = pl.ds(pl.multiple_of(j * rows, rows), rows)
                    w_bf[m, sl, :] = wst[m, sl, :].astype(BF16)
                    return carry
                lax.fori_loop(0, SUBLANES, cast_rows, 0)
            nxt_e = nxt_e_ref[i]

            @pl.when(nxt_e >= 0)
            def _():
                for c in weight_copies(nxt_e):
                    c.start()

        x = xbuf[rows_of(slot), :]
        gather_rows(jnp.minimum(i + 1, n_used - 1), range(tm))
        finish_gather(1 - slot)
        y = bd_ref[pl.ds(e, 1), :]
        for c in range(0, D_MODEL, FF_CHUNK):
            cols = pl.ds(c, FF_CHUNK)
            g = jnp.dot(x, w_bf[0, :, cols], preferred_element_type=F32) + bg_ref[pl.ds(e, 1), cols]
            u = jnp.dot(x, w_bf[1, :, cols], preferred_element_type=F32) + bu_ref[pl.ds(e, 1), cols]
            g = jnp.minimum(g, SWIGLU_LIMIT)
            u = jnp.clip(u, -SWIGLU_LIMIT, SWIGLU_LIMIT)
            a = g * jax.nn.sigmoid(SWIGLU_ALPHA * g) * (u + 1.0)
            y = y + jnp.dot(a.astype(BF16), w_bf[2, cols, :], preferred_element_type=F32)
        _store_token_major(y_ref, y)


def _experts(blk_e, nxt_e, n_used, src, hp, w_gate, b_gate, w_up, b_up, w_down, b_down, n_tiles):
    tm = TM_EXPERT
    d_ff = w_gate.shape[2]
    assert d_ff == D_MODEL
    n_tok = hp.shape[0]
    assert n_tok & (n_tok - 1) == 0, "assignment ids are split with a power-of-two mask"
    assert n_tiles >= 2
    const2 = lambda i, *_: (0, 0)
    bias = pl.BlockSpec((N_EXPERTS, D_MODEL), const2)
    hbm = pl.BlockSpec(memory_space=pl.ANY)
    return pl.pallas_call(
        functools.partial(_expert_body, n_tok=n_tok),
        grid_spec=pltpu.PrefetchScalarGridSpec(
            num_scalar_prefetch=4,
            grid=(n_tiles,),
            in_specs=[
                pl.BlockSpec((n_tok, SLABS, LANES), lambda i, *_: (0, 0, 0),
                             pipeline_mode=pl.Buffered(1)),
                hbm, hbm, hbm, bias, bias, bias,
            ],
            out_specs=pl.BlockSpec((tm * SLABS, LANES), lambda i, *_: (i, 0)),
            scratch_shapes=[pltpu.VMEM((tm * SLABS, LANES), BF16),
                            pltpu.VMEM((tm * SLABS, LANES), F32),
                            pltpu.VMEM((2 * tm, D_MODEL), BF16),
                            pltpu.VMEM((3, D_MODEL, D_MODEL), F32),
                            pltpu.VMEM((3, D_MODEL, D_MODEL), BF16),
                            pltpu.SemaphoreType.DMA((3,))],
        ),
        out_shape=jax.ShapeDtypeStruct((n_tiles * tm * SLABS, LANES), F32),
        compiler_params=pltpu.CompilerParams(
            dimension_semantics=("arbitrary",), vmem_limit_bytes=EXPERT_VMEM_LIMIT_BYTES),
        name="experts",
    )(blk_e, nxt_e, n_used, src, hp, w_gate, w_up, w_down, b_gate, b_up, b_down)


def _combine_body(piece_src_ref, n_pieces_ref, srow_ref, wgt_ref, y_hbm, x1_ref, g_ref, o_ref,
                  stage, tokmaj, sem, *, n_tok):
    tm = x1_ref.shape[0]
    b = pl.program_id(0)
    slot = b % 2
    piece = PIECE_ROWS * SLABS
    max_pieces = stage.shape[1] // piece

    def piece_copy(blk, buf, j):
        src = pl.multiple_of(piece_src_ref[blk * max_pieces + j] * SLABS, SLABS)
        dst = pl.multiple_of(j * piece, piece)
        return pltpu.make_async_copy(y_hbm.at[pl.ds(src, piece)], stage.at[buf, pl.ds(dst, piece)],
                                     sem.at[buf])

    def start_pieces(blk, buf):
        def body(j, carry):
            piece_copy(blk, buf, j).start()
            return carry
        lax.fori_loop(0, n_pieces_ref[blk], body, 0)

    def wait_pieces(blk, buf):
        def body(j, carry):
            piece_copy(blk, buf, j).wait()
            return carry
        lax.fori_loop(0, n_pieces_ref[blk], body, 0)

    @pl.when(b == 0)
    def _():
        start_pieces(0, 0)

    @pl.when(b + 1 < pl.num_programs(0))
    def _():
        start_pieces(b + 1, 1 - slot)

    wait_pieces(b, slot)

    def token(t, carry):
        base = b * tm + t
        acc = None
        for k in range(TOP_K):
            row = pl.multiple_of(srow_ref[k * n_tok + base], SLABS)
            term = wgt_ref[k * n_tok + base] * stage[slot, pl.ds(row, SLABS), :]
            acc = term if acc is None else acc + term
        tokmaj[pl.ds(pl.multiple_of(t * SLABS, SLABS), SLABS), :] = acc
        return carry

    lax.fori_loop(0, tm, token, 0, unroll=32)
    o_ref[...] = _rms_scale(x1_ref[...] + _load_token_major(tokmaj, tm), g_ref[...])


def _combine(piece_src, n_pieces, srow, wgt, y_sorted, x1, g_final):
    T = x1.shape[0]
    tm = TM_MOVE
    max_pieces = piece_src.shape[0] // (T // tm)
    row_blk = lambda i, *_: (i, 0)
    return pl.pallas_call(
        functools.partial(_combine_body, n_tok=T),
        grid_spec=pltpu.PrefetchScalarGridSpec(
            num_scalar_prefetch=4,
            grid=(T // tm,),
            in_specs=[
                pl.BlockSpec(memory_space=pl.ANY),
                pl.BlockSpec((tm, D_MODEL), row_blk),
                pl.BlockSpec((1, D_MODEL), lambda i, *_: (0, 0)),
            ],
            out_specs=pl.BlockSpec((tm, D_MODEL), row_blk),
            scratch_shapes=[pltpu.VMEM((2, max_pieces * PIECE_ROWS * SLABS, LANES), F32),
                            pltpu.VMEM((tm * SLABS, LANES), F32),
                            pltpu.SemaphoreType.DMA((2,))],
        ),
        out_shape=jax.ShapeDtypeStruct((T, D_MODEL), F32),
        compiler_params=pltpu.CompilerParams(
            dimension_semantics=("arbitrary",), vmem_limit_bytes=VMEM_LIMIT_BYTES),
        name="combine",
    )(piece_src, n_pieces, srow, wgt, y_sorted, x1, g_final)


def _rope_tables(seq_len):
    half = HEAD_DIM // 2
    inv_freq = ROPE_THETA ** (-np.arange(half, dtype=np.float64) / half)
    ang = np.arange(seq_len, dtype=np.float64)[:, None] * inv_freq[None, :]
    cos = np.cos(ang).astype(np.float32)
    sin = np.sin(ang).astype(np.float32)
    reps = LANES // HEAD_DIM
    cos_t = np.tile(np.concatenate([cos, cos], axis=-1), (1, reps))
    sin_t = np.tile(np.concatenate([-sin, sin], axis=-1), (1, reps))
    return jnp.asarray(cos_t), jnp.asarray(sin_t)


def _router_hi_lo(w_router):
    hi = w_router.astype(BF16)
    lo = (w_router - hi.astype(F32)).astype(BF16)
    pad = jnp.zeros((w_router.shape[0], LANES - 2 * N_EXPERTS), BF16)
    return jnp.concatenate([hi, lo, pad], axis=1)


def _layer(x2, batch, seq_len, g_mix, w_in, b_in, sinks, w_conv, w_attn_o, w_conv_o, w_out,
           g_ffn, w_router, b_router, w_gate, b_gate, w_up, b_up, w_down, b_down, g_out):
    T = x2.shape[0]
    cos_t, sin_t = _rope_tables(seq_len)
    q, k, v, z, sa, sc = _inproj(x2, g_mix[None, :], w_in.astype(BF16), b_in[None, :],
                                 cos_t, sin_t, w_conv, seq_len)
    attn = _attention(q, k, v, sinks, batch, seq_len)
    x1, hp, idx_t, wgt_t = _mixout(
        x2, attn, z, sa, sc, w_attn_o.astype(BF16), w_conv_o.astype(BF16), w_out.astype(BF16),
        g_ffn[None, :], _router_hi_lo(w_router), b_router[:, None])

    rank_t, cnt, before = _rank(idx_t)
    counts = cnt[:, 0]
    padded = (counts + TM_EXPERT - 1) // TM_EXPERT * TM_EXPERT
    pad_end = jnp.cumsum(padded)
    pad_start = pad_end - padded
    n_tiles = (T * TOP_K) // TM_EXPERT + N_EXPERTS + 1
    tile_row = jnp.arange(n_tiles, dtype=jnp.int32) * TM_EXPERT
    blk_e = jnp.minimum(jnp.sum(pad_end[None, :] <= tile_row[:, None], axis=1),
                        N_EXPERTS - 1).astype(jnp.int32)
    n_used = (pad_end[-1:] // TM_EXPERT).astype(jnp.int32)
    e_ids = jnp.arange(N_EXPERTS, dtype=jnp.int32)
    dest = rank_t + jnp.sum(
        jnp.where(idx_t[:, :, None] == e_ids[None, None, :], pad_start[None, None, :], 0), axis=-1)
    dest_flat = dest.reshape(-1).astype(jnp.int32)
    fill_start = (pad_start + counts).astype(jnp.int32)

    group_end = jnp.sum(jnp.where(blk_e[:, None] == e_ids[None, :], pad_end[None, :], 0), axis=1)
    nxt_e = jnp.where(
        group_end < pad_end[-1],
        jnp.minimum(jnp.sum(pad_end[None, :] <= group_end[:, None], axis=1), N_EXPERTS - 1),
        -1).astype(jnp.int32)

    src = _invperm(dest_flat, fill_start, pad_end.astype(jnp.int32), n_tiles * TM_EXPERT)
    y_sorted = _experts(blk_e, nxt_e, n_used, src, hp.reshape(T, SLABS, LANES),
                        w_gate, b_gate, w_up, b_up, w_down, b_down, n_tiles)

    n_blocks = T // TM_MOVE
    max_pieces = TOP_K * TM_MOVE // PIECE_ROWS + N_EXPERTS
    lo = before.reshape(n_blocks, N_EXPERTS, LANES)[:, :, 0]
    cnt_be = jnp.concatenate([lo[1:], counts[None, :]], axis=0) - lo
    pieces_be = (cnt_be + PIECE_ROWS - 1) // PIECE_ROWS
    pend = jnp.cumsum(pieces_be, axis=1)
    pbase = pend - pieces_be
    n_pieces = pend[:, -1].astype(jnp.int32)
    j = jnp.arange(max_pieces, dtype=jnp.int32)
    e_of_piece = jnp.minimum(jnp.sum(pend[:, None, :] <= j[None, :, None], axis=2), N_EXPERTS - 1)
    onehot = e_of_piece[:, :, None] == e_ids[None, None, :]
    pick = lambda tab: jnp.sum(jnp.where(onehot, tab[:, None, :], 0), axis=2)
    piece_src = pick(pad_start[None, :] + lo) + PIECE_ROWS * (j[None, :] - pick(pbase))
    piece_src = jnp.where(j[None, :] < n_pieces[:, None], piece_src, 0).astype(jnp.int32)
    tab = jnp.broadcast_to((PIECE_ROWS * pbase - lo)[:, None, :],
                           (n_blocks, TM_MOVE, N_EXPERTS)).reshape(T, N_EXPERTS)
    srow = rank_t + jnp.sum(jnp.where(idx_t[:, :, None] == e_ids[None, None, :], tab[None], 0), axis=-1)
    return _combine(piece_src.reshape(-1), n_pieces, (srow * SLABS).reshape(-1).astype(jnp.int32),
                    wgt_t.reshape(-1), y_sorted, x1, g_out[None, :])


def kernel(x, g_mix, w_in, b_in, sinks, w_conv, w_attn_o, w_conv_o, w_out, g_ffn, w_router,
           b_router, w_gate, b_gate, w_up, b_up, w_down, b_down, g_final):
    batch, seq_len, d = x.shape
    depth = g_mix.shape[0]
    assert depth == 1, "the final norm is fused into the single layer's combine step"
    x2 = x.reshape(batch * seq_len, d)
    out = _layer(x2, batch, seq_len, g_mix[0], w_in[0], b_in[0], sinks[0], w_conv[0],
                 w_attn_o[0], w_conv_o[0], w_out[0], g_ffn[0], w_router[0], b_router[0],
                 w_gate[0], b_gate[0], w_up[0], b_up[0], w_down[0], b_down[0], g_final)
    return out.reshape(batch, seq_len, d)
```
